```python
import jax, jax.numpy as jnp
from jax import lax
import numpy as np

D_MODEL = 2048
BATCH = 1
SEQ = 8192
DEPTH = 4

GRID_W = 64
CTX_LEN = 256
N_HEADS = 8
N_KV_HEADS = 2
GROUP = N_HEADS // N_KV_HEADS
D_HEAD = 128
ATTN_W = N_HEADS * D_HEAD
KV_W = N_KV_HEADS * D_HEAD
WINDOW = 128
BLOCK = 128
FOURIER_W = D_MODEL // 4
N_FOURIER_GROUPS = 4
FOURIER_GROUP_W = FOURIER_W // N_FOURIER_GROUPS
POOL_WINDOWS = (2, 4, 8, 16)
POOL_W = D_MODEL // 4
POOL_GROUP_W = POOL_W // len(POOL_WINDOWS)
POOL_OUT_GROUP_W = D_MODEL // len(POOL_WINDOWS)
N_BRANCHES = 3
IN_W = ATTN_W + 2 * KV_W + FOURIER_W + POOL_W + N_BRANCHES * D_MODEL
D_FF = 5632
N_MOD = 9
ROPE_BASE = 10000.0
EPS = 1e-6
NEG_INF = -1e30

kernel_name = "hybrid_gated_dit_block"


def rms_norm(x, gain):
    xf = x.astype(jnp.float32)
    y = xf * lax.rsqrt(jnp.mean(xf * xf, axis=-1, keepdims=True) + EPS)
    return (y * gain.astype(jnp.float32)).astype(x.dtype)


def modulate(x, gain, shift, scale):
    return rms_norm(x, gain) * (1 + scale) + shift


def swiglu(h, wg, wu, wd):
    return (jax.nn.silu(h @ wg) * (h @ wu)) @ wd


def ffn_sublayer(s, gain, shift, scale, gate, wg, wu, wd):
    return s + 0.5 * gate * swiglu(modulate(s, gain, shift, scale), wg, wu, wd)


def axial_rope_tables(S):
    rows_n = S // GRID_W
    rows = jnp.repeat(jnp.arange(rows_n), GRID_W).astype(jnp.float32)
    cols = jnp.tile(jnp.arange(GRID_W), rows_n).astype(jnp.float32)
    n_freq = D_HEAD // 4
    freqs = ROPE_BASE ** (-jnp.arange(n_freq, dtype=jnp.float32) / n_freq)
    ang = jnp.stack([rows[:, None] * freqs, cols[:, None] * freqs], axis=1)
    return jnp.cos(ang), jnp.sin(ang)


def apply_rope(x, cos, sin):
    B, S, H, _ = x.shape
    xr = x.astype(jnp.float32).reshape(B, S, H, 2, 2, D_HEAD // 4)
    x1, x2 = xr[..., 0, :], xr[..., 1, :]
    c = cos[None, :, None]
    s = sin[None, :, None]
    out = jnp.stack([x1 * c - x2 * s, x2 * c + x1 * s], axis=-2)
    return out.reshape(B, S, H, D_HEAD).astype(x.dtype)


def kv_heads(k, v, k_gain):
    B, S = k.shape[:2]
    k = rms_norm(k.reshape(B, S, N_KV_HEADS, D_HEAD), k_gain)
    return k, v.reshape(B, S, N_KV_HEADS, D_HEAD)


def project(h, w_in, q_gain, k_gain):
    B, S = h.shape[:2]
    z = h @ w_in
    o1 = ATTN_W
    o2 = o1 + KV_W
    o3 = o2 + KV_W
    o4 = o3 + FOURIER_W
    o5 = o4 + POOL_W
    q, k, v, uf, up, g = jnp.split(z, [o1, o2, o3, o4, o5], axis=-1)
    q = rms_norm(q.reshape(B, S, N_HEADS, D_HEAD), q_gain)
    k, v = kv_heads(k, v, k_gain)
    return q, k, v, uf, up, g


def sink_softmax(logits, sink):
    sink_col = jnp.broadcast_to(sink.astype(jnp.float32).reshape(N_KV_HEADS, GROUP, 1, 1),
                                logits.shape[:-1] + (1,))
    p = jax.nn.softmax(jnp.concatenate([logits, sink_col], axis=-1), axis=-1)
    return p[..., :-1]


def window_attention(q, k, v, kc, vc, sink):
    B, S = q.shape[:2]
    nb = S // BLOCK
    scale = D_HEAD ** -0.5
    qb = q.reshape(B, nb, BLOCK, N_KV_HEADS, GROUP, D_HEAD)

    def bands(t):
        tp = jnp.pad(t, ((0, 0), (BLOCK, BLOCK), (0, 0), (0, 0)))
        return jnp.concatenate(
            [tp[:, i * BLOCK:i * BLOCK + S].reshape(B, nb, BLOCK, N_KV_HEADS, D_HEAD) for i in range(3)],
            axis=2)

    kb, vb = bands(k), bands(v)
    s_loc = jnp.einsum('bnqkgd,bnskd->bnkgqs', qb, kb).astype(jnp.float32) * scale
    s_ctx = jnp.einsum('bnqkgd,bckd->bnkgqc', qb, kc).astype(jnp.float32) * scale
    qi = jnp.arange(BLOCK)[:, None]
    kj = jnp.arange(3 * BLOCK)[None, :]
    key_pos = jnp.arange(nb)[:, None, None] * BLOCK - BLOCK + kj[None]
    mask = (jnp.abs(kj - BLOCK - qi) <= WINDOW)[None] & (key_pos >= 0) & (key_pos < S)
    s_loc = jnp.where(mask[None, :, None, None], s_loc, NEG_INF)
    p = sink_softmax(jnp.concatenate([s_loc, s_ctx], axis=-1), sink).astype(v.dtype)
    p_loc, p_ctx = p[..., :3 * BLOCK], p[..., 3 * BLOCK:]
    out = (jnp.einsum('bnkgqs,bnskd->bnqkgd', p_loc, vb)
           + jnp.einsum('bnkgqc,bckd->bnqkgd', p_ctx, vc))
    return out.reshape(B, S, ATTN_W)


def context_attention(qc, kc, vc, sink):
    B, C = qc.shape[:2]
    scale = D_HEAD ** -0.5
    qg = qc.reshape(B, C, N_KV_HEADS, GROUP, D_HEAD)
    s = jnp.einsum('bqkgd,bskd->bkgqs', qg, kc).astype(jnp.float32) * scale
    p = sink_softmax(s, sink).astype(vc.dtype)
    return jnp.einsum('bkgqs,bskd->bqkgd', p, vc).reshape(B, C, ATTN_W)


def fourier_mix(uf):
    B, S = uf.shape[:2]
    u = uf.astype(jnp.float32).reshape(B, S, N_FOURIER_GROUPS, FOURIER_GROUP_W)
    y = jnp.real(jnp.fft.fft2(u, axes=(1, 3), norm='ortho'))
    return y.reshape(B, S, FOURIER_W).astype(uf.dtype)


def pool_mix(up, w_pool, pool_scale):
    B, S = up.shape[:2]
    u = up.astype(jnp.float32).reshape(B, S, len(POOL_WINDOWS), POOL_GROUP_W)
    cs = jnp.pad(jnp.cumsum(u, axis=1), ((0, 0), (1, 0), (0, 0), (0, 0)))
    t = jnp.arange(S)
    outs = []
    for gi, w in enumerate(POOL_WINDOWS):
        lo = jnp.clip(t - w // 2, 0, S)
        hi = jnp.clip(t + (w - w // 2), 0, S)
        cs_g = cs[:, :, gi]
        mean = (cs_g[:, hi] - cs_g[:, lo]) / (hi - lo).astype(jnp.float32)[None, :, None]
        outs.append(mean - u[:, :, gi])
    pooled = jnp.stack(outs, axis=2).astype(up.dtype)
    y = jnp.einsum('bsgc,gcd->bsgd', pooled, w_pool).reshape(B, S, D_MODEL)
    return y * pool_scale


def merge(a, uf, up, g, w_attn_o, w_fourier, w_pool, pool_scale, w_out):
    ga, gf, gp = jnp.split(jax.nn.sigmoid(g), N_BRANCHES, axis=-1)
    y = (ga * (a @ w_attn_o)
         + gf * (fourier_mix(uf) @ w_fourier)
         + gp * pool_mix(up, w_pool, pool_scale))
    return y @ w_out


def setup_inputs(seed: int = 0) -> dict:
    key = jax.random.key(seed)
    ks = jax.random.split(key, 19)
    f32 = jnp.float32

    def nrm(k, shape, scale):
        return jax.random.normal(k, shape, f32) * scale

    return {
        "x": nrm(ks[0], (BATCH, SEQ, D_MODEL), 1.0),
        "c": nrm(ks[1], (BATCH, D_MODEL), 1.0),
        "ctx": nrm(ks[2], (BATCH, CTX_LEN, D_MODEL), 1.0),
        "c_ctx": nrm(ks[3], (D_MODEL,), 1.0),
        "w_ada": nrm(ks[4], (DEPTH, D_MODEL, N_MOD * D_MODEL), 0.5 * D_MODEL ** -0.5),
        "b_ada": nrm(ks[5], (DEPTH, N_MOD * D_MODEL), 0.01),
        "norm_w": 1.0 + nrm(ks[6], (DEPTH, 3, D_MODEL), 0.05),
        "ffn_w_gate": nrm(ks[7], (DEPTH, 2, D_MODEL, D_FF), D_MODEL ** -0.5),
        "ffn_w_up": nrm(ks[8], (DEPTH, 2, D_MODEL, D_FF), D_MODEL ** -0.5),
        "ffn_w_down": nrm(ks[9], (DEPTH, 2, D_FF, D_MODEL), D_FF ** -0.5),
        "w_in": nrm(ks[10], (DEPTH, D_MODEL, IN_W), D_MODEL ** -0.5),
        "q_gain": 1.0 + nrm(ks[11], (DEPTH, D_HEAD), 0.05),
        "k_gain": 1.0 + nrm(ks[12], (DEPTH, D_HEAD), 0.05),
        "sink": nrm(ks[13], (DEPTH, N_HEADS), 0.5),
        "w_attn_o": nrm(ks[14], (DEPTH, ATTN_W, D_MODEL), ATTN_W ** -0.5),
        "w_fourier": nrm(ks[15], (DEPTH, FOURIER_W, D_MODEL), FOURIER_W ** -0.5),
        "w_pool": nrm(ks[16], (DEPTH, len(POOL_WINDOWS), POOL_GROUP_W, POOL_OUT_GROUP_W), POOL_GROUP_W ** -0.5),
        "pool_scale": 1.0 + nrm(ks[17], (DEPTH, D_MODEL), 0.05),
        "w_out": nrm(ks[18], (DEPTH, D_MODEL, D_MODEL), D_MODEL ** -0.5),
    }


def reference(x, c, ctx, c_ctx, w_ada, b_ada, norm_w, ffn_w_gate, ffn_w_up, ffn_w_down,
              w_in, q_gain, k_gain, sink, w_attn_o, w_fourier, w_pool, pool_scale, w_out):
    B, S, _ = x.shape
    cos, sin = axial_rope_tables(S)
    cond_x = jax.nn.silu(c)
    cond_c = jax.nn.silu(c_ctx)
    for l in range(DEPTH):
        last = l == DEPTH - 1
        mod_x = (cond_x @ w_ada[l] + b_ada[l]).reshape(B, N_MOD, 1, D_MODEL)
        mod_c = (cond_c @ w_ada[l] + b_ada[l]).reshape(N_MOD, 1, D_MODEL)
        mx = [mod_x[:, i] for i in range(N_MOD)]
        mc = [mod_c[i] for i in range(N_MOD)]

        x = ffn_sublayer(x, norm_w[l, 0], mx[0], mx[1], mx[2],
                         ffn_w_gate[l, 0], ffn_w_up[l, 0], ffn_w_down[l, 0])
        ctx = ffn_sublayer(ctx, norm_w[l, 0], mc[0], mc[1], mc[2],
                           ffn_w_gate[l, 0], ffn_w_up[l, 0], ffn_w_down[l, 0])

        hx = modulate(x, norm_w[l, 1], mx[3], mx[4])
        hc = modulate(ctx, norm_w[l, 1], mc[3], mc[4])
        qx, kx, vx, ufx, upx, gx = project(hx, w_in[l], q_gain[l], k_gain[l])
        qx = apply_rope(qx, cos, sin)
        kx = apply_rope(kx, cos, sin)
        if last:
            zkv = hc @ w_in[l][:, ATTN_W:ATTN_W + 2 * KV_W]
            kc, vc = kv_heads(zkv[..., :KV_W], zkv[..., KV_W:], k_gain[l])
        else:
            qc, kc, vc, ufc, upc, gc = project(hc, w_in[l], q_gain[l], k_gain[l])
        ax = window_attention(qx, kx, vx, kc, vc, sink[l])
        mix_x = merge(ax, ufx, upx, gx, w_attn_o[l], w_fourier[l], w_pool[l], pool_scale[l], w_out[l])
        x = x + mx[5] * mix_x
        if not last:
            ac = context_attention(qc, kc, vc, sink[l])
            mix_c = merge(ac, ufc, upc, gc, w_attn_o[l], w_fourier[l], w_pool[l], pool_scale[l], w_out[l])
            ctx = ctx + mc[5] * mix_c

        x = ffn_sublayer(x, norm_w[l, 2], mx[6], mx[7], mx[8],
                         ffn_w_gate[l, 1], ffn_w_up[l, 1], ffn_w_down[l, 1])
        if not last:
            ctx = ffn_sublayer(ctx, norm_w[l, 2], mc[6], mc[7], mc[8],
                               ffn_w_gate[l, 1], ffn_w_up[l, 1], ffn_w_down[l, 1])
    return x
```

```python
import functools

import numpy as np
import jax
import jax.numpy as jnp
from jax import lax
from jax.experimental import pallas as pl
from jax.experimental.pallas import tpu as pltpu

D_MODEL = 2048
DEPTH = 4
GRID_W = 64
N_HEADS = 8
N_KV_HEADS = 2
GROUP = N_HEADS // N_KV_HEADS
D_HEAD = 128
ATTN_W = N_HEADS * D_HEAD
KV_W = N_KV_HEADS * D_HEAD
WINDOW = 128
BLOCK = 128
FOURIER_W = D_MODEL // 4
N_FOURIER_GROUPS = 4
FOURIER_GROUP_W = FOURIER_W // N_FOURIER_GROUPS
POOL_WINDOWS = (2, 4, 8, 16)
POOL_W = D_MODEL // 4
POOL_GROUP_W = POOL_W // len(POOL_WINDOWS)
POOL_OUT_GROUP_W = D_MODEL // len(POOL_WINDOWS)
N_BRANCHES = 3
GATE_W = N_BRANCHES * D_MODEL
IN_W = ATTN_W + 2 * KV_W + FOURIER_W + POOL_W + GATE_W
D_FF = 5632
N_MOD = 9
ROPE_BASE = 10000.0
EPS = 1e-6
NEG_INF = -1e30

BF16 = jnp.bfloat16
F32 = jnp.float32

VMEM_LIMIT_BYTES = 56 * 1024 * 1024
POOL_HALO = 16
COL_TILE = 512
DFT_ROWS = GRID_W


def _params(*semantics):
    return pltpu.CompilerParams(dimension_semantics=semantics, vmem_limit_bytes=VMEM_LIMIT_BYTES)


def _modulate(x, gain, shift, scale):
    y = x * lax.rsqrt(jnp.mean(x * x, axis=-1, keepdims=True) + EPS)
    return (y * gain) * (1 + scale) + shift


def _ada_kernel(cond_ref, w_ref, b_ref, o_ref):
    h = jax.nn.silu(cond_ref[...]).astype(BF16)
    o_ref[0] = jnp.dot(h, w_ref[0].astype(BF16), preferred_element_type=F32) + b_ref[0]


def _ada_call(cond, w_ada, b_ada):
    depth, d, n = w_ada.shape
    tn = 1024
    return pl.pallas_call(
        _ada_kernel,
        grid=(depth, n // tn),
        in_specs=[
            pl.BlockSpec((8, d), lambda l, j: (0, 0)),
            pl.BlockSpec((1, d, tn), lambda l, j: (l, 0, j)),
            pl.BlockSpec((1, 1, tn), lambda l, j: (l, 0, j)),
        ],
        out_specs=pl.BlockSpec((1, 8, tn), lambda l, j: (l, 0, j)),
        out_shape=jax.ShapeDtypeStruct((depth, 8, n), F32),
        compiler_params=_params("arbitrary", "arbitrary"),
        name="adaln",
    )(cond, w_ada, b_ada.reshape(depth, 1, n))


def _ffn_kernel(x_ref, nw_ref, sh_ref, sc_ref, gt_ref, wg_ref, wu_ref, wd_ref, o_ref, h_ref):
    j = pl.program_id(1)

    @pl.when(j == 0)
    def _():
        h_ref[...] = _modulate(x_ref[...], nw_ref[...], sh_ref[...], sc_ref[...]).astype(BF16)
        o_ref[...] = jnp.zeros_like(o_ref)

    h = h_ref[...]
    g = jnp.dot(h, wg_ref[...], preferred_element_type=F32)
    u = jnp.dot(h, wu_ref[...], preferred_element_type=F32)
    a = (jax.nn.silu(g) * u).astype(BF16)
    o_ref[...] += jnp.dot(a, wd_ref[...], preferred_element_type=F32)

    @pl.when(j == pl.num_programs(1) - 1)
    def _():
        o_ref[...] = x_ref[...] + (0.5 * gt_ref[...]) * o_ref[...]


def _ffn_call(x, nw, shift, scale, gate, wg, wu, wd, l, which, tm):
    s, d = x.shape
    tf = 512
    row = pl.BlockSpec((1, d), lambda i, j: (0, 0))
    return pl.pallas_call(
        _ffn_kernel,
        grid=(s // tm, D_FF // tf),
        in_specs=[
            pl.BlockSpec((tm, d), lambda i, j: (i, 0)),
            row, row, row, row,
            pl.BlockSpec((None, None, d, tf), lambda i, j: (l, which, 0, j)),
            pl.BlockSpec((None, None, d, tf), lambda i, j: (l, which, 0, j)),
            pl.BlockSpec((None, None, tf, d), lambda i, j: (l, which, j, 0)),
        ],
        out_specs=pl.BlockSpec((tm, d), lambda i, j: (i, 0)),
        out_shape=jax.ShapeDtypeStruct((s, d), F32),
        scratch_shapes=[pltpu.VMEM((tm, d), BF16)],
        compiler_params=_params("arbitrary", "arbitrary"),
        name="ffn",
    )(x, nw, shift, scale, gate, wg, wu, wd)


Q_TILES = ATTN_W // COL_TILE
KV_TILE = Q_TILES
UF_TILE = KV_TILE + 1
UP_TILE = UF_TILE + 1
G_TILE0 = UP_TILE + 1
N_COL_TILES = IN_W // COL_TILE


def _norm_rope(z, gain, cos, sin_signed):
    y = z * lax.rsqrt(jnp.mean(z * z, axis=-1, keepdims=True) + EPS) * gain
    lane = lax.broadcasted_iota(jnp.int32, y.shape, 1)
    first_half = (lane & (D_HEAD // 4)) == 0
    partner = jnp.where(first_half, pltpu.roll(y, D_HEAD - D_HEAD // 4, 1), pltpu.roll(y, D_HEAD // 4, 1))
    return y * cos + partner * sin_signed


def _inproj_kernel(x_ref, nw_ref, sh_ref, sc_ref, qg_ref, kg_ref, cos_ref, sin_ref, w_ref,
                   q_ref, kv_ref, uf_ref, up_ref, g_ref, h_ref):
    j = pl.program_id(1)

    @pl.when(j == 0)
    def _():
        h_ref[...] = _modulate(x_ref[...], nw_ref[...], sh_ref[...], sc_ref[...]).astype(BF16)

    z = jnp.dot(h_ref[...], w_ref[...], preferred_element_type=F32)

    @pl.when(j < Q_TILES)
    def _():
        cos, sin = cos_ref[...], sin_ref[...]
        for hh in range(COL_TILE // D_HEAD):
            sl = slice(hh * D_HEAD, (hh + 1) * D_HEAD)
            q = _norm_rope(z[:, sl], qg_ref[...], cos, sin) * (D_HEAD ** -0.5)
            q_ref[:, sl] = q.astype(BF16)

    @pl.when(j == KV_TILE)
    def _():
        cos, sin = cos_ref[...], sin_ref[...]
        for hh in range(N_KV_HEADS):
            sl = slice(hh * D_HEAD, (hh + 1) * D_HEAD)
            kv_ref[:, sl] = _norm_rope(z[:, sl], kg_ref[...], cos, sin).astype(BF16)
        kv_ref[:, KV_W:] = z[:, KV_W:].astype(BF16)

    @pl.when(j == UF_TILE)
    def _():
        uf_ref[...] = z.astype(BF16)

    @pl.when(j == UP_TILE)
    def _():
        up_ref[...] = z.astype(BF16)

    @pl.when(j >= G_TILE0)
    def _():
        g_ref[...] = z.astype(BF16)


def _inproj_call(x, nw, shift, scale, qg, kg, cos, sin, w_in, l, tm):
    s, d = x.shape
    ct = COL_TILE
    row = pl.BlockSpec((1, d), lambda i, j: (0, 0))
    hrow = pl.BlockSpec((1, D_HEAD), lambda i, j: (0, 0))
    tab = pl.BlockSpec((tm, D_HEAD), lambda i, j: (i, 0))
    n_g = GATE_W // ct
    return pl.pallas_call(
        _inproj_kernel,
        grid=(s // tm, N_COL_TILES),
        in_specs=[
            pl.BlockSpec((tm, d), lambda i, j: (i, 0)),
            row, row, row, hrow, hrow, tab, tab,
            pl.BlockSpec((None, d, ct), lambda i, j: (l, 0, j)),
        ],
        out_specs=[
            pl.BlockSpec((tm, ct), lambda i, j: (i, jnp.minimum(j, Q_TILES - 1))),
            pl.BlockSpec((tm, ct), lambda i, j: (i, 0)),
            pl.BlockSpec((tm, ct), lambda i, j: (i, 0)),
            pl.BlockSpec((tm, ct), lambda i, j: (i, 0)),
            pl.BlockSpec((tm, ct), lambda i, j: (i, jnp.clip(j - G_TILE0, 0, n_g - 1))),
        ],
        out_shape=[
            jax.ShapeDtypeStruct((s, ATTN_W), BF16),
            jax.ShapeDtypeStruct((s, 2 * KV_W), BF16),
            jax.ShapeDtypeStruct((s, FOURIER_W), BF16),
            jax.ShapeDtypeStruct((s, POOL_W), BF16),
            jax.ShapeDtypeStruct((s, GATE_W), BF16),
        ],
        scratch_shapes=[pltpu.VMEM((tm, d), BF16)],
        compiler_params=_params("arbitrary", "arbitrary"),
        name="inproj",
    )(x, nw, shift, scale, qg, kg, cos, sin, w_in)


def _qk(q, k_ref):
    return lax.dot_general(q, k_ref[...], (((1,), (1,)), ((), ())), preferred_element_type=F32)


def _sink_column(sink_ref, kvh, rows):
    head = lax.broadcasted_iota(jnp.int32, (GROUP * rows, 1), 0) // rows
    col = jnp.full((GROUP * rows, 1), sink_ref[kvh * GROUP], F32)
    for hh in range(1, GROUP):
        col = jnp.where(head == hh, sink_ref[kvh * GROUP + hh], col)
    return col


def _stack_heads(q_ref):
    return jnp.concatenate([q_ref[:, hh * D_HEAD:(hh + 1) * D_HEAD] for hh in range(GROUP)], axis=0)


def _softmax_pv(scores, values, sink_col):
    m = sink_col
    for s in scores:
        m = jnp.maximum(m, jnp.max(s, axis=-1, keepdims=True))
    denom = jnp.exp(sink_col - m)
    acc = None
    for s, v_ref in zip(scores, values):
        e = jnp.exp(s - m)
        denom = denom + jnp.sum(e, axis=-1, keepdims=True)
        pv = jnp.dot(e.astype(BF16), v_ref[...], preferred_element_type=F32)
        acc = pv if acc is None else acc + pv
    return acc / denom


def _unstack_heads(out, o_ref, rows):
    for hh in range(GROUP):
        o_ref[:, hh * D_HEAD:(hh + 1) * D_HEAD] = out[hh * rows:(hh + 1) * rows].astype(o_ref.dtype)


def _window_attn_kernel(sink_ref, q_ref, kp_ref, kc_ref, kn_ref, vp_ref, vc_ref, vn_ref,
                        kx_ref, vx_ref, o_ref):
    kvh, i = pl.program_id(0), pl.program_id(1)
    nb = pl.num_programs(1)
    q = _stack_heads(q_ref)
    shape = (GROUP * BLOCK, BLOCK)
    qi = lax.broadcasted_iota(jnp.int32, shape, 0) % BLOCK
    kj = lax.broadcasted_iota(jnp.int32, shape, 1)
    s_prev = jnp.where((kj >= qi) & (i > 0), _qk(q, kp_ref), NEG_INF)
    s_cur = _qk(q, kc_ref)
    s_next = jnp.where((kj <= qi) & (i < nb - 1), _qk(q, kn_ref), NEG_INF)
    s_ctx = _qk(q, kx_ref)
    out = _softmax_pv([s_prev, s_cur, s_next, s_ctx], [vp_ref, vc_ref, vn_ref, vx_ref],
                      _sink_column(sink_ref, kvh, BLOCK))
    _unstack_heads(out, o_ref, BLOCK)


def _window_attn_call(q, kv, kv_ctx, sink):
    s = q.shape[0]
    c = kv_ctx.shape[0]
    nb = s // BLOCK
    qw = GROUP * D_HEAD
    kblk = lambda off, col: pl.BlockSpec(
        (BLOCK, D_HEAD), lambda h, i: (jnp.clip(i + off, 0, nb - 1), col * N_KV_HEADS + h))
    cblk = lambda col: pl.BlockSpec((c, D_HEAD), lambda h, i: (0, col * N_KV_HEADS + h))
    return pl.pallas_call(
        _window_attn_kernel,
        grid=(N_KV_HEADS, nb),
        in_specs=[
            pl.BlockSpec(memory_space=pltpu.SMEM),
            pl.BlockSpec((BLOCK, qw), lambda h, i: (i, h)),
            kblk(-1, 0), kblk(0, 0), kblk(1, 0),
            kblk(-1, 1), kblk(0, 1), kblk(1, 1),
            cblk(0), cblk(1),
        ],
        out_specs=pl.BlockSpec((BLOCK, qw), lambda h, i: (i, h)),
        out_shape=jax.ShapeDtypeStruct((s, ATTN_W), BF16),
        compiler_params=_params("arbitrary", "arbitrary"),
        name="window_attn",
    )(sink, q, kv, kv, kv, kv, kv, kv, kv_ctx, kv_ctx)


def _ctx_attn_kernel(sink_ref, q_ref, k_ref, v_ref, o_ref):
    kvh = pl.program_id(0)
    rows = q_ref.shape[0]
    q = _stack_heads(q_ref)
    out = _softmax_pv([_qk(q, k_ref)], [v_ref], _sink_column(sink_ref, kvh, rows))
    _unstack_heads(out, o_ref, rows)


def _ctx_attn_call(q, kv, sink):
    c = q.shape[0]
    qw = GROUP * D_HEAD
    return pl.pallas_call(
        _ctx_attn_kernel,
        grid=(N_KV_HEADS,),
        in_specs=[
            pl.BlockSpec(memory_space=pltpu.SMEM),
            pl.BlockSpec((c, qw), lambda h: (0, h)),
            pl.BlockSpec((c, D_HEAD), lambda h: (0, h)),
            pl.BlockSpec((c, D_HEAD), lambda h: (0, N_KV_HEADS + h)),
        ],
        out_specs=pl.BlockSpec((c, qw), lambda h: (0, h)),
        out_shape=jax.ShapeDtypeStruct((c, ATTN_W), BF16),
        compiler_params=_params("arbitrary"),
        name="ctx_attn",
    )(sink, q, kv, kv)


def _dft_tables(s_len):
    n1 = DFT_ROWS if s_len > 1024 else 1
    n2 = s_len // n1
    k1 = np.arange(n1)
    ang1 = 2.0 * np.pi * ((k1[:, None] * k1[None, :]) % n1) / n1
    w1 = np.concatenate([np.cos(ang1), -np.sin(ang1)], axis=0)
    k2 = np.arange(n2)
    phase = (k2[None, None, :] * (k1[:, None, None] + n1 * k2[None, :, None])) % s_len
    ang2 = 2.0 * np.pi * phase / s_len
    gr, gi = np.cos(ang2), -np.sin(ang2)
    g = np.concatenate([np.concatenate([gr, -gi], axis=2),
                        np.concatenate([gi, gr], axis=2)], axis=1)
    if n1 == 1:
        g = g[:, :, :n2]
    c = np.arange(FOURIER_GROUP_W)
    angc = 2.0 * np.pi * ((c[:, None] * c[None, :]) % FOURIER_GROUP_W) / FOURIER_GROUP_W
    norm = 1.0 / np.sqrt(float(s_len) * FOURIER_GROUP_W)
    return (w1.astype(np.float32), g.astype(np.float32),
            (np.cos(angc) * norm).astype(np.float32), (np.sin(angc) * norm).astype(np.float32))


def _dft1_kernel(w_ref, u_ref, o_ref):
    o_ref[...] = jnp.dot(w_ref[...], u_ref[...], preferred_element_type=F32).astype(o_ref.dtype)


def _dft1_call(w1, uf):
    s, cw = uf.shape
    n1 = DFT_ROWS
    cols = (s // n1) * cw
    tn = 4096
    return pl.pallas_call(
        _dft1_kernel,
        grid=(cols // tn,),
        in_specs=[pl.BlockSpec((2 * n1, n1), lambda j: (0, 0)),
                  pl.BlockSpec((n1, tn), lambda j: (0, j))],
        out_specs=pl.BlockSpec((2 * n1, tn), lambda j: (0, j)),
        out_shape=jax.ShapeDtypeStruct((2 * n1, cols), BF16),
        compiler_params=_params("arbitrary"),
        name="dft_rows",
    )(w1, uf.reshape(n1, cols))


def _dft2_kernel(g_ref, *refs):
    x_refs, (cc_ref, sc_ref, o_ref) = refs[:-3], refs[-3:]
    x = x_refs[0][...] if len(x_refs) == 1 else jnp.concatenate([r[...] for r in x_refs], axis=0)
    y = jnp.dot(g_ref[0], x, preferred_element_type=F32)
    n = y.shape[0] // 2
    yr, yi = y[:n].astype(BF16), y[n:].astype(BF16)
    for gi in range(N_FOURIER_GROUPS):
        sl = slice(gi * FOURIER_GROUP_W, (gi + 1) * FOURIER_GROUP_W)
        o = (jnp.dot(yr[:, sl], cc_ref[...], preferred_element_type=F32)
             + jnp.dot(yi[:, sl], sc_ref[...], preferred_element_type=F32))
        o_ref[:, sl] = o.astype(o_ref.dtype)


def _dft2_call(g, x1, cc, sc, s_len):
    n1 = DFT_ROWS
    n2 = s_len // n1
    cw = FOURIER_W
    gw = FOURIER_GROUP_W
    out = pl.pallas_call(
        _dft2_kernel,
        grid=(n1,),
        in_specs=[pl.BlockSpec((1, 2 * n2, 2 * n2), lambda k: (k, 0, 0)),
                  pl.BlockSpec((n2, cw), lambda k: (k, 0)),
                  pl.BlockSpec((n2, cw), lambda k: (n1 + k, 0)),
                  pl.BlockSpec((gw, gw), lambda k: (0, 0)),
                  pl.BlockSpec((gw, gw), lambda k: (0, 0))],
        out_specs=pl.BlockSpec((n2, cw), lambda k: (0, k)),
        out_shape=jax.ShapeDtypeStruct((n2, n1 * cw), BF16),
        compiler_params=_params("arbitrary"),
        name="dft_cols",
    )(g, x1.reshape(2 * n1 * n2, cw), x1.reshape(2 * n1 * n2, cw), cc, sc)
    return out.reshape(s_len, cw)


def _dft_small_call(g, uf, cc, sc):
    c, cw = uf.shape
    gw = FOURIER_GROUP_W
    return pl.pallas_call(
        _dft2_kernel,
        grid=(1,),
        in_specs=[pl.BlockSpec((1, 2 * c, c), lambda k: (0, 0, 0)),
                  pl.BlockSpec((c, cw), lambda k: (0, 0)),
                  pl.BlockSpec((gw, gw), lambda k: (0, 0)),
                  pl.BlockSpec((gw, gw), lambda k: (0, 0))],
        out_specs=pl.BlockSpec((c, cw), lambda k: (0, 0)),
        out_shape=jax.ShapeDtypeStruct((c, cw), BF16),
        compiler_params=_params("arbitrary"),
        name="dft_ctx",
    )(g, uf, cc, sc)


def _merge_kernel(a_ref, f_ref, up_ref, upp_ref, upn_ref, ga_ref, gf_ref, gp_ref, x_ref,
                  gate_ref, ps_ref, wo_ref, wf_ref, wp_ref, wout_ref, o_ref, ext_ref, *, s_len):
    i = pl.program_id(0)
    tm = up_ref.shape[0]
    h = POOL_HALO
    ext_ref[0:h] = jnp.where(i > 0, upp_ref[...].astype(F32), 0.0)
    ext_ref[h:h + tm] = up_ref[...].astype(F32)
    ext_ref[h + tm:] = jnp.where(i < pl.num_programs(0) - 1, upn_ref[...].astype(F32), 0.0)

    t = i * tm + lax.broadcasted_iota(jnp.int32, (tm, 1), 0)
    pooled = []
    for gi, w in enumerate(POOL_WINDOWS):
        sl = slice(gi * POOL_GROUP_W, (gi + 1) * POOL_GROUP_W)
        total = ext_ref[h - w // 2:h - w // 2 + tm, sl]
        for off in range(-(w // 2) + 1, w - w // 2):
            total = total + ext_ref[h + off:h + off + tm, sl]
        count = jnp.minimum(t + (w - w // 2), s_len) - jnp.maximum(t - w // 2, 0)
        u = ext_ref[h:h + tm, sl]
        pooled.append((total / count.astype(F32) - u).astype(BF16))
    yp = jnp.concatenate(
        [jnp.dot(p, wp_ref[gi], preferred_element_type=F32) for gi, p in enumerate(pooled)], axis=-1)

    ya = jnp.dot(a_ref[...], wo_ref[...], preferred_element_type=F32)
    yf = jnp.dot(f_ref[...], wf_ref[...], preferred_element_type=F32)
    y = (jax.nn.sigmoid(ga_ref[...].astype(F32)) * ya
         + jax.nn.sigmoid(gf_ref[...].astype(F32)) * yf
         + jax.nn.sigmoid(gp_ref[...].astype(F32)) * (yp * ps_ref[...]))
    mix = jnp.dot(y.astype(BF16), wout_ref[...], preferred_element_type=F32)
    o_ref[...] = x_ref[...] + gate_ref[...] * mix


def _merge_call(a, f, up, g, x, gate, pool_scale, w_attn_o, w_fourier, w_pool, w_out, l, tm):
    s, d = x.shape
    h = POOL_HALO
    n_halo = s // h
    row = pl.BlockSpec((1, d), lambda i: (0, 0))
    once = dict(pipeline_mode=pl.Buffered(1))
    return pl.pallas_call(
        functools.partial(_merge_kernel, s_len=s),
        grid=(s // tm,),
        in_specs=[
            pl.BlockSpec((tm, ATTN_W), lambda i: (i, 0)),
            pl.BlockSpec((tm, FOURIER_W), lambda i: (i, 0)),
            pl.BlockSpec((tm, POOL_W), lambda i: (i, 0)),
            pl.BlockSpec((h, POOL_W), lambda i: (jnp.maximum(i * (tm // h) - 1, 0), 0)),
            pl.BlockSpec((h, POOL_W), lambda i: (jnp.minimum((i + 1) * (tm // h), n_halo - 1), 0)),
            pl.BlockSpec((tm, d), lambda i: (i, 0)),
            pl.BlockSpec((tm, d), lambda i: (i, 1)),
            pl.BlockSpec((tm, d), lambda i: (i, 2)),
            pl.BlockSpec((tm, d), lambda i: (i, 0)),
            row, row,
            pl.BlockSpec((None, ATTN_W, d), lambda i: (l, 0, 0), **once),
            pl.BlockSpec((None, FOURIER_W, d), lambda i: (l, 0, 0), **once),
            pl.BlockSpec((None, len(POOL_WINDOWS), POOL_GROUP_W, POOL_OUT_GROUP_W),
                         lambda i: (l, 0, 0, 0), **once),
            pl.BlockSpec((None, d, d), lambda i: (l, 0, 0), **once),
        ],
        out_specs=pl.BlockSpec((tm, d), lambda i: (i, 0)),
        out_shape=jax.ShapeDtypeStruct((s, d), F32),
        scratch_shapes=[pltpu.VMEM((tm + 2 * h, POOL_W), F32)],
        compiler_params=_params("arbitrary"),
        name="merge",
    )(a, f, up, up, up, g, g, g, x, gate, pool_scale, w_attn_o, w_fourier, w_pool, w_out)


def _rope_tables(s_len):
    rows_n = s_len // GRID_W
    rows = jnp.repeat(jnp.arange(rows_n), GRID_W).astype(F32)
    cols = jnp.tile(jnp.arange(GRID_W), rows_n).astype(F32)
    n_freq = D_HEAD // 4
    freqs = ROPE_BASE ** (-jnp.arange(n_freq, dtype=F32) / n_freq)
    ar, ac = rows[:, None] * freqs, cols[:, None] * freqs
    cos = jnp.concatenate([jnp.cos(ar), jnp.cos(ar), jnp.cos(ac), jnp.cos(ac)], axis=-1)
    sin = jnp.concatenate([-jnp.sin(ar), jnp.sin(ar), -jnp.sin(ac), jnp.sin(ac)], axis=-1)
    return cos, sin


def kernel(x, c, ctx, c_ctx, w_ada, b_ada, norm_w, ffn_w_gate, ffn_w_up, ffn_w_down,
           w_in, q_gain, k_gain, sink, w_attn_o, w_fourier, w_pool, pool_scale, w_out):
    b, s_len, d = x.shape
    c_len = ctx.shape[1]
    assert b == 1 and d == D_MODEL and s_len % (DFT_ROWS * BLOCK) == 0
    x, ctx = x[0], ctx[0]

    cond = jnp.zeros((8, d), F32).at[0].set(c[0]).at[1].set(c_ctx)
    mod = _ada_call(cond, w_ada, b_ada).reshape(DEPTH, 8, N_MOD, 1, d)

    wg, wu, wd = ffn_w_gate.astype(BF16), ffn_w_up.astype(BF16), ffn_w_down.astype(BF16)
    w_in_b, w_ao, w_fo = w_in.astype(BF16), w_attn_o.astype(BF16), w_fourier.astype(BF16)
    w_po, w_ou = w_pool.astype(BF16), w_out.astype(BF16)

    cos_x, sin_x = _rope_tables(s_len)
    cos_c, sin_c = jnp.ones((c_len, D_HEAD), F32), jnp.zeros((c_len, D_HEAD), F32)
    w1, g_x, cc, sc = (jnp.asarray(t).astype(BF16) for t in _dft_tables(s_len))
    _, g_c, cc_c, sc_c = (jnp.asarray(t).astype(BF16) for t in _dft_tables(c_len))

    tm_x, tm_m = 512, 256
    for l in range(DEPTH):
        last = l == DEPTH - 1
        mx, mc = mod[l, 0], mod[l, 1]
        nw = norm_w[l][:, None, :]
        qg, kg = q_gain[l][None, :], k_gain[l][None, :]
        ps = pool_scale[l][None, :]

        x = _ffn_call(x, nw[0], mx[0], mx[1], mx[2], wg, wu, wd, l, 0, tm_x)
        ctx = _ffn_call(ctx, nw[0], mc[0], mc[1], mc[2], wg, wu, wd, l, 0, c_len)

        qx, kvx, ufx, upx, gx = _inproj_call(x, nw[1], mx[3], mx[4], qg, kg, cos_x, sin_x, w_in_b, l, tm_x)
        qc, kvc, ufc, upc, gc = _inproj_call(ctx, nw[1], mc[3], mc[4], qg, kg, cos_c, sin_c, w_in_b, l, c_len)

        ax = _window_attn_call(qx, kvx, kvc, sink[l])
        fx = _dft2_call(g_x, _dft1_call(w1, ufx), cc, sc, s_len)
        x = _merge_call(ax, fx, upx, gx, x, mx[5], ps, w_ao, w_fo, w_po, w_ou, l, tm_m)
        if not last:
            ac = _ctx_attn_call(qc, kvc, sink[l])
            fc = _dft_small_call(g_c, ufc, cc_c, sc_c)
            ctx = _merge_call(ac, fc, upc, gc, ctx, mc[5], ps, w_ao, w_fo, w_po, w_ou, l, c_len)

        x = _ffn_call(x, nw[2], mx[6], mx[7], mx[8], wg, wu, wd, l, 1, tm_x)
        if not last:
            ctx = _ffn_call(ctx, nw[2], mc[6], mc[7], mc[8], wg, wu, wd, l, 1, c_len)
    return x[None]
```

```python
import functools

import numpy as np
import jax
import jax.numpy as jnp
from jax import lax
from jax.experimental import pallas as pl
from jax.experimental.pallas import tpu as pltpu

D_MODEL = 2048
DEPTH = 4
GRID_W = 64
N_HEADS = 8
N_KV_HEADS = 2
GROUP = N_HEADS // N_KV_HEADS
D_HEAD = 128
ATTN_W = N_HEADS * D_HEAD
KV_W = N_KV_HEADS * D_HEAD
WINDOW = 128
BLOCK = 128
FOURIER_W = D_MODEL // 4
N_FOURIER_GROUPS = 4
FOURIER_GROUP_W = FOURIER_W // N_FOURIER_GROUPS
POOL_WINDOWS = (2, 4, 8, 16)
POOL_W = D_MODEL // 4
POOL_GROUP_W = POOL_W // len(POOL_WINDOWS)
POOL_OUT_GROUP_W = D_MODEL // len(POOL_WINDOWS)
N_BRANCHES = 3
GATE_W = N_BRANCHES * D_MODEL
IN_W = ATTN_W + 2 * KV_W + FOURIER_W + POOL_W + GATE_W
D_FF = 5632
N_MOD = 9
ROPE_BASE = 10000.0
EPS = 1e-6
NEG_INF = -1e30

BF16 = jnp.bfloat16
F32 = jnp.float32

VMEM_LIMIT_BYTES = 60 * 1024 * 1024
POOL_HALO = 16
COL_TILE = 512
DFT_ROWS = GRID_W


def _params(*semantics):
    return pltpu.CompilerParams(dimension_semantics=semantics, vmem_limit_bytes=VMEM_LIMIT_BYTES)


def _modulate(x, gain, shift, scale):
    y = x * lax.rsqrt(jnp.mean(x * x, axis=-1, keepdims=True) + EPS)
    return (y * gain) * (1 + scale) + shift


def _ada_kernel(cond_ref, w_ref, b_ref, o_ref):
    h = jax.nn.silu(cond_ref[...]).astype(BF16)
    o_ref[0] = jnp.dot(h, w_ref[0].astype(BF16), preferred_element_type=F32) + b_ref[0]


def _ada_call(cond, w_ada, b_ada):
    depth, d, n = w_ada.shape
    tn = 1024
    return pl.pallas_call(
        _ada_kernel,
        grid=(depth, n // tn),
        in_specs=[
            pl.BlockSpec((8, d), lambda l, j: (0, 0)),
            pl.BlockSpec((1, d, tn), lambda l, j: (l, 0, j)),
            pl.BlockSpec((1, 1, tn), lambda l, j: (l, 0, j)),
        ],
        out_specs=pl.BlockSpec((1, 8, tn), lambda l, j: (l, 0, j)),
        out_shape=jax.ShapeDtypeStruct((depth, 8, n), F32),
        compiler_params=_params("arbitrary", "arbitrary"),
        name="adaln",
    )(cond, w_ada, b_ada.reshape(depth, 1, n))


def _ffn_kernel(x_ref, nw_ref, sh_ref, sc_ref, gt_ref, wg_ref, wu_ref, wd_ref, o_ref, h_ref):
    j = pl.program_id(1)

    @pl.when(j == 0)
    def _():
        h_ref[...] = _modulate(x_ref[...], nw_ref[...], sh_ref[...], sc_ref[...]).astype(BF16)
        o_ref[...] = jnp.zeros_like(o_ref)

    h = h_ref[...]
    g = jnp.dot(h, wg_ref[...], preferred_element_type=F32)
    u = jnp.dot(h, wu_ref[...], preferred_element_type=F32)
    a = (jax.nn.silu(g) * u).astype(BF16)
    o_ref[...] += jnp.dot(a, wd_ref[...], preferred_element_type=F32)

    @pl.when(j == pl.num_programs(1) - 1)
    def _():
        o_ref[...] = x_ref[...] + (0.5 * gt_ref[...]) * o_ref[...]


def _ffn_call(x, nw, shift, scale, gate, wg, wu, wd, l, which, tm):
    s, d = x.shape
    tf = 512
    row = pl.BlockSpec((1, d), lambda i, j: (0, 0))
    return pl.pallas_call(
        _ffn_kernel,
        grid=(s // tm, D_FF // tf),
        in_specs=[
            pl.BlockSpec((tm, d), lambda i, j: (i, 0)),
            row, row, row, row,
            pl.BlockSpec((None, None, d, tf), lambda i, j: (l, which, 0, j)),
            pl.BlockSpec((None, None, d, tf), lambda i, j: (l, which, 0, j)),
            pl.BlockSpec((None, None, tf, d), lambda i, j: (l, which, j, 0)),
        ],
        out_specs=pl.BlockSpec((tm, d), lambda i, j: (i, 0)),
        out_shape=jax.ShapeDtypeStruct((s, d), F32),
        scratch_shapes=[pltpu.VMEM((tm, d), BF16)],
        compiler_params=_params("arbitrary", "arbitrary"),
        name="ffn",
    )(x, nw, shift, scale, gate, wg, wu, wd)


Q_TILES = ATTN_W // COL_TILE
KV_TILE = Q_TILES
UF_TILE = KV_TILE + 1
UP_TILE = UF_TILE + 1
G_TILE0 = UP_TILE + 1
N_COL_TILES = IN_W // COL_TILE


def _norm_rope(z, gain, cos, sin_signed):
    y = z * lax.rsqrt(jnp.mean(z * z, axis=-1, keepdims=True) + EPS) * gain
    lane = lax.broadcasted_iota(jnp.int32, y.shape, 1)
    first_half = (lane & (D_HEAD // 4)) == 0
    partner = jnp.where(first_half, pltpu.roll(y, D_HEAD - D_HEAD // 4, 1), pltpu.roll(y, D_HEAD // 4, 1))
    return y * cos + partner * sin_signed


def _inproj_kernel(x_ref, nw_ref, sh_ref, sc_ref, qg_ref, kg_ref, cos_ref, sin_ref, w_ref,
                   q_ref, kv_ref, uf_ref, up_ref, g_ref, h_ref):
    j = pl.program_id(1)

    @pl.when(j == 0)
    def _():
        h_ref[...] = _modulate(x_ref[...], nw_ref[...], sh_ref[...], sc_ref[...]).astype(BF16)

    z = jnp.dot(h_ref[...], w_ref[...].astype(BF16), preferred_element_type=F32)

    @pl.when(j < Q_TILES)
    def _():
        cos, sin = cos_ref[...], sin_ref[...]
        for hh in range(COL_TILE // D_HEAD):
            sl = slice(hh * D_HEAD, (hh + 1) * D_HEAD)
            q = _norm_rope(z[:, sl], qg_ref[...], cos, sin) * (D_HEAD ** -0.5)
            q_ref[:, sl] = q.astype(BF16)

    @pl.when(j == KV_TILE)
    def _():
        cos, sin = cos_ref[...], sin_ref[...]
        for hh in range(N_KV_HEADS):
            sl = slice(hh * D_HEAD, (hh + 1) * D_HEAD)
            kv_ref[:, sl] = _norm_rope(z[:, sl], kg_ref[...], cos, sin).astype(BF16)
        kv_ref[:, KV_W:] = z[:, KV_W:].astype(BF16)

    @pl.when(j == UF_TILE)
    def _():
        uf_ref[...] = z.astype(BF16)

    @pl.when(j == UP_TILE)
    def _():
        up_ref[...] = z.astype(BF16)

    @pl.when(j >= G_TILE0)
    def _():
        g_ref[...] = z.astype(BF16)


def _inproj_call(x, nw, shift, scale, qg, kg, cos, sin, w_in, l, tm):
    s, d = x.shape
    ct = COL_TILE
    row = pl.BlockSpec((1, d), lambda i, j: (0, 0))
    hrow = pl.BlockSpec((1, D_HEAD), lambda i, j: (0, 0))
    tab = pl.BlockSpec((tm, D_HEAD), lambda i, j: (i, 0))
    n_g = GATE_W // ct
    return pl.pallas_call(
        _inproj_kernel,
        grid=(s // tm, N_COL_TILES),
        in_specs=[
            pl.BlockSpec((tm, d), lambda i, j: (i, 0)),
            row, row, row, hrow, hrow, tab, tab,
            pl.BlockSpec((None, d, ct), lambda i, j: (l, 0, j)),
        ],
        out_specs=[
            pl.BlockSpec((tm, ct), lambda i, j: (i, jnp.minimum(j, Q_TILES - 1))),
            pl.BlockSpec((tm, ct), lambda i, j: (i, 0)),
            pl.BlockSpec((tm, ct), lambda i, j: (i, 0)),
            pl.BlockSpec((tm, ct), lambda i, j: (i, 0)),
            pl.BlockSpec((tm, ct), lambda i, j: (i, jnp.clip(j - G_TILE0, 0, n_g - 1))),
        ],
        out_shape=[
            jax.ShapeDtypeStruct((s, ATTN_W), BF16),
            jax.ShapeDtypeStruct((s, 2 * KV_W), BF16),
            jax.ShapeDtypeStruct((s, FOURIER_W), BF16),
            jax.ShapeDtypeStruct((s, POOL_W), BF16),
            jax.ShapeDtypeStruct((s, GATE_W), BF16),
        ],
        scratch_shapes=[pltpu.VMEM((tm, d), BF16)],
        compiler_params=_params("arbitrary", "arbitrary"),
        name="inproj",
    )(x, nw, shift, scale, qg, kg, cos, sin, w_in)


def _qk(q, k):
    return lax.dot_general(q, k, (((1,), (1,)), ((), ())), preferred_element_type=F32)


def _k_cols(kvh):
    return slice(kvh * D_HEAD, (kvh + 1) * D_HEAD)


def _v_cols(kvh):
    return slice(KV_W + kvh * D_HEAD, KV_W + (kvh + 1) * D_HEAD)


def _sink_column(sink_ref, kvh, rows):
    head = lax.broadcasted_iota(jnp.int32, (GROUP * rows, 1), 0) // rows
    col = jnp.full((GROUP * rows, 1), sink_ref[kvh * GROUP], F32)
    for hh in range(1, GROUP):
        col = jnp.where(head == hh, sink_ref[kvh * GROUP + hh], col)
    return col


def _stack_heads(q_ref, kvh):
    heads = range(kvh * GROUP, (kvh + 1) * GROUP)
    return jnp.concatenate([q_ref[:, hh * D_HEAD:(hh + 1) * D_HEAD] for hh in heads], axis=0)


def _softmax_pv(scores, values, sink_col):
    def lane_tiles(a):
        return [a[:, t:t + D_HEAD] for t in range(0, a.shape[1], D_HEAD)]

    m_tile = functools.reduce(jnp.maximum, [t for s in scores for t in lane_tiles(s)])
    m = jnp.maximum(sink_col, jnp.max(m_tile, axis=-1, keepdims=True))
    e_tile = None
    acc = None
    for s, v in zip(scores, values):
        e = jnp.exp(s - m)
        for t in lane_tiles(e):
            e_tile = t if e_tile is None else e_tile + t
        pv = jnp.dot(e.astype(BF16), v, preferred_element_type=F32)
        acc = pv if acc is None else acc + pv
    denom = jnp.exp(sink_col - m) + jnp.sum(e_tile, axis=-1, keepdims=True)
    return acc / denom


def _unstack_heads(out, o_ref, kvh, rows):
    for g in range(GROUP):
        hh = kvh * GROUP + g
        o_ref[:, hh * D_HEAD:(hh + 1) * D_HEAD] = out[g * rows:(g + 1) * rows].astype(o_ref.dtype)


def _window_attn_kernel(sink_ref, q_ref, kvp_ref, kvc_ref, kvn_ref, kvx_ref, o_ref):
    i = pl.program_id(0)
    nb = pl.num_programs(0)
    shape = (GROUP * BLOCK, BLOCK)
    qi = lax.broadcasted_iota(jnp.int32, shape, 0) % BLOCK
    kj = lax.broadcasted_iota(jnp.int32, shape, 1)
    keep_prev = (kj >= qi) & (i > 0)
    keep_next = (kj <= qi) & (i < nb - 1)
    for kvh in range(N_KV_HEADS):
        q = _stack_heads(q_ref, kvh)
        ks, vs = _k_cols(kvh), _v_cols(kvh)
        s_prev = jnp.where(keep_prev, _qk(q, kvp_ref[:, ks]), NEG_INF)
        s_cur = _qk(q, kvc_ref[:, ks])
        s_next = jnp.where(keep_next, _qk(q, kvn_ref[:, ks]), NEG_INF)
        s_ctx = _qk(q, kvx_ref[:, ks])
        out = _softmax_pv([s_prev, s_cur, s_next, s_ctx],
                          [kvp_ref[:, vs], kvc_ref[:, vs], kvn_ref[:, vs], kvx_ref[:, vs]],
                          _sink_column(sink_ref, kvh, BLOCK))
        _unstack_heads(out, o_ref, kvh, BLOCK)


def _window_attn_call(q, kv, kv_ctx, sink):
    s = q.shape[0]
    c = kv_ctx.shape[0]
    nb = s // BLOCK
    kblk = lambda off: pl.BlockSpec((BLOCK, 2 * KV_W), lambda i: (jnp.clip(i + off, 0, nb - 1), 0))
    return pl.pallas_call(
        _window_attn_kernel,
        grid=(nb,),
        in_specs=[
            pl.BlockSpec(memory_space=pltpu.SMEM),
            pl.BlockSpec((BLOCK, ATTN_W), lambda i: (i, 0)),
            kblk(-1), kblk(0), kblk(1),
            pl.BlockSpec((c, 2 * KV_W), lambda i: (0, 0)),
        ],
        out_specs=pl.BlockSpec((BLOCK, ATTN_W), lambda i: (i, 0)),
        out_shape=jax.ShapeDtypeStruct((s, ATTN_W), BF16),
        compiler_params=_params("arbitrary"),
        name="window_attn",
    )(sink, q, kv, kv, kv, kv_ctx)


def _ctx_attn_kernel(sink_ref, q_ref, kv_ref, o_ref):
    rows = q_ref.shape[0]
    for kvh in range(N_KV_HEADS):
        q = _stack_heads(q_ref, kvh)
        out = _softmax_pv([_qk(q, kv_ref[:, _k_cols(kvh)])], [kv_ref[:, _v_cols(kvh)]],
                          _sink_column(sink_ref, kvh, rows))
        _unstack_heads(out, o_ref, kvh, rows)


def _ctx_attn_call(q, kv, sink):
    c = q.shape[0]
    return pl.pallas_call(
        _ctx_attn_kernel,
        grid=(1,),
        in_specs=[
            pl.BlockSpec(memory_space=pltpu.SMEM),
            pl.BlockSpec((c, ATTN_W), lambda i: (0, 0)),
            pl.BlockSpec((c, 2 * KV_W), lambda i: (0, 0)),
        ],
        out_specs=pl.BlockSpec((c, ATTN_W), lambda i: (0, 0)),
        out_shape=jax.ShapeDtypeStruct((c, ATTN_W), BF16),
        compiler_params=_params("arbitrary"),
        name="ctx_attn",
    )(sink, q, kv)


def _dft_tables(s_len):
    n1 = DFT_ROWS if s_len > 1024 else 1
    n2 = s_len // n1
    k1 = np.arange(n1)
    ang1 = 2.0 * np.pi * ((k1[:, None] * k1[None, :]) % n1) / n1
    w1 = np.concatenate([np.cos(ang1), -np.sin(ang1)], axis=0)
    k2 = np.arange(n2)
    phase = (k2[None, None, :] * (k1[:, None, None] + n1 * k2[None, :, None])) % s_len
    ang2 = 2.0 * np.pi * phase / s_len
    gr, gi = np.cos(ang2), -np.sin(ang2)
    g = np.concatenate([np.concatenate([gr, -gi], axis=2),
                        np.concatenate([gi, gr], axis=2)], axis=1)
    if n1 == 1:
        g = g[:, :, :n2]
    c = np.arange(FOURIER_GROUP_W)
    angc = 2.0 * np.pi * ((c[:, None] * c[None, :]) % FOURIER_GROUP_W) / FOURIER_GROUP_W
    norm = 1.0 / np.sqrt(float(s_len) * FOURIER_GROUP_W)
    return (w1.astype(np.float32), g.astype(np.float32),
            (np.cos(angc) * norm).astype(np.float32), (np.sin(angc) * norm).astype(np.float32))


def _dft1_kernel(w_ref, u_ref, o_ref):
    o_ref[...] = jnp.dot(w_ref[...], u_ref[...], preferred_element_type=F32).astype(o_ref.dtype)


def _dft1_call(w1, uf):
    s, cw = uf.shape
    n1 = DFT_ROWS
    cols = (s // n1) * cw
    tn = 4096
    return pl.pallas_call(
        _dft1_kernel,
        grid=(cols // tn,),
        in_specs=[pl.BlockSpec((2 * n1, n1), lambda j: (0, 0)),
                  pl.BlockSpec((n1, tn), lambda j: (0, j))],
        out_specs=pl.BlockSpec((2 * n1, tn), lambda j: (0, j)),
        out_shape=jax.ShapeDtypeStruct((2 * n1, cols), BF16),
        compiler_params=_params("arbitrary"),
        name="dft_rows",
    )(w1, uf.reshape(n1, cols))


def _dft2_kernel(g_ref, *refs):
    x_refs, (cc_ref, sc_ref, o_ref) = refs[:-3], refs[-3:]
    kb = g_ref.shape[0]
    n_in = x_refs[0].shape[0] // kb
    n = g_ref.shape[1] // 2
    cw = FOURIER_W
    for b in range(kb):
        rows = slice(b * n_in, (b + 1) * n_in)
        x = (x_refs[0][rows] if len(x_refs) == 1
             else jnp.concatenate([r[rows] for r in x_refs], axis=0))
        y = jnp.dot(g_ref[b], x, preferred_element_type=F32)
        yr, yi = y[:n].astype(BF16), y[n:].astype(BF16)
        for gi in range(N_FOURIER_GROUPS):
            sl = slice(gi * FOURIER_GROUP_W, (gi + 1) * FOURIER_GROUP_W)
            o = (jnp.dot(yr[:, sl], cc_ref[...], preferred_element_type=F32)
                 + jnp.dot(yi[:, sl], sc_ref[...], preferred_element_type=F32))
            o_ref[:, b * cw + gi * FOURIER_GROUP_W:b * cw + (gi + 1) * FOURIER_GROUP_W] = o.astype(o_ref.dtype)


def _dft2_call(g, x1, cc, sc, s_len):
    n1 = DFT_ROWS
    n2 = s_len // n1
    cw = FOURIER_W
    gw = FOURIER_GROUP_W
    kb = 4
    out = pl.pallas_call(
        _dft2_kernel,
        grid=(n1 // kb,),
        in_specs=[pl.BlockSpec((kb, 2 * n2, 2 * n2), lambda k: (k, 0, 0)),
                  pl.BlockSpec((kb * n2, cw), lambda k: (k, 0)),
                  pl.BlockSpec((kb * n2, cw), lambda k: (n1 // kb + k, 0)),
                  pl.BlockSpec((gw, gw), lambda k: (0, 0)),
                  pl.BlockSpec((gw, gw), lambda k: (0, 0))],
        out_specs=pl.BlockSpec((n2, kb * cw), lambda k: (0, k)),
        out_shape=jax.ShapeDtypeStruct((n2, n1 * cw), BF16),
        compiler_params=_params("arbitrary"),
        name="dft_cols",
    )(g, x1.reshape(2 * n1 * n2, cw), x1.reshape(2 * n1 * n2, cw), cc, sc)
    return out.reshape(s_len, cw)


def _dft_small_call(g, uf, cc, sc):
    c, cw = uf.shape
    gw = FOURIER_GROUP_W
    return pl.pallas_call(
        _dft2_kernel,
        grid=(1,),
        in_specs=[pl.BlockSpec((1, 2 * c, c), lambda k: (0, 0, 0)),
                  pl.BlockSpec((c, cw), lambda k: (0, 0)),
                  pl.BlockSpec((gw, gw), lambda k: (0, 0)),
                  pl.BlockSpec((gw, gw), lambda k: (0, 0))],
        out_specs=pl.BlockSpec((c, cw), lambda k: (0, 0)),
        out_shape=jax.ShapeDtypeStruct((c, cw), BF16),
        compiler_params=_params("arbitrary"),
        name="dft_ctx",
    )(g, uf, cc, sc)


def _merge_kernel(a_ref, f_ref, up_ref, upp_ref, upn_ref, ga_ref, gf_ref, gp_ref, x_ref,
                  gate_ref, ps_ref, wo_ref, wf_ref, wp_ref, wout_ref, o_ref, ext_ref, *, s_len):
    i = pl.program_id(0)
    tm = up_ref.shape[0]
    h = POOL_HALO
    ext_ref[0:h] = jnp.where(i > 0, upp_ref[...].astype(F32), 0.0)
    ext_ref[h:h + tm] = up_ref[...].astype(F32)
    ext_ref[h + tm:] = jnp.where(i < pl.num_programs(0) - 1, upn_ref[...].astype(F32), 0.0)

    t = i * tm + lax.broadcasted_iota(jnp.int32, (tm, 1), 0)
    pooled = []
    for gi, w in enumerate(POOL_WINDOWS):
        sl = slice(gi * POOL_GROUP_W, (gi + 1) * POOL_GROUP_W)
        total = ext_ref[h - w // 2:h - w // 2 + tm, sl]
        for off in range(-(w // 2) + 1, w - w // 2):
            total = total + ext_ref[h + off:h + off + tm, sl]
        count = jnp.minimum(t + (w - w // 2), s_len) - jnp.maximum(t - w // 2, 0)
        u = ext_ref[h:h + tm, sl]
        pooled.append((total / count.astype(F32) - u).astype(BF16))
    yp = jnp.concatenate(
        [jnp.dot(p, wp_ref[gi], preferred_element_type=F32) for gi, p in enumerate(pooled)], axis=-1)

    ya = jnp.dot(a_ref[...], wo_ref[...], preferred_element_type=F32)
    yf = jnp.dot(f_ref[...], wf_ref[...], preferred_element_type=F32)
    y = (jax.nn.sigmoid(ga_ref[...].astype(F32)) * ya
         + jax.nn.sigmoid(gf_ref[...].astype(F32)) * yf
         + jax.nn.sigmoid(gp_ref[...].astype(F32)) * (yp * ps_ref[...]))
    mix = jnp.dot(y.astype(BF16), wout_ref[...], preferred_element_type=F32)
    o_ref[...] = x_ref[...] + gate_ref[...] * mix


def _merge_call(a, f, up, g, x, gate, pool_scale, w_attn_o, w_fourier, w_pool, w_out, l, tm):
    s, d = x.shape
    h = POOL_HALO
    n_halo = s // h
    row = pl.BlockSpec((1, d), lambda i: (0, 0))
    once = dict(pipeline_mode=pl.Buffered(1))
    return pl.pallas_call(
        functools.partial(_merge_kernel, s_len=s),
        grid=(s // tm,),
        in_specs=[
            pl.BlockSpec((tm, ATTN_W), lambda i: (i, 0)),
            pl.BlockSpec((tm, FOURIER_W), lambda i: (i, 0)),
            pl.BlockSpec((tm, POOL_W), lambda i: (i, 0)),
            pl.BlockSpec((h, POOL_W), lambda i: (jnp.maximum(i * (tm // h) - 1, 0), 0)),
            pl.BlockSpec((h, POOL_W), lambda i: (jnp.minimum((i + 1) * (tm // h), n_halo - 1), 0)),
            pl.BlockSpec((tm, d), lambda i: (i, 0)),
            pl.BlockSpec((tm, d), lambda i: (i, 1)),
            pl.BlockSpec((tm, d), lambda i: (i, 2)),
            pl.BlockSpec((tm, d), lambda i: (i, 0)),
            row, row,
            pl.BlockSpec((None, ATTN_W, d), lambda i: (l, 0, 0), **once),
            pl.BlockSpec((None, FOURIER_W, d), lambda i: (l, 0, 0), **once),
            pl.BlockSpec((None, len(POOL_WINDOWS), POOL_GROUP_W, POOL_OUT_GROUP_W),
                         lambda i: (l, 0, 0, 0), **once),
            pl.BlockSpec((None, d, d), lambda i: (l, 0, 0), **once),
        ],
        out_specs=pl.BlockSpec((tm, d), lambda i: (i, 0)),
        out_shape=jax.ShapeDtypeStruct((s, d), F32),
        scratch_shapes=[pltpu.VMEM((tm + 2 * h, POOL_W), F32)],
        compiler_params=_params("arbitrary"),
        name="merge",
    )(a, f, up, up, up, g, g, g, x, gate, pool_scale, w_attn_o, w_fourier, w_pool, w_out)


def _rope_tables(s_len):
    rows_n = s_len // GRID_W
    rows = jnp.repeat(jnp.arange(rows_n), GRID_W).astype(F32)
    cols = jnp.tile(jnp.arange(GRID_W), rows_n).astype(F32)
    n_freq = D_HEAD // 4
    freqs = ROPE_BASE ** (-jnp.arange(n_freq, dtype=F32) / n_freq)
    ar, ac = rows[:, None] * freqs, cols[:, None] * freqs
    cos = jnp.concatenate([jnp.cos(ar), jnp.cos(ar), jnp.cos(ac), jnp.cos(ac)], axis=-1)
    sin = jnp.concatenate([-jnp.sin(ar), jnp.sin(ar), -jnp.sin(ac), jnp.sin(ac)], axis=-1)
    return cos, sin


def kernel(x, c, ctx, c_ctx, w_ada, b_ada, norm_w, ffn_w_gate, ffn_w_up, ffn_w_down,
           w_in, q_gain, k_gain, sink, w_attn_o, w_fourier, w_pool, pool_scale, w_out):
    b, s_len, d = x.shape
    c_len = ctx.shape[1]
    assert b == 1 and d == D_MODEL and s_len % (DFT_ROWS * BLOCK) == 0
    x, ctx = x[0], ctx[0]

    cond = jnp.zeros((8, d), F32).at[0].set(c[0]).at[1].set(c_ctx)
    mod = _ada_call(cond, w_ada, b_ada).reshape(DEPTH, 8, N_MOD, 1, d)

    wg, wu, wd = ffn_w_gate.astype(BF16), ffn_w_up.astype(BF16), ffn_w_down.astype(BF16)
    w_in_b, w_ao, w_fo = w_in, w_attn_o.astype(BF16), w_fourier.astype(BF16)
    w_po, w_ou = w_pool.astype(BF16), w_out.astype(BF16)

    cos_x, sin_x = _rope_tables(s_len)
    cos_c, sin_c = jnp.ones((c_len, D_HEAD), F32), jnp.zeros((c_len, D_HEAD), F32)
    w1, g_x, cc, sc = (jnp.asarray(t).astype(BF16) for t in _dft_tables(s_len))
    _, g_c, cc_c, sc_c = (jnp.asarray(t).astype(BF16) for t in _dft_tables(c_len))

    tm_x, tm_f, tm_m = 1024, 512, 256
    for l in range(DEPTH):
        last = l == DEPTH - 1
        mx, mc = mod[l, 0], mod[l, 1]
        nw = norm_w[l][:, None, :]
        qg, kg = q_gain[l][None, :], k_gain[l][None, :]
        ps = pool_scale[l][None, :]

        x = _ffn_call(x, nw[0], mx[0], mx[1], mx[2], wg, wu, wd, l, 0, tm_f)
        ctx = _ffn_call(ctx, nw[0], mc[0], mc[1], mc[2], wg, wu, wd, l, 0, c_len)

        qx, kvx, ufx, upx, gx = _inproj_call(x, nw[1], mx[3], mx[4], qg, kg, cos_x, sin_x, w_in_b, l, tm_x)
        qc, kvc, ufc, upc, gc = _inproj_call(ctx, nw[1], mc[3], mc[4], qg, kg, cos_c, sin_c, w_in_b, l, c_len)

        ax = _window_attn_call(qx, kvx, kvc, sink[l])
        fx = _dft2_call(g_x, _dft1_call(w1, ufx), cc, sc, s_len)
        x = _merge_call(ax, fx, upx, gx, x, mx[5], ps, w_ao, w_fo, w_po, w_ou, l, tm_m)
        if not last:
            ac = _ctx_attn_call(qc, kvc, sink[l])
            fc = _dft_small_call(g_c, ufc, cc_c, sc_c)
            ctx = _merge_call(ac, fc, upc, gc, ctx, mc[5], ps, w_ao, w_fo, w_po, w_ou, l, c_len)

        x = _ffn_call(x, nw[2], mx[6], mx[7], mx[8], wg, wu, wd, l, 1, tm_f)
        if not last:
            ctx = _ffn_call(ctx, nw[2], mc[6], mc[7], mc[8], wg, wu, wd, l, 1, c_len)
    return x[None]
```

```python
import functools

import numpy as np
import jax
import jax.numpy as jnp
from jax import lax
from jax.experimental import pallas as pl
from jax.experimental.pallas import tpu as pltpu

D_MODEL = 2048
DEPTH = 4
GRID_W = 64
N_HEADS = 8
N_KV_HEADS = 2
GROUP = N_HEADS // N_KV_HEADS
D_HEAD = 128
ATTN_W = N_HEADS * D_HEAD
KV_W = N_KV_HEADS * D_HEAD
WINDOW = 128
BLOCK = 128
FOURIER_W = D_MODEL // 4
N_FOURIER_GROUPS = 4
FOURIER_GROUP_W = FOURIER_W // N_FOURIER_GROUPS
POOL_WINDOWS = (2, 4, 8, 16)
POOL_W = D_MODEL // 4
POOL_GROUP_W = POOL_W // len(POOL_WINDOWS)
POOL_OUT_GROUP_W = D_MODEL // len(POOL_WINDOWS)
N_BRANCHES = 3
GATE_W = N_BRANCHES * D_MODEL
IN_W = ATTN_W + 2 * KV_W + FOURIER_W + POOL_W + GATE_W
D_FF = 5632
N_MOD = 9
ROPE_BASE = 10000.0
EPS = 1e-6
NEG_INF = -1e30

BF16 = jnp.bfloat16
F32 = jnp.float32

VMEM_LIMIT_BYTES = 60 * 1024 * 1024
POOL_HALO = 16
COL_TILE = 512
FFN_TILE_F = 512
DFT_ROWS = GRID_W


def _params(*semantics):
    return pltpu.CompilerParams(dimension_semantics=semantics, vmem_limit_bytes=VMEM_LIMIT_BYTES)


def _modulate(x, gain, shift, scale):
    y = x * lax.rsqrt(jnp.mean(x * x, axis=-1, keepdims=True) + EPS)
    return (y * gain) * (1 + scale) + shift


def _ada_kernel(cond_ref, w_ref, b_ref, o_ref):
    h = jax.nn.silu(cond_ref[...]).astype(BF16)
    o_ref[0] = jnp.dot(h, w_ref[0].astype(BF16), preferred_element_type=F32) + b_ref[0]


def _ada_call(cond, w_ada, b_ada):
    depth, d, n = w_ada.shape
    tn = 1024
    return pl.pallas_call(
        _ada_kernel,
        grid=(depth, n // tn),
        in_specs=[
            pl.BlockSpec((8, d), lambda l, j: (0, 0)),
            pl.BlockSpec((1, d, tn), lambda l, j: (l, 0, j)),
            pl.BlockSpec((1, 1, tn), lambda l, j: (l, 0, j)),
        ],
        out_specs=pl.BlockSpec((1, 8, tn), lambda l, j: (l, 0, j)),
        out_shape=jax.ShapeDtypeStruct((depth, 8, n), F32),
        compiler_params=_params("arbitrary", "arbitrary"),
        name="adaln",
    )(cond, w_ada, b_ada.reshape(depth, 1, n))


def _ffn_kernel(*refs, convert_next):
    if convert_next:
        (x_hbm, nw_ref, sh_ref, sc_ref, gt_ref, wg_ref, wu_ref, wd_ref, ng_ref, nu_ref, nd_ref,
         o_hbm, cg_ref, cu_ref, cd_ref, xs_ref, acc_ref, h_ref, sem) = refs
    else:
        (x_hbm, nw_ref, sh_ref, sc_ref, gt_ref, wg_ref, wu_ref, wd_ref,
         o_hbm, xs_ref, acc_ref, h_ref, sem) = refs
    i, j = pl.program_id(0), pl.program_id(1)
    n_tiles, n_ff = pl.num_programs(0), pl.num_programs(1)
    tm = xs_ref.shape[0]

    def x_copy(tile):
        return pltpu.make_async_copy(x_hbm.at[pl.ds(tile * tm, tm)], xs_ref, sem.at[0])

    def o_copy(tile):
        return pltpu.make_async_copy(acc_ref, o_hbm.at[pl.ds(tile * tm, tm)], sem.at[1])

    @pl.when(j == 0)
    def _():
        @pl.when(i == 0)
        def _():
            x_copy(0).start()

        x_copy(i).wait()
        h_ref[...] = _modulate(xs_ref[...], nw_ref[...], sh_ref[...], sc_ref[...]).astype(BF16)

        @pl.when(i > 0)
        def _():
            o_copy(i - 1).wait()

        acc_ref[...] = xs_ref[...]

        @pl.when(i + 1 < n_tiles)
        def _():
            x_copy(i + 1).start()

    h = h_ref[...]
    g = jnp.dot(h, wg_ref[...], preferred_element_type=F32)
    u = jnp.dot(h, wu_ref[...], preferred_element_type=F32)
    a = (jax.nn.silu(g) * u).astype(BF16)
    acc_ref[...] += (0.5 * gt_ref[...]) * jnp.dot(a, wd_ref[...], preferred_element_type=F32)

    if convert_next:
        cg_ref[...] = ng_ref[...].astype(BF16)
        cu_ref[...] = nu_ref[...].astype(BF16)
        cd_ref[...] = nd_ref[...].astype(BF16)

    @pl.when(j == n_ff - 1)
    def _():
        o_copy(i).start()

        @pl.when(i == n_tiles - 1)
        def _():
            o_copy(i).wait()


def _ffn_call(x, nw, shift, scale, gate, w_bf16, tm, next_f32=None):
    s, d = x.shape
    tf = FFN_TILE_F
    n_tiles = s // tm
    row = pl.BlockSpec((1, d), lambda i, j: (0, 0))
    in_specs = [
        pl.BlockSpec(memory_space=pl.ANY),
        row, row, row, row,
        pl.BlockSpec((d, tf), lambda i, j: (0, j)),
        pl.BlockSpec((d, tf), lambda i, j: (0, j)),
        pl.BlockSpec((tf, d), lambda i, j: (j, 0)),
    ]
    out_specs = [pl.BlockSpec(memory_space=pl.ANY)]
    out_shape = [jax.ShapeDtypeStruct((s, d), F32)]
    args = [x, nw, shift, scale, gate, *w_bf16]
    if next_f32 is not None:
        (ng, nu, nd), l2, w2 = next_f32
        rb = d // n_tiles
        in_specs += [
            pl.BlockSpec((None, None, rb, tf), lambda i, j: (l2, w2, i, j)),
            pl.BlockSpec((None, None, rb, tf), lambda i, j: (l2, w2, i, j)),
            pl.BlockSpec((None, None, tf, rb), lambda i, j: (l2, w2, j, i)),
        ]
        out_specs += [
            pl.BlockSpec((rb, tf), lambda i, j: (i, j)),
            pl.BlockSpec((rb, tf), lambda i, j: (i, j)),
            pl.BlockSpec((tf, rb), lambda i, j: (j, i)),
        ]
        out_shape += [jax.ShapeDtypeStruct((d, D_FF), BF16), jax.ShapeDtypeStruct((d, D_FF), BF16),
                      jax.ShapeDtypeStruct((D_FF, d), BF16)]
        args += [ng, nu, nd]
    outs = pl.pallas_call(
        functools.partial(_ffn_kernel, convert_next=next_f32 is not None),
        grid=(n_tiles, D_FF // tf),
        in_specs=in_specs,
        out_specs=out_specs,
        out_shape=out_shape,
        scratch_shapes=[pltpu.VMEM((tm, d), F32), pltpu.VMEM((tm, d), F32), pltpu.VMEM((tm, d), BF16),
                        pltpu.SemaphoreType.DMA((2,))],
        compiler_params=_params("arbitrary", "arbitrary"),
        name="ffn",
    )(*args)
    return outs[0], (tuple(outs[1:]) if next_f32 is not None else None)


Q_TILES = ATTN_W // COL_TILE
KV_TILE = Q_TILES
UF_TILE = KV_TILE + 1
UP_TILE = UF_TILE + 1
G_TILE0 = UP_TILE + 1
N_COL_TILES = IN_W // COL_TILE


def _norm_rope(z, gain, cos, sin_signed):
    y = z * lax.rsqrt(jnp.mean(z * z, axis=-1, keepdims=True) + EPS) * gain
    lane = lax.broadcasted_iota(jnp.int32, y.shape, 1)
    first_half = (lane & (D_HEAD // 4)) == 0
    partner = jnp.where(first_half, pltpu.roll(y, D_HEAD - D_HEAD // 4, 1), pltpu.roll(y, D_HEAD // 4, 1))
    return y * cos + partner * sin_signed


def _inproj_kernel(x_ref, nw_ref, sh_ref, sc_ref, qg_ref, kg_ref, cos_ref, sin_ref, w_ref,
                   q_ref, kv_ref, uf_ref, up_ref, g_ref, h_ref):
    j = pl.program_id(1)

    @pl.when(j == 0)
    def _():
        h_ref[...] = _modulate(x_ref[...], nw_ref[...], sh_ref[...], sc_ref[...]).astype(BF16)

    z = jnp.dot(h_ref[...], w_ref[...].astype(BF16), preferred_element_type=F32)

    @pl.when(j < Q_TILES)
    def _():
        cos, sin = cos_ref[...], sin_ref[...]
        for hh in range(COL_TILE // D_HEAD):
            sl = slice(hh * D_HEAD, (hh + 1) * D_HEAD)
            q = _norm_rope(z[:, sl], qg_ref[...], cos, sin) * (D_HEAD ** -0.5)
            q_ref[:, sl] = q.astype(BF16)

    @pl.when(j == KV_TILE)
    def _():
        cos, sin = cos_ref[...], sin_ref[...]
        for hh in range(N_KV_HEADS):
            sl = slice(hh * D_HEAD, (hh + 1) * D_HEAD)
            kv_ref[:, sl] = _norm_rope(z[:, sl], kg_ref[...], cos, sin).astype(BF16)
        kv_ref[:, KV_W:] = z[:, KV_W:].astype(BF16)

    @pl.when(j == UF_TILE)
    def _():
        uf_ref[...] = z.astype(BF16)

    @pl.when(j == UP_TILE)
    def _():
        up_ref[...] = z.astype(BF16)

    @pl.when(j >= G_TILE0)
    def _():
        g_ref[...] = z.astype(BF16)


def _inproj_call(x, nw, shift, scale, qg, kg, cos, sin, w_in, l, tm):
    s, d = x.shape
    ct = COL_TILE
    row = pl.BlockSpec((1, d), lambda i, j: (0, 0))
    hrow = pl.BlockSpec((1, D_HEAD), lambda i, j: (0, 0))
    tab = pl.BlockSpec((tm, D_HEAD), lambda i, j: (i, 0))
    n_g = GATE_W // ct
    return pl.pallas_call(
        _inproj_kernel,
        grid=(s // tm, N_COL_TILES),
        in_specs=[
            pl.BlockSpec((tm, d), lambda i, j: (i, 0)),
            row, row, row, hrow, hrow, tab, tab,
            pl.BlockSpec((None, d, ct), lambda i, j: (l, 0, j)),
        ],
        out_specs=[
            pl.BlockSpec((tm, ct), lambda i, j: (i, jnp.minimum(j, Q_TILES - 1))),
            pl.BlockSpec((tm, ct), lambda i, j: (i, 0)),
            pl.BlockSpec((tm, ct), lambda i, j: (i, 0)),
            pl.BlockSpec((tm, ct), lambda i, j: (i, 0)),
            pl.BlockSpec((tm, ct), lambda i, j: (i, jnp.clip(j - G_TILE0, 0, n_g - 1))),
        ],
        out_shape=[
            jax.ShapeDtypeStruct((s, ATTN_W), BF16),
            jax.ShapeDtypeStruct((s, 2 * KV_W), BF16),
            jax.ShapeDtypeStruct((s, FOURIER_W), BF16),
            jax.ShapeDtypeStruct((s, POOL_W), BF16),
            jax.ShapeDtypeStruct((s, GATE_W), BF16),
        ],
        scratch_shapes=[pltpu.VMEM((tm, d), BF16)],
        compiler_params=_params("arbitrary", "arbitrary"),
        name="inproj",
    )(x, nw, shift, scale, qg, kg, cos, sin, w_in)


def _qk(q, k):
    return lax.dot_general(q, k, (((1,), (1,)), ((), ())), preferred_element_type=F32)


def _k_cols(kvh):
    return slice(kvh * D_HEAD, (kvh + 1) * D_HEAD)


def _v_cols(kvh):
    return slice(KV_W + kvh * D_HEAD, KV_W + (kvh + 1) * D_HEAD)


def _sink_column(sink_ref, kvh, rows):
    head = lax.broadcasted_iota(jnp.int32, (GROUP * rows, 1), 0) // rows
    col = jnp.full((GROUP * rows, 1), sink_ref[kvh * GROUP], F32)
    for hh in range(1, GROUP):
        col = jnp.where(head == hh, sink_ref[kvh * GROUP + hh], col)
    return col


def _stack_heads(q_ref, kvh):
    heads = range(kvh * GROUP, (kvh + 1) * GROUP)
    return jnp.concatenate([q_ref[:, hh * D_HEAD:(hh + 1) * D_HEAD] for hh in heads], axis=0)


def _softmax_pv(scores, values, sink_col):
    def lane_tiles(a):
        return [a[:, t:t + D_HEAD] for t in range(0, a.shape[1], D_HEAD)]

    m_tile = functools.reduce(jnp.maximum, [t for s in scores for t in lane_tiles(s)])
    m = jnp.maximum(sink_col, jnp.max(m_tile, axis=-1, keepdims=True))
    e_tile = None
    acc = None
    for s, v in zip(scores, values):
        e = jnp.exp(s - m)
        for t in lane_tiles(e):
            e_tile = t if e_tile is None else e_tile + t
        pv = jnp.dot(e.astype(BF16), v, preferred_element_type=F32)
        acc = pv if acc is None else acc + pv
    denom = jnp.exp(sink_col - m) + jnp.sum(e_tile, axis=-1, keepdims=True)
    return acc / denom


def _unstack_heads(out, o_ref, kvh, rows):
    for g in range(GROUP):
        hh = kvh * GROUP + g
        o_ref[:, hh * D_HEAD:(hh + 1) * D_HEAD] = out[g * rows:(g + 1) * rows].astype(o_ref.dtype)


def _window_attn_kernel(sink_ref, q_ref, kvp_ref, kvc_ref, kvn_ref, kvx_ref, o_ref):
    i = pl.program_id(0)
    nb = pl.num_programs(0)
    shape = (GROUP * BLOCK, BLOCK)
    qi = lax.broadcasted_iota(jnp.int32, shape, 0) % BLOCK
    kj = lax.broadcasted_iota(jnp.int32, shape, 1)
    keep_prev = (kj >= qi) & (i > 0)
    keep_next = (kj <= qi) & (i < nb - 1)
    for kvh in range(N_KV_HEADS):
        q = _stack_heads(q_ref, kvh)
        ks, vs = _k_cols(kvh), _v_cols(kvh)
        s_prev = jnp.where(keep_prev, _qk(q, kvp_ref[:, ks]), NEG_INF)
        s_cur = _qk(q, kvc_ref[:, ks])
        s_next = jnp.where(keep_next, _qk(q, kvn_ref[:, ks]), NEG_INF)
        s_ctx = _qk(q, kvx_ref[:, ks])
        out = _softmax_pv([s_prev, s_cur, s_next, s_ctx],
                          [kvp_ref[:, vs], kvc_ref[:, vs], kvn_ref[:, vs], kvx_ref[:, vs]],
                          _sink_column(sink_ref, kvh, BLOCK))
        _unstack_heads(out, o_ref, kvh, BLOCK)


def _window_attn_call(q, kv, kv_ctx, sink):
    s = q.shape[0]
    c = kv_ctx.shape[0]
    nb = s // BLOCK
    kblk = lambda off: pl.BlockSpec((BLOCK, 2 * KV_W), lambda i: (jnp.clip(i + off, 0, nb - 1), 0))
    return pl.pallas_call(
        _window_attn_kernel,
        grid=(nb,),
        in_specs=[
            pl.BlockSpec(memory_space=pltpu.SMEM),
            pl.BlockSpec((BLOCK, ATTN_W), lambda i: (i, 0)),
            kblk(-1), kblk(0), kblk(1),
            pl.BlockSpec((c, 2 * KV_W), lambda i: (0, 0)),
        ],
        out_specs=pl.BlockSpec((BLOCK, ATTN_W), lambda i: (i, 0)),
        out_shape=jax.ShapeDtypeStruct((s, ATTN_W), BF16),
        compiler_params=_params("arbitrary"),
        name="window_attn",
    )(sink, q, kv, kv, kv, kv_ctx)


def _ctx_attn_kernel(sink_ref, q_ref, kv_ref, o_ref):
    rows = q_ref.shape[0]
    for kvh in range(N_KV_HEADS):
        q = _stack_heads(q_ref, kvh)
        out = _softmax_pv([_qk(q, kv_ref[:, _k_cols(kvh)])], [kv_ref[:, _v_cols(kvh)]],
                          _sink_column(sink_ref, kvh, rows))
        _unstack_heads(out, o_ref, kvh, rows)


def _ctx_attn_call(q, kv, sink):
    c = q.shape[0]
    return pl.pallas_call(
        _ctx_attn_kernel,
        grid=(1,),
        in_specs=[
            pl.BlockSpec(memory_space=pltpu.SMEM),
            pl.BlockSpec((c, ATTN_W), lambda i: (0, 0)),
            pl.BlockSpec((c, 2 * KV_W), lambda i: (0, 0)),
        ],
        out_specs=pl.BlockSpec((c, ATTN_W), lambda i: (0, 0)),
        out_shape=jax.ShapeDtypeStruct((c, ATTN_W), BF16),
        compiler_params=_params("arbitrary"),
        name="ctx_attn",
    )(sink, q, kv)


def _dft_tables(s_len):
    n1 = DFT_ROWS if s_len > 1024 else 1
    n2 = s_len // n1
    k1 = np.arange(n1)
    ang1 = 2.0 * np.pi * ((k1[:, None] * k1[None, :]) % n1) / n1
    w1 = np.concatenate([np.cos(ang1), -np.sin(ang1)], axis=0)
    k2 = np.arange(n2)
    phase = (k2[None, None, :] * (k1[:, None, None] + n1 * k2[None, :, None])) % s_len
    ang2 = 2.0 * np.pi * phase / s_len
    gr, gi = np.cos(ang2), -np.sin(ang2)
    g = np.concatenate([np.concatenate([gr, -gi], axis=2),
                        np.concatenate([gi, gr], axis=2)], axis=1)
    if n1 == 1:
        g = g[:, :, :n2]
    c = np.arange(FOURIER_GROUP_W)
    angc = 2.0 * np.pi * ((c[:, None] * c[None, :]) % FOURIER_GROUP_W) / FOURIER_GROUP_W
    norm = 1.0 / np.sqrt(float(s_len) * FOURIER_GROUP_W)
    return (w1.astype(np.float32), g.astype(np.float32),
            (np.cos(angc) * norm).astype(np.float32), (np.sin(angc) * norm).astype(np.float32))


def _dft1_kernel(w_ref, u_ref, o_ref):
    o_ref[...] = jnp.dot(w_ref[...], u_ref[...], preferred_element_type=F32).astype(o_ref.dtype)


def _dft1_call(w1, uf):
    s, cw = uf.shape
    n1 = DFT_ROWS
    cols = (s // n1) * cw
    tn = 4096
    return pl.pallas_call(
        _dft1_kernel,
        grid=(cols // tn,),
        in_specs=[pl.BlockSpec((2 * n1, n1), lambda j: (0, 0)),
                  pl.BlockSpec((n1, tn), lambda j: (0, j))],
        out_specs=pl.BlockSpec((2 * n1, tn), lambda j: (0, j)),
        out_shape=jax.ShapeDtypeStruct((2 * n1, cols), BF16),
        compiler_params=_params("arbitrary"),
        name="dft_rows",
    )(w1, uf.reshape(n1, cols))


def _dft2_kernel(g_ref, *refs):
    x_refs, (cc_ref, sc_ref, o_ref) = refs[:-3], refs[-3:]
    kb = g_ref.shape[0]
    n_in = x_refs[0].shape[0] // kb
    n = g_ref.shape[1] // 2
    cw = FOURIER_W
    for b in range(kb):
        rows = slice(b * n_in, (b + 1) * n_in)
        x = (x_refs[0][rows] if len(x_refs) == 1
             else jnp.concatenate([r[rows] for r in x_refs], axis=0))
        y = jnp.dot(g_ref[b], x, preferred_element_type=F32)
        yr, yi = y[:n].astype(BF16), y[n:].astype(BF16)
        for gi in range(N_FOURIER_GROUPS):
            sl = slice(gi * FOURIER_GROUP_W, (gi + 1) * FOURIER_GROUP_W)
            o = (jnp.dot(yr[:, sl], cc_ref[...], preferred_element_type=F32)
                 + jnp.dot(yi[:, sl], sc_ref[...], preferred_element_type=F32))
            o_ref[:, b * cw + gi * FOURIER_GROUP_W:b * cw + (gi + 1) * FOURIER_GROUP_W] = o.astype(o_ref.dtype)


def _dft2_call(g, x1, cc, sc, s_len):
    n1 = DFT_ROWS
    n2 = s_len // n1
    cw = FOURIER_W
    gw = FOURIER_GROUP_W
    kb = 4
    out = pl.pallas_call(
        _dft2_kernel,
        grid=(n1 // kb,),
        in_specs=[pl.BlockSpec((kb, 2 * n2, 2 * n2), lambda k: (k, 0, 0)),
                  pl.BlockSpec((kb * n2, cw), lambda k: (k, 0)),
                  pl.BlockSpec((kb * n2, cw), lambda k: (n1 // kb + k, 0)),
                  pl.BlockSpec((gw, gw), lambda k: (0, 0)),
                  pl.BlockSpec((gw, gw), lambda k: (0, 0))],
        out_specs=pl.BlockSpec((n2, kb * cw), lambda k: (0, k)),
        out_shape=jax.ShapeDtypeStruct((n2, n1 * cw), BF16),
        compiler_params=_params("arbitrary"),
        name="dft_cols",
    )(g, x1.reshape(2 * n1 * n2, cw), x1.reshape(2 * n1 * n2, cw), cc, sc)
    return out.reshape(s_len, cw)


def _dft_small_call(g, uf, cc, sc):
    c, cw = uf.shape
    gw = FOURIER_GROUP_W
    return pl.pallas_call(
        _dft2_kernel,
        grid=(1,),
        in_specs=[pl.BlockSpec((1, 2 * c, c), lambda k: (0, 0, 0)),
                  pl.BlockSpec((c, cw), lambda k: (0, 0)),
                  pl.BlockSpec((gw, gw), lambda k: (0, 0)),
                  pl.BlockSpec((gw, gw), lambda k: (0, 0))],
        out_specs=pl.BlockSpec((c, cw), lambda k: (0, 0)),
        out_shape=jax.ShapeDtypeStruct((c, cw), BF16),
        compiler_params=_params("arbitrary"),
        name="dft_ctx",
    )(g, uf, cc, sc)


def _merge_kernel(a_ref, f_ref, up_ref, upp_ref, upn_ref, ga_ref, gf_ref, gp_ref, x_ref,
                  gate_ref, ps_ref, wo_ref, wf_ref, wp_ref, wout_ref, o_ref, ext_ref, *, s_len):
    i = pl.program_id(0)
    tm = up_ref.shape[0]
    h = POOL_HALO
    ext_ref[0:h] = jnp.where(i > 0, upp_ref[...].astype(F32), 0.0)
    ext_ref[h:h + tm] = up_ref[...].astype(F32)
    ext_ref[h + tm:] = jnp.where(i < pl.num_programs(0) - 1, upn_ref[...].astype(F32), 0.0)

    t = i * tm + lax.broadcasted_iota(jnp.int32, (tm, 1), 0)
    pooled = []
    for gi, w in enumerate(POOL_WINDOWS):
        sl = slice(gi * POOL_GROUP_W, (gi + 1) * POOL_GROUP_W)
        total = ext_ref[h - w // 2:h - w // 2 + tm, sl]
        for off in range(-(w // 2) + 1, w - w // 2):
            total = total + ext_ref[h + off:h + off + tm, sl]
        count = jnp.minimum(t + (w - w // 2), s_len) - jnp.maximum(t - w // 2, 0)
        u = ext_ref[h:h + tm, sl]
        pooled.append((total / count.astype(F32) - u).astype(BF16))
    yp = jnp.concatenate(
        [jnp.dot(p, wp_ref[gi], preferred_element_type=F32) for gi, p in enumerate(pooled)], axis=-1)

    ya = jnp.dot(a_ref[...], wo_ref[...], preferred_element_type=F32)
    yf = jnp.dot(f_ref[...], wf_ref[...], preferred_element_type=F32)
    y = (jax.nn.sigmoid(ga_ref[...].astype(F32)) * ya
         + jax.nn.sigmoid(gf_ref[...].astype(F32)) * yf
         + jax.nn.sigmoid(gp_ref[...].astype(F32)) * (yp * ps_ref[...]))
    mix = jnp.dot(y.astype(BF16), wout_ref[...], preferred_element_type=F32)
    o_ref[...] = x_ref[...] + gate_ref[...] * mix


def _merge_call(a, f, up, g, x, gate, pool_scale, w_attn_o, w_fourier, w_pool, w_out, l, tm):
    s, d = x.shape
    h = POOL_HALO
    n_halo = s // h
    row = pl.BlockSpec((1, d), lambda i: (0, 0))
    once = dict(pipeline_mode=pl.Buffered(1))
    return pl.pallas_call(
        functools.partial(_merge_kernel, s_len=s),
        grid=(s // tm,),
        in_specs=[
            pl.BlockSpec((tm, ATTN_W), lambda i: (i, 0)),
            pl.BlockSpec((tm, FOURIER_W), lambda i: (i, 0)),
            pl.BlockSpec((tm, POOL_W), lambda i: (i, 0)),
            pl.BlockSpec((h, POOL_W), lambda i: (jnp.maximum(i * (tm // h) - 1, 0), 0)),
            pl.BlockSpec((h, POOL_W), lambda i: (jnp.minimum((i + 1) * (tm // h), n_halo - 1), 0)),
            pl.BlockSpec((tm, d), lambda i: (i, 0)),
            pl.BlockSpec((tm, d), lambda i: (i, 1)),
            pl.BlockSpec((tm, d), lambda i: (i, 2)),
            pl.BlockSpec((tm, d), lambda i: (i, 0)),
            row, row,
            pl.BlockSpec((None, ATTN_W, d), lambda i: (l, 0, 0), **once),
            pl.BlockSpec((None, FOURIER_W, d), lambda i: (l, 0, 0), **once),
            pl.BlockSpec((None, len(POOL_WINDOWS), POOL_GROUP_W, POOL_OUT_GROUP_W),
                         lambda i: (l, 0, 0, 0), **once),
            pl.BlockSpec((None, d, d), lambda i: (l, 0, 0), **once),
        ],
        out_specs=pl.BlockSpec((tm, d), lambda i: (i, 0)),
        out_shape=jax.ShapeDtypeStruct((s, d), F32),
        scratch_shapes=[pltpu.VMEM((tm + 2 * h, POOL_W), F32)],
        compiler_params=_params("arbitrary"),
        name="merge",
    )(a, f, up, up, up, g, g, g, x, gate, pool_scale, w_attn_o, w_fourier, w_pool, w_out)


def _rope_tables(s_len):
    rows_n = s_len // GRID_W
    rows = jnp.repeat(jnp.arange(rows_n), GRID_W).astype(F32)
    cols = jnp.tile(jnp.arange(GRID_W), rows_n).astype(F32)
    n_freq = D_HEAD // 4
    freqs = ROPE_BASE ** (-jnp.arange(n_freq, dtype=F32) / n_freq)
    ar, ac = rows[:, None] * freqs, cols[:, None] * freqs
    cos = jnp.concatenate([jnp.cos(ar), jnp.cos(ar), jnp.cos(ac), jnp.cos(ac)], axis=-1)
    sin = jnp.concatenate([-jnp.sin(ar), jnp.sin(ar), -jnp.sin(ac), jnp.sin(ac)], axis=-1)
    return cos, sin


def kernel(x, c, ctx, c_ctx, w_ada, b_ada, norm_w, ffn_w_gate, ffn_w_up, ffn_w_down,
           w_in, q_gain, k_gain, sink, w_attn_o, w_fourier, w_pool, pool_scale, w_out):
    b, s_len, d = x.shape
    c_len = ctx.shape[1]
    assert b == 1 and d == D_MODEL and s_len % (DFT_ROWS * BLOCK) == 0
    x, ctx = x[0], ctx[0]

    cond = jnp.zeros((8, d), F32).at[0].set(c[0]).at[1].set(c_ctx)
    mod = _ada_call(cond, w_ada, b_ada).reshape(DEPTH, 8, N_MOD, 1, d)

    ffn_f32 = (ffn_w_gate, ffn_w_up, ffn_w_down)
    ffn_w = tuple(w[0, 0].astype(BF16) for w in ffn_f32)
    w_in_b, w_ao, w_fo = w_in, w_attn_o.astype(BF16), w_fourier.astype(BF16)
    w_po, w_ou = w_pool.astype(BF16), w_out.astype(BF16)

    cos_x, sin_x = _rope_tables(s_len)
    cos_c, sin_c = jnp.ones((c_len, D_HEAD), F32), jnp.zeros((c_len, D_HEAD), F32)
    w1, g_x, cc, sc = (jnp.asarray(t).astype(BF16) for t in _dft_tables(s_len))
    _, g_c, cc_c, sc_c = (jnp.asarray(t).astype(BF16) for t in _dft_tables(c_len))

    tm_x, tm_m = 1024, 256
    for l in range(DEPTH):
        last = l == DEPTH - 1
        mx, mc = mod[l, 0], mod[l, 1]
        nw = norm_w[l][:, None, :]
        qg, kg = q_gain[l][None, :], k_gain[l][None, :]
        ps = pool_scale[l][None, :]

        x, ffn_w_post = _ffn_call(x, nw[0], mx[0], mx[1], mx[2], ffn_w, tm_x, (ffn_f32, l, 1))
        ctx, _ = _ffn_call(ctx, nw[0], mc[0], mc[1], mc[2], ffn_w, c_len)

        qx, kvx, ufx, upx, gx = _inproj_call(x, nw[1], mx[3], mx[4], qg, kg, cos_x, sin_x, w_in_b, l, tm_x)
        qc, kvc, ufc, upc, gc = _inproj_call(ctx, nw[1], mc[3], mc[4], qg, kg, cos_c, sin_c, w_in_b, l, c_len)

        ax = _window_attn_call(qx, kvx, kvc, sink[l])
        fx = _dft2_call(g_x, _dft1_call(w1, ufx), cc, sc, s_len)
        x = _merge_call(ax, fx, upx, gx, x, mx[5], ps, w_ao, w_fo, w_po, w_ou, l, tm_m)
        if not last:
            ac = _ctx_attn_call(qc, kvc, sink[l])
            fc = _dft_small_call(g_c, ufc, cc_c, sc_c)
            ctx = _merge_call(ac, fc, upc, gc, ctx, mc[5], ps, w_ao, w_fo, w_po, w_ou, l, c_len)

        x, ffn_w_next = _ffn_call(x, nw[2], mx[6], mx[7], mx[8], ffn_w_post, tm_x,
                                  None if last else (ffn_f32, l + 1, 0))
        if not last:
            ctx, _ = _ffn_call(ctx, nw[2], mc[6], mc[7], mc[8], ffn_w_post, c_len)
        ffn_w = ffn_w_next
    return x[None]
```

```python
import functools

import numpy as np
import jax
import jax.numpy as jnp
from jax import lax
from jax.experimental import pallas as pl
from jax.experimental.pallas import tpu as pltpu

D_MODEL = 2048
DEPTH = 4
GRID_W = 64
N_HEADS = 8
N_KV_HEADS = 2
GROUP = N_HEADS // N_KV_HEADS
D_HEAD = 128
ATTN_W = N_HEADS * D_HEAD
KV_W = N_KV_HEADS * D_HEAD
WINDOW = 128
BLOCK = 128
FOURIER_W = D_MODEL // 4
N_FOURIER_GROUPS = 4
FOURIER_GROUP_W = FOURIER_W // N_FOURIER_GROUPS
POOL_WINDOWS = (2, 4, 8, 16)
POOL_W = D_MODEL // 4
POOL_GROUP_W = POOL_W // len(POOL_WINDOWS)
POOL_OUT_GROUP_W = D_MODEL // len(POOL_WINDOWS)
N_BRANCHES = 3
GATE_W = N_BRANCHES * D_MODEL
IN_W = ATTN_W + 2 * KV_W + FOURIER_W + POOL_W + GATE_W
D_FF = 5632
N_MOD = 9
ROPE_BASE = 10000.0
EPS = 1e-6
NEG_INF = -1e30

BF16 = jnp.bfloat16
F32 = jnp.float32

VMEM_LIMIT_BYTES = 60 * 1024 * 1024
POOL_HALO = 16
COL_TILE = 512
FFN_TILE_F = 512
DFT_ROWS = GRID_W


def _params(*semantics):
    return pltpu.CompilerParams(dimension_semantics=semantics, vmem_limit_bytes=VMEM_LIMIT_BYTES)


def _modulate(x, gain, shift, scale):
    y = x * lax.rsqrt(jnp.mean(x * x, axis=-1, keepdims=True) + EPS)
    return (y * gain) * (1 + scale) + shift


def _ada_kernel(cond_ref, w_ref, b_ref, o_ref):
    h = jax.nn.silu(cond_ref[...]).astype(BF16)
    o_ref[0] = jnp.dot(h, w_ref[0].astype(BF16), preferred_element_type=F32) + b_ref[0]


def _ada_call(cond, w_ada, b_ada):
    depth, d, n = w_ada.shape
    tn = 1024
    return pl.pallas_call(
        _ada_kernel,
        grid=(depth, n // tn),
        in_specs=[
            pl.BlockSpec((8, d), lambda l, j: (0, 0)),
            pl.BlockSpec((1, d, tn), lambda l, j: (l, 0, j)),
            pl.BlockSpec((1, 1, tn), lambda l, j: (l, 0, j)),
        ],
        out_specs=pl.BlockSpec((1, 8, tn), lambda l, j: (l, 0, j)),
        out_shape=jax.ShapeDtypeStruct((depth, 8, n), F32),
        compiler_params=_params("arbitrary", "arbitrary"),
        name="adaln",
    )(cond, w_ada, b_ada.reshape(depth, 1, n))


def _ffn_kernel(*refs, convert_next):
    if convert_next:
        (x_hbm, nw_ref, sh_ref, sc_ref, gt_ref, wg_ref, wu_ref, wd_ref, ng_ref, nu_ref, nd_ref,
         o_hbm, cg_ref, cu_ref, cd_ref, xs_ref, acc_ref, h_ref, sem) = refs
    else:
        (x_hbm, nw_ref, sh_ref, sc_ref, gt_ref, wg_ref, wu_ref, wd_ref,
         o_hbm, xs_ref, acc_ref, h_ref, sem) = refs
    i, j = pl.program_id(0), pl.program_id(1)
    n_tiles, n_ff = pl.num_programs(0), pl.num_programs(1)
    tm = xs_ref.shape[0]

    def x_copy(tile):
        return pltpu.make_async_copy(x_hbm.at[pl.ds(tile * tm, tm)], xs_ref, sem.at[0])

    def o_copy(tile):
        return pltpu.make_async_copy(acc_ref, o_hbm.at[pl.ds(tile * tm, tm)], sem.at[1])

    def gate_up():
        h = h_ref[...]
        g = jnp.dot(h, wg_ref[...], preferred_element_type=F32)
        u = jnp.dot(h, wu_ref[...], preferred_element_type=F32)
        return (jax.nn.silu(g) * u).astype(BF16)

    def down(a):
        return (0.5 * gt_ref[...]) * jnp.dot(a, wd_ref[...], preferred_element_type=F32)

    def convert():
        if convert_next:
            cg_ref[...] = ng_ref[...].astype(BF16)
            cu_ref[...] = nu_ref[...].astype(BF16)
            cd_ref[...] = nd_ref[...].astype(BF16)

    @pl.when(j == 0)
    def _():
        @pl.when(i == 0)
        def _():
            x_copy(0).start()

        x_copy(i).wait()
        h_ref[...] = _modulate(xs_ref[...], nw_ref[...], sh_ref[...], sc_ref[...]).astype(BF16)
        a = gate_up()
        convert()

        @pl.when(i > 0)
        def _():
            o_copy(i - 1).wait()

        acc_ref[...] = xs_ref[...] + down(a)

        @pl.when(i + 1 < n_tiles)
        def _():
            x_copy(i + 1).start()

    @pl.when(j > 0)
    def _():
        acc_ref[...] += down(gate_up())
        convert()

    @pl.when(j == n_ff - 1)
    def _():
        o_copy(i).start()

        @pl.when(i == n_tiles - 1)
        def _():
            o_copy(i).wait()


def _ffn_call(x, nw, shift, scale, gate, w_bf16, tm, next_f32=None):
    s, d = x.shape
    tf = FFN_TILE_F
    n_tiles = s // tm
    row = pl.BlockSpec((1, d), lambda i, j: (0, 0))
    in_specs = [
        pl.BlockSpec(memory_space=pl.ANY),
        row, row, row, row,
        pl.BlockSpec((d, tf), lambda i, j: (0, j)),
        pl.BlockSpec((d, tf), lambda i, j: (0, j)),
        pl.BlockSpec((tf, d), lambda i, j: (j, 0)),
    ]
    out_specs = [pl.BlockSpec(memory_space=pl.ANY)]
    out_shape = [jax.ShapeDtypeStruct((s, d), F32)]
    args = [x, nw, shift, scale, gate, *w_bf16]
    if next_f32 is not None:
        (ng, nu, nd), l2, w2 = next_f32
        rb = d // n_tiles
        in_specs += [
            pl.BlockSpec((None, None, rb, tf), lambda i, j: (l2, w2, i, j)),
            pl.BlockSpec((None, None, rb, tf), lambda i, j: (l2, w2, i, j)),
            pl.BlockSpec((None, None, tf, rb), lambda i, j: (l2, w2, j, i)),
        ]
        out_specs += [
            pl.BlockSpec((rb, tf), lambda i, j: (i, j)),
            pl.BlockSpec((rb, tf), lambda i, j: (i, j)),
            pl.BlockSpec((tf, rb), lambda i, j: (j, i)),
        ]
        out_shape += [jax.ShapeDtypeStruct((d, D_FF), BF16), jax.ShapeDtypeStruct((d, D_FF), BF16),
                      jax.ShapeDtypeStruct((D_FF, d), BF16)]
        args += [ng, nu, nd]
    outs = pl.pallas_call(
        functools.partial(_ffn_kernel, convert_next=next_f32 is not None),
        grid=(n_tiles, D_FF // tf),
        in_specs=in_specs,
        out_specs=out_specs,
        out_shape=out_shape,
        scratch_shapes=[pltpu.VMEM((tm, d), F32), pltpu.VMEM((tm, d), F32), pltpu.VMEM((tm, d), BF16),
                        pltpu.SemaphoreType.DMA((2,))],
        compiler_params=_params("arbitrary", "arbitrary"),
        name="ffn",
    )(*args)
    return outs[0], (tuple(outs[1:]) if next_f32 is not None else None)


Q_TILES = ATTN_W // COL_TILE
KV_TILE = Q_TILES
UF_TILE = KV_TILE + 1
UP_TILE = UF_TILE + 1
G_TILE0 = UP_TILE + 1
N_COL_TILES = IN_W // COL_TILE


def _norm_rope(z, gain, cos, sin_signed):
    y = z * lax.rsqrt(jnp.mean(z * z, axis=-1, keepdims=True) + EPS) * gain
    lane = lax.broadcasted_iota(jnp.int32, y.shape, 1)
    first_half = (lane & (D_HEAD // 4)) == 0
    partner = jnp.where(first_half, pltpu.roll(y, D_HEAD - D_HEAD // 4, 1), pltpu.roll(y, D_HEAD // 4, 1))
    return y * cos + partner * sin_signed


def _inproj_kernel(x_ref, nw_ref, sh_ref, sc_ref, qg_ref, kg_ref, cos_ref, sin_ref, w_ref,
                   q_ref, kv_ref, uf_ref, up_ref, g_ref, h_ref, z_ref):
    j = pl.program_id(1)

    def proj():
        return jnp.dot(h_ref[...], w_ref[...].astype(BF16), preferred_element_type=F32)

    def q_epilogue(z):
        cos, sin = cos_ref[...], sin_ref[...]
        for hh in range(COL_TILE // D_HEAD):
            sl = slice(hh * D_HEAD, (hh + 1) * D_HEAD)
            q = _norm_rope(z[:, sl], qg_ref[...], cos, sin) * (D_HEAD ** -0.5)
            q_ref[:, sl] = q.astype(BF16)

    def kv_epilogue(z):
        cos, sin = cos_ref[...], sin_ref[...]
        for hh in range(N_KV_HEADS):
            sl = slice(hh * D_HEAD, (hh + 1) * D_HEAD)
            kv_ref[:, sl] = _norm_rope(z[:, sl], kg_ref[...], cos, sin).astype(BF16)
        kv_ref[:, KV_W:] = z[:, KV_W:].astype(BF16)

    @pl.when(j == 0)
    def _():
        h_ref[...] = _modulate(x_ref[...], nw_ref[...], sh_ref[...], sc_ref[...]).astype(BF16)
        z_ref[0] = proj()

    for t in range(1, KV_TILE + 1):
        @pl.when(j == t)
        def _(t=t):
            z_ref[t % 2] = proj()
            q_epilogue(z_ref[(t - 1) % 2])

    @pl.when(j == UF_TILE)
    def _():
        uf_ref[...] = proj().astype(BF16)
        kv_epilogue(z_ref[KV_TILE % 2])

    @pl.when(j == UP_TILE)
    def _():
        up_ref[...] = proj().astype(BF16)

    @pl.when(j >= G_TILE0)
    def _():
        g_ref[...] = proj().astype(BF16)


def _inproj_call(x, nw, shift, scale, qg, kg, cos, sin, w_in, l, tm):
    s, d = x.shape
    ct = COL_TILE
    row = pl.BlockSpec((1, d), lambda i, j: (0, 0))
    hrow = pl.BlockSpec((1, D_HEAD), lambda i, j: (0, 0))
    tab = pl.BlockSpec((tm, D_HEAD), lambda i, j: (i, 0))
    n_g = GATE_W // ct
    return pl.pallas_call(
        _inproj_kernel,
        grid=(s // tm, N_COL_TILES),
        in_specs=[
            pl.BlockSpec((tm, d), lambda i, j: (i, 0)),
            row, row, row, hrow, hrow, tab, tab,
            pl.BlockSpec((None, d, ct), lambda i, j: (l, 0, j)),
        ],
        out_specs=[
            pl.BlockSpec((tm, ct), lambda i, j: (i, jnp.clip(j - 1, 0, Q_TILES - 1))),
            pl.BlockSpec((tm, ct), lambda i, j: (i, 0)),
            pl.BlockSpec((tm, ct), lambda i, j: (i, 0)),
            pl.BlockSpec((tm, ct), lambda i, j: (i, 0)),
            pl.BlockSpec((tm, ct), lambda i, j: (i, jnp.clip(j - G_TILE0, 0, n_g - 1))),
        ],
        out_shape=[
            jax.ShapeDtypeStruct((s, ATTN_W), BF16),
            jax.ShapeDtypeStruct((s, 2 * KV_W), BF16),
            jax.ShapeDtypeStruct((s, FOURIER_W), BF16),
            jax.ShapeDtypeStruct((s, POOL_W), BF16),
            jax.ShapeDtypeStruct((s, GATE_W), BF16),
        ],
        scratch_shapes=[pltpu.VMEM((tm, d), BF16), pltpu.VMEM((2, tm, ct), F32)],
        compiler_params=_params("arbitrary", "arbitrary"),
        name="inproj",
    )(x, nw, shift, scale, qg, kg, cos, sin, w_in)


def _qk(q, k):
    return lax.dot_general(q, k, (((1,), (1,)), ((), ())), preferred_element_type=F32)


def _k_cols(kvh):
    return slice(kvh * D_HEAD, (kvh + 1) * D_HEAD)


def _v_cols(kvh):
    return slice(KV_W + kvh * D_HEAD, KV_W + (kvh + 1) * D_HEAD)


def _sink_column(sink_ref, kvh, rows):
    head = lax.broadcasted_iota(jnp.int32, (GROUP * rows, 1), 0) // rows
    col = jnp.full((GROUP * rows, 1), sink_ref[kvh * GROUP], F32)
    for hh in range(1, GROUP):
        col = jnp.where(head == hh, sink_ref[kvh * GROUP + hh], col)
    return col


def _stack_heads(q_ref, kvh):
    heads = range(kvh * GROUP, (kvh + 1) * GROUP)
    return jnp.concatenate([q_ref[:, hh * D_HEAD:(hh + 1) * D_HEAD] for hh in heads], axis=0)


def _softmax_pv(scores, values, sink_col):
    def lane_tiles(a):
        return [a[:, t:t + D_HEAD] for t in range(0, a.shape[1], D_HEAD)]

    m_tile = functools.reduce(jnp.maximum, [t for s in scores for t in lane_tiles(s)])
    m = jnp.maximum(sink_col, jnp.max(m_tile, axis=-1, keepdims=True))
    e_tile = None
    acc = None
    for s, v in zip(scores, values):
        e = jnp.exp(s - m)
        for t in lane_tiles(e):
            e_tile = t if e_tile is None else e_tile + t
        pv = jnp.dot(e.astype(BF16), v, preferred_element_type=F32)
        acc = pv if acc is None else acc + pv
    denom = jnp.exp(sink_col - m) + jnp.sum(e_tile, axis=-1, keepdims=True)
    return acc / denom


def _unstack_heads(out, o_ref, kvh, rows):
    for g in range(GROUP):
        hh = kvh * GROUP + g
        o_ref[:, hh * D_HEAD:(hh + 1) * D_HEAD] = out[g * rows:(g + 1) * rows].astype(o_ref.dtype)


def _window_attn_kernel(sink_ref, q_ref, kvp_ref, kvc_ref, kvn_ref, kvx_ref, o_ref):
    i = pl.program_id(0)
    nb = pl.num_programs(0)
    shape = (GROUP * BLOCK, BLOCK)
    qi = lax.broadcasted_iota(jnp.int32, shape, 0) % BLOCK
    kj = lax.broadcasted_iota(jnp.int32, shape, 1)
    keep_prev = (kj >= qi) & (i > 0)
    keep_next = (kj <= qi) & (i < nb - 1)
    for kvh in range(N_KV_HEADS):
        q = _stack_heads(q_ref, kvh)
        ks, vs = _k_cols(kvh), _v_cols(kvh)
        s_prev = jnp.where(keep_prev, _qk(q, kvp_ref[:, ks]), NEG_INF)
        s_cur = _qk(q, kvc_ref[:, ks])
        s_next = jnp.where(keep_next, _qk(q, kvn_ref[:, ks]), NEG_INF)
        s_ctx = _qk(q, kvx_ref[:, ks])
        out = _softmax_pv([s_prev, s_cur, s_next, s_ctx],
                          [kvp_ref[:, vs], kvc_ref[:, vs], kvn_ref[:, vs], kvx_ref[:, vs]],
                          _sink_column(sink_ref, kvh, BLOCK))
        _unstack_heads(out, o_ref, kvh, BLOCK)


def _window_attn_call(q, kv, kv_ctx, sink):
    s = q.shape[0]
    c = kv_ctx.shape[0]
    nb = s // BLOCK
    kblk = lambda off: pl.BlockSpec((BLOCK, 2 * KV_W), lambda i: (jnp.clip(i + off, 0, nb - 1), 0))
    return pl.pallas_call(
        _window_attn_kernel,
        grid=(nb,),
        in_specs=[
            pl.BlockSpec(memory_space=pltpu.SMEM),
            pl.BlockSpec((BLOCK, ATTN_W), lambda i: (i, 0)),
            kblk(-1), kblk(0), kblk(1),
            pl.BlockSpec((c, 2 * KV_W), lambda i: (0, 0)),
        ],
        out_specs=pl.BlockSpec((BLOCK, ATTN_W), lambda i: (i, 0)),
        out_shape=jax.ShapeDtypeStruct((s, ATTN_W), BF16),
        compiler_params=_params("arbitrary"),
        name="window_attn",
    )(sink, q, kv, kv, kv, kv_ctx)


def _ctx_attn_kernel(sink_ref, q_ref, kv_ref, o_ref):
    rows = q_ref.shape[0]
    for kvh in range(N_KV_HEADS):
        q = _stack_heads(q_ref, kvh)
        out = _softmax_pv([_qk(q, kv_ref[:, _k_cols(kvh)])], [kv_ref[:, _v_cols(kvh)]],
                          _sink_column(sink_ref, kvh, rows))
        _unstack_heads(out, o_ref, kvh, rows)


def _ctx_attn_call(q, kv, sink):
    c = q.shape[0]
    return pl.pallas_call(
        _ctx_attn_kernel,
        grid=(1,),
        in_specs=[
            pl.BlockSpec(memory_space=pltpu.SMEM),
            pl.BlockSpec((c, ATTN_W), lambda i: (0, 0)),
            pl.BlockSpec((c, 2 * KV_W), lambda i: (0, 0)),
        ],
        out_specs=pl.BlockSpec((c, ATTN_W), lambda i: (0, 0)),
        out_shape=jax.ShapeDtypeStruct((c, ATTN_W), BF16),
        compiler_params=_params("arbitrary"),
        name="ctx_attn",
    )(sink, q, kv)


def _dft_tables(s_len):
    n1 = DFT_ROWS if s_len > 1024 else 1
    n2 = s_len // n1
    k1 = np.arange(n1)
    ang1 = 2.0 * np.pi * ((k1[:, None] * k1[None, :]) % n1) / n1
    w1 = np.concatenate([np.cos(ang1), -np.sin(ang1)], axis=0)
    w1 = np.kron(w1, np.eye(DFT_S2_GROUP))
    k2 = np.arange(n2)
    phase = (k2[None, None, :] * (k1[:, None, None] + n1 * k2[None, :, None])) % s_len
    ang2 = 2.0 * np.pi * phase / s_len
    gr, gi = np.cos(ang2), -np.sin(ang2)
    g = np.concatenate([np.concatenate([gr, -gi], axis=2),
                        np.concatenate([gi, gr], axis=2)], axis=1)
    if n1 == 1:
        g = g[:, :, :n2]
    c = np.arange(FOURIER_GROUP_W)
    angc = 2.0 * np.pi * ((c[:, None] * c[None, :]) % FOURIER_GROUP_W) / FOURIER_GROUP_W
    norm = 1.0 / np.sqrt(float(s_len) * FOURIER_GROUP_W)
    return (w1.astype(np.float32), g.astype(np.float32),
            (np.cos(angc) * norm).astype(np.float32), (np.sin(angc) * norm).astype(np.float32))


DFT_S2_GROUP = 16


def _dft1_kernel(w_ref, u_ref, o_ref):
    n1, grp, cw = u_ref.shape
    y = jnp.dot(w_ref[...], u_ref[...].reshape(n1 * grp, cw), preferred_element_type=F32)
    o_ref[...] = y.astype(o_ref.dtype).reshape(o_ref.shape)


def _dft1_call(w1k, uf):
    s, cw = uf.shape
    n1 = DFT_ROWS
    n2 = s // n1
    grp = DFT_S2_GROUP
    return pl.pallas_call(
        _dft1_kernel,
        grid=(n2 // grp,),
        in_specs=[pl.BlockSpec((2 * n1 * grp, n1 * grp), lambda j: (0, 0)),
                  pl.BlockSpec((n1, grp, cw), lambda j: (0, j, 0))],
        out_specs=pl.BlockSpec((2 * n1, grp, cw), lambda j: (0, j, 0)),
        out_shape=jax.ShapeDtypeStruct((2 * n1, n2, cw), BF16),
        compiler_params=_params("arbitrary"),
        name="dft_rows",
    )(w1k, uf.reshape(n1, n2, cw))


def _dft2_kernel(g_ref, *refs):
    x_refs, (cc_ref, sc_ref, o_ref) = refs[:-3], refs[-3:]
    kb = g_ref.shape[0]
    n_in = x_refs[0].shape[0] // kb
    n = g_ref.shape[1] // 2
    cw = FOURIER_W
    for b in range(kb):
        rows = slice(b * n_in, (b + 1) * n_in)
        x = (x_refs[0][rows] if len(x_refs) == 1
             else jnp.concatenate([r[rows] for r in x_refs], axis=0))
        y = jnp.dot(g_ref[b], x, preferred_element_type=F32)
        yr, yi = y[:n].astype(BF16), y[n:].astype(BF16)
        for gi in range(N_FOURIER_GROUPS):
            sl = slice(gi * FOURIER_GROUP_W, (gi + 1) * FOURIER_GROUP_W)
            o = (jnp.dot(yr[:, sl], cc_ref[...], preferred_element_type=F32)
                 + jnp.dot(yi[:, sl], sc_ref[...], preferred_element_type=F32))
            o_ref[:, b * cw + gi * FOURIER_GROUP_W:b * cw + (gi + 1) * FOURIER_GROUP_W] = o.astype(o_ref.dtype)


def _dft2_call(g, x1, cc, sc, s_len):
    n1 = DFT_ROWS
    n2 = s_len // n1
    cw = FOURIER_W
    gw = FOURIER_GROUP_W
    kb = 4
    out = pl.pallas_call(
        _dft2_kernel,
        grid=(n1 // kb,),
        in_specs=[pl.BlockSpec((kb, 2 * n2, 2 * n2), lambda k: (k, 0, 0)),
                  pl.BlockSpec((kb * n2, cw), lambda k: (k, 0)),
                  pl.BlockSpec((kb * n2, cw), lambda k: (n1 // kb + k, 0)),
                  pl.BlockSpec((gw, gw), lambda k: (0, 0)),
                  pl.BlockSpec((gw, gw), lambda k: (0, 0))],
        out_specs=pl.BlockSpec((n2, kb * cw), lambda k: (0, k)),
        out_shape=jax.ShapeDtypeStruct((n2, n1 * cw), BF16),
        compiler_params=_params("arbitrary"),
        name="dft_cols",
    )(g, x1.reshape(2 * n1 * n2, cw), x1.reshape(2 * n1 * n2, cw), cc, sc)
    return out.reshape(s_len, cw)


def _dft_small_call(g, uf, cc, sc):
    c, cw = uf.shape
    gw = FOURIER_GROUP_W
    return pl.pallas_call(
        _dft2_kernel,
        grid=(1,),
        in_specs=[pl.BlockSpec((1, 2 * c, c), lambda k: (0, 0, 0)),
                  pl.BlockSpec((c, cw), lambda k: (0, 0)),
                  pl.BlockSpec((gw, gw), lambda k: (0, 0)),
                  pl.BlockSpec((gw, gw), lambda k: (0, 0))],
        out_specs=pl.BlockSpec((c, cw), lambda k: (0, 0)),
        out_shape=jax.ShapeDtypeStruct((c, cw), BF16),
        compiler_params=_params("arbitrary"),
        name="dft_ctx",
    )(g, uf, cc, sc)


def _merge_kernel(a_ref, f_ref, up_ref, upp_ref, upn_ref, ga_ref, gf_ref, gp_ref, x_ref,
                  gate_ref, ps_ref, wo_ref, wf_ref, wp_ref, wout_ref, o_ref, ext_ref, *, s_len):
    i = pl.program_id(0)
    tm = up_ref.shape[0]
    h = POOL_HALO
    ext_ref[0:h] = jnp.where(i > 0, upp_ref[...].astype(F32), 0.0)
    ext_ref[h:h + tm] = up_ref[...].astype(F32)
    ext_ref[h + tm:] = jnp.where(i < pl.num_programs(0) - 1, upn_ref[...].astype(F32), 0.0)

    t = i * tm + lax.broadcasted_iota(jnp.int32, (tm, 1), 0)
    pooled = []
    for gi, w in enumerate(POOL_WINDOWS):
        sl = slice(gi * POOL_GROUP_W, (gi + 1) * POOL_GROUP_W)
        total = ext_ref[h - w // 2:h - w // 2 + tm, sl]
        for off in range(-(w // 2) + 1, w - w // 2):
            total = total + ext_ref[h + off:h + off + tm, sl]
        count = jnp.minimum(t + (w - w // 2), s_len) - jnp.maximum(t - w // 2, 0)
        u = ext_ref[h:h + tm, sl]
        pooled.append((total / count.astype(F32) - u).astype(BF16))
    yp = jnp.concatenate(
        [jnp.dot(p, wp_ref[gi], preferred_element_type=F32) for gi, p in enumerate(pooled)], axis=-1)

    ya = jnp.dot(a_ref[...], wo_ref[...], preferred_element_type=F32)
    yf = jnp.dot(f_ref[...], wf_ref[...], preferred_element_type=F32)
    y = (jax.nn.sigmoid(ga_ref[...].astype(F32)) * ya
         + jax.nn.sigmoid(gf_ref[...].astype(F32)) * yf
         + jax.nn.sigmoid(gp_ref[...].astype(F32)) * (yp * ps_ref[...]))
    mix = jnp.dot(y.astype(BF16), wout_ref[...], preferred_element_type=F32)
    o_ref[...] = x_ref[...] + gate_ref[...] * mix


def _merge_call(a, f, up, g, x, gate, pool_scale, w_attn_o, w_fourier, w_pool, w_out, l, tm):
    s, d = x.shape
    h = POOL_HALO
    n_halo = s // h
    row = pl.BlockSpec((1, d), lambda i: (0, 0))
    once = dict(pipeline_mode=pl.Buffered(1))
    return pl.pallas_call(
        functools.partial(_merge_kernel, s_len=s),
        grid=(s // tm,),
        in_specs=[
            pl.BlockSpec((tm, ATTN_W), lambda i: (i, 0)),
            pl.BlockSpec((tm, FOURIER_W), lambda i: (i, 0)),
            pl.BlockSpec((tm, POOL_W), lambda i: (i, 0)),
            pl.BlockSpec((h, POOL_W), lambda i: (jnp.maximum(i * (tm // h) - 1, 0), 0)),
            pl.BlockSpec((h, POOL_W), lambda i: (jnp.minimum((i + 1) * (tm // h), n_halo - 1), 0)),
            pl.BlockSpec((tm, d), lambda i: (i, 0)),
            pl.BlockSpec((tm, d), lambda i: (i, 1)),
            pl.BlockSpec((tm, d), lambda i: (i, 2)),
            pl.BlockSpec((tm, d), lambda i: (i, 0)),
            row, row,
            pl.BlockSpec((None, ATTN_W, d), lambda i: (l, 0, 0), **once),
            pl.BlockSpec((None, FOURIER_W, d), lambda i: (l, 0, 0), **once),
            pl.BlockSpec((None, len(POOL_WINDOWS), POOL_GROUP_W, POOL_OUT_GROUP_W),
                         lambda i: (l, 0, 0, 0), **once),
            pl.BlockSpec((None, d, d), lambda i: (l, 0, 0), **once),
        ],
        out_specs=pl.BlockSpec((tm, d), lambda i: (i, 0)),
        out_shape=jax.ShapeDtypeStruct((s, d), F32),
        scratch_shapes=[pltpu.VMEM((tm + 2 * h, POOL_W), F32)],
        compiler_params=_params("arbitrary"),
        name="merge",
    )(a, f, up, up, up, g, g, g, x, gate, pool_scale, w_attn_o, w_fourier, w_pool, w_out)


def _rope_tables(s_len):
    rows_n = s_len // GRID_W
    rows = jnp.repeat(jnp.arange(rows_n), GRID_W).astype(F32)
    cols = jnp.tile(jnp.arange(GRID_W), rows_n).astype(F32)
    n_freq = D_HEAD // 4
    freqs = ROPE_BASE ** (-jnp.arange(n_freq, dtype=F32) / n_freq)
    ar, ac = rows[:, None] * freqs, cols[:, None] * freqs
    cos = jnp.concatenate([jnp.cos(ar), jnp.cos(ar), jnp.cos(ac), jnp.cos(ac)], axis=-1)
    sin = jnp.concatenate([-jnp.sin(ar), jnp.sin(ar), -jnp.sin(ac), jnp.sin(ac)], axis=-1)
    return cos, sin


def kernel(x, c, ctx, c_ctx, w_ada, b_ada, norm_w, ffn_w_gate, ffn_w_up, ffn_w_down,
           w_in, q_gain, k_gain, sink, w_attn_o, w_fourier, w_pool, pool_scale, w_out):
    b, s_len, d = x.shape
    c_len = ctx.shape[1]
    assert b == 1 and d == D_MODEL and s_len % (DFT_ROWS * BLOCK) == 0
    x, ctx = x[0], ctx[0]

    cond = jnp.zeros((8, d), F32).at[0].set(c[0]).at[1].set(c_ctx)
    mod = _ada_call(cond, w_ada, b_ada).reshape(DEPTH, 8, N_MOD, 1, d)

    ffn_f32 = (ffn_w_gate, ffn_w_up, ffn_w_down)
    ffn_w = tuple(w[0, 0].astype(BF16) for w in ffn_f32)
    w_in_b, w_ao, w_fo = w_in, w_attn_o.astype(BF16), w_fourier.astype(BF16)
    w_po, w_ou = w_pool.astype(BF16), w_out.astype(BF16)

    cos_x, sin_x = _rope_tables(s_len)
    cos_c, sin_c = jnp.ones((c_len, D_HEAD), F32), jnp.zeros((c_len, D_HEAD), F32)
    w1, g_x, cc, sc = (jnp.asarray(t).astype(BF16) for t in _dft_tables(s_len))
    _, g_c, cc_c, sc_c = (jnp.asarray(t).astype(BF16) for t in _dft_tables(c_len))

    tm_x, tm_m = 1024, 256
    for l in range(DEPTH):
        last = l == DEPTH - 1
        mx, mc = mod[l, 0], mod[l, 1]
        nw = norm_w[l][:, None, :]
        qg, kg = q_gain[l][None, :], k_gain[l][None, :]
        ps = pool_scale[l][None, :]

        x, ffn_w_post = _ffn_call(x, nw[0], mx[0], mx[1], mx[2], ffn_w, tm_x, (ffn_f32, l, 1))
        ctx, _ = _ffn_call(ctx, nw[0], mc[0], mc[1], mc[2], ffn_w, c_len)

        qx, kvx, ufx, upx, gx = _inproj_call(x, nw[1], mx[3], mx[4], qg, kg, cos_x, sin_x, w_in_b, l, tm_x)
        qc, kvc, ufc, upc, gc = _inproj_call(ctx, nw[1], mc[3], mc[4], qg, kg, cos_c, sin_c, w_in_b, l, c_len)

        ax = _window_attn_call(qx, kvx, kvc, sink[l])
        fx = _dft2_call(g_x, _dft1_call(w1, ufx), cc, sc, s_len)
        x = _merge_call(ax, fx, upx, gx, x, mx[5], ps, w_ao, w_fo, w_po, w_ou, l, tm_m)
        if not last:
            ac = _ctx_attn_call(qc, kvc, sink[l])
            fc = _dft_small_call(g_c, ufc, cc_c, sc_c)
            ctx = _merge_call(ac, fc, upc, gc, ctx, mc[5], ps, w_ao, w_fo, w_po, w_ou, l, c_len)

        x, ffn_w_next = _ffn_call(x, nw[2], mx[6], mx[7], mx[8], ffn_w_post, tm_x,
                                  None if last else (ffn_f32, l + 1, 0))
        if not last:
            ctx, _ = _ffn_call(ctx, nw[2], mc[6], mc[7], mc[8], ffn_w_post, c_len)
        ffn_w = ffn_w_next
    return x[None]
```

```python
import functools

import numpy as np
import jax
import jax.numpy as jnp
from jax import lax
from jax.experimental import pallas as pl
from jax.experimental.pallas import tpu as pltpu

D_MODEL = 2048
DEPTH = 4
GRID_W = 64
N_HEADS = 8
N_KV_HEADS = 2
GROUP = N_HEADS // N_KV_HEADS
D_HEAD = 128
ATTN_W = N_HEADS * D_HEAD
KV_W = N_KV_HEADS * D_HEAD
WINDOW = 128
BLOCK = 128
FOURIER_W = D_MODEL // 4
N_FOURIER_GROUPS = 4
FOURIER_GROUP_W = FOURIER_W // N_FOURIER_GROUPS
POOL_WINDOWS = (2, 4, 8, 16)
POOL_W = D_MODEL // 4
POOL_GROUP_W = POOL_W // len(POOL_WINDOWS)
POOL_OUT_GROUP_W = D_MODEL // len(POOL_WINDOWS)
N_BRANCHES = 3
GATE_W = N_BRANCHES * D_MODEL
IN_W = ATTN_W + 2 * KV_W + FOURIER_W + POOL_W + GATE_W
D_FF = 5632
N_MOD = 9
ROPE_BASE = 10000.0
EPS = 1e-6
NEG_INF = -1e30

BF16 = jnp.bfloat16
F32 = jnp.float32

VMEM_LIMIT_BYTES = 60 * 1024 * 1024
POOL_HALO = 16
COL_TILE = 512
FFN_TILE_F = 512
LOG2_E = 1.4426950408889634
Q_SCALE = LOG2_E * D_HEAD ** -0.5
DFT_ROWS = GRID_W


def _params(*semantics):
    return pltpu.CompilerParams(dimension_semantics=semantics, vmem_limit_bytes=VMEM_LIMIT_BYTES)


def _modulate(x, gain, shift, scale):
    y = x * lax.rsqrt(jnp.mean(x * x, axis=-1, keepdims=True) + EPS)
    return (y * gain) * (1 + scale) + shift


def _ada_kernel(cond_ref, w_ref, b_ref, o_ref):
    h = jax.nn.silu(cond_ref[...]).astype(BF16)
    o_ref[0] = jnp.dot(h, w_ref[0].astype(BF16), preferred_element_type=F32) + b_ref[0]


def _ada_call(cond, w_ada, b_ada):
    depth, d, n = w_ada.shape
    tn = 1024
    return pl.pallas_call(
        _ada_kernel,
        grid=(depth, n // tn),
        in_specs=[
            pl.BlockSpec((8, d), lambda l, j: (0, 0)),
            pl.BlockSpec((1, d, tn), lambda l, j: (l, 0, j)),
            pl.BlockSpec((1, 1, tn), lambda l, j: (l, 0, j)),
        ],
        out_specs=pl.BlockSpec((1, 8, tn), lambda l, j: (l, 0, j)),
        out_shape=jax.ShapeDtypeStruct((depth, 8, n), F32),
        compiler_params=_params("arbitrary", "arbitrary"),
        name="adaln",
    )(cond, w_ada, b_ada.reshape(depth, 1, n))


def _ffn_kernel(*refs, convert_next):
    if convert_next:
        (x_hbm, nw_ref, sh_ref, sc_ref, gt_ref, wg_ref, wu_ref, wd_ref, ng_ref, nu_ref, nd_ref,
         o_hbm, cg_ref, cu_ref, cd_ref, xs_ref, acc_ref, h_ref, sem) = refs
    else:
        (x_hbm, nw_ref, sh_ref, sc_ref, gt_ref, wg_ref, wu_ref, wd_ref,
         o_hbm, xs_ref, acc_ref, h_ref, sem) = refs
    i, j = pl.program_id(0), pl.program_id(1)
    n_tiles, n_ff = pl.num_programs(0), pl.num_programs(1)
    tm = xs_ref.shape[0]

    def x_copy(tile):
        return pltpu.make_async_copy(x_hbm.at[pl.ds(tile * tm, tm)], xs_ref, sem.at[0])

    def o_copy(tile):
        return pltpu.make_async_copy(acc_ref, o_hbm.at[pl.ds(tile * tm, tm)], sem.at[1])

    def gate_up():
        h = h_ref[...]
        g = jnp.dot(h, wg_ref[...].astype(BF16), preferred_element_type=F32)
        u = jnp.dot(h, wu_ref[...].astype(BF16), preferred_element_type=F32)
        return (jax.nn.silu(g) * u).astype(BF16)

    def down(a):
        return (0.5 * gt_ref[...]) * jnp.dot(a, wd_ref[...].astype(BF16), preferred_element_type=F32)

    def convert():
        if convert_next:
            cg_ref[...] = ng_ref[...].astype(BF16)
            cu_ref[...] = nu_ref[...].astype(BF16)
            cd_ref[...] = nd_ref[...].astype(BF16)

    @pl.when(j == 0)
    def _():
        @pl.when(i == 0)
        def _():
            x_copy(0).start()

        x_copy(i).wait()
        h_ref[...] = _modulate(xs_ref[...], nw_ref[...], sh_ref[...], sc_ref[...]).astype(BF16)
        a = gate_up()
        convert()

        @pl.when(i > 0)
        def _():
            o_copy(i - 1).wait()

        acc_ref[...] = xs_ref[...] + down(a)

        @pl.when(i + 1 < n_tiles)
        def _():
            x_copy(i + 1).start()

    @pl.when(j > 0)
    def _():
        acc_ref[...] += down(gate_up())
        convert()

    @pl.when(j == n_ff - 1)
    def _():
        o_copy(i).start()

        @pl.when(i == n_tiles - 1)
        def _():
            o_copy(i).wait()


def _ffn_call(x, nw, shift, scale, gate, w, tm, next_f32=None):
    s, d = x.shape
    n_tiles = s // tm
    row = pl.BlockSpec((1, d), lambda i, j: (0, 0))
    if not isinstance(w[0], tuple):
        tf = FFN_TILE_F
        w_args = list(w)
        w_specs = [pl.BlockSpec((d, tf), lambda i, j: (0, j)),
                   pl.BlockSpec((d, tf), lambda i, j: (0, j)),
                   pl.BlockSpec((tf, d), lambda i, j: (j, 0))]
    else:
        tf = FFN_TILE_F // 2
        w_args, l0, w0 = w
        w_args = list(w_args)
        w_specs = [pl.BlockSpec((None, None, d, tf), lambda i, j: (l0, w0, 0, j)),
                   pl.BlockSpec((None, None, d, tf), lambda i, j: (l0, w0, 0, j)),
                   pl.BlockSpec((None, None, tf, d), lambda i, j: (l0, w0, j, 0))]
    in_specs = [pl.BlockSpec(memory_space=pl.ANY), row, row, row, row, *w_specs]
    out_specs = [pl.BlockSpec(memory_space=pl.ANY)]
    out_shape = [jax.ShapeDtypeStruct((s, d), F32)]
    args = [x, nw, shift, scale, gate, *w_args]
    if next_f32 is not None:
        (ng, nu, nd), l2, w2 = next_f32
        rb = d // n_tiles
        in_specs += [
            pl.BlockSpec((None, None, rb, tf), lambda i, j: (l2, w2, i, j)),
            pl.BlockSpec((None, None, rb, tf), lambda i, j: (l2, w2, i, j)),
            pl.BlockSpec((None, None, tf, rb), lambda i, j: (l2, w2, j, i)),
        ]
        out_specs += [
            pl.BlockSpec((rb, tf), lambda i, j: (i, j)),
            pl.BlockSpec((rb, tf), lambda i, j: (i, j)),
            pl.BlockSpec((tf, rb), lambda i, j: (j, i)),
        ]
        out_shape += [jax.ShapeDtypeStruct((d, D_FF), BF16), jax.ShapeDtypeStruct((d, D_FF), BF16),
                      jax.ShapeDtypeStruct((D_FF, d), BF16)]
        args += [ng, nu, nd]
    outs = pl.pallas_call(
        functools.partial(_ffn_kernel, convert_next=next_f32 is not None),
        grid=(n_tiles, D_FF // tf),
        in_specs=in_specs,
        out_specs=out_specs,
        out_shape=out_shape,
        scratch_shapes=[pltpu.VMEM((tm, d), F32), pltpu.VMEM((tm, d), F32), pltpu.VMEM((tm, d), BF16),
                        pltpu.SemaphoreType.DMA((2,))],
        compiler_params=_params("arbitrary", "arbitrary"),
        name="ffn",
    )(*args)
    return outs[0], (tuple(outs[1:]) if next_f32 is not None else None)


Q_TILES = ATTN_W // COL_TILE
KV_TILE = Q_TILES
UF_TILE = KV_TILE + 1
UP_TILE = UF_TILE + 1
G_TILE0 = UP_TILE + 1
N_COL_TILES = IN_W // COL_TILE


def _norm_rope(z, gain, cos, sin_signed):
    y = z * lax.rsqrt(jnp.mean(z * z, axis=-1, keepdims=True) + EPS) * gain
    lane = lax.broadcasted_iota(jnp.int32, y.shape, 1)
    first_half = (lane & (D_HEAD // 4)) == 0
    partner = jnp.where(first_half, pltpu.roll(y, D_HEAD - D_HEAD // 4, 1), pltpu.roll(y, D_HEAD // 4, 1))
    return y * cos + partner * sin_signed


def _inproj_kernel(x_ref, nw_ref, sh_ref, sc_ref, qg_ref, kg_ref, cos_ref, sin_ref, w_ref,
                   q_ref, kv_ref, uf_ref, up_ref, g_ref, h_ref, z_ref):
    j = pl.program_id(1)

    def proj():
        return jnp.dot(h_ref[...], w_ref[...].astype(BF16), preferred_element_type=F32)

    def q_epilogue(z):
        cos, sin = cos_ref[...], sin_ref[...]
        for hh in range(COL_TILE // D_HEAD):
            sl = slice(hh * D_HEAD, (hh + 1) * D_HEAD)
            q = _norm_rope(z[:, sl], qg_ref[...], cos, sin) * Q_SCALE
            q_ref[:, sl] = q.astype(BF16)

    def kv_epilogue(z):
        cos, sin = cos_ref[...], sin_ref[...]
        for hh in range(N_KV_HEADS):
            sl = slice(hh * D_HEAD, (hh + 1) * D_HEAD)
            kv_ref[:, sl] = _norm_rope(z[:, sl], kg_ref[...], cos, sin).astype(BF16)
        kv_ref[:, KV_W:] = z[:, KV_W:].astype(BF16)

    @pl.when(j == 0)
    def _():
        h_ref[...] = _modulate(x_ref[...], nw_ref[...], sh_ref[...], sc_ref[...]).astype(BF16)
        z_ref[0] = proj()

    for t in range(1, KV_TILE + 1):
        @pl.when(j == t)
        def _(t=t):
            z_ref[t % 2] = proj()
            q_epilogue(z_ref[(t - 1) % 2])

    @pl.when(j == UF_TILE)
    def _():
        uf_ref[...] = proj().astype(BF16)
        kv_epilogue(z_ref[KV_TILE % 2])

    @pl.when(j == UP_TILE)
    def _():
        up_ref[...] = proj().astype(BF16)

    @pl.when(j >= G_TILE0)
    def _():
        g_ref[...] = proj().astype(BF16)


def _inproj_call(x, nw, shift, scale, qg, kg, cos, sin, w_in, l, tm):
    s, d = x.shape
    ct = COL_TILE
    row = pl.BlockSpec((1, d), lambda i, j: (0, 0))
    hrow = pl.BlockSpec((1, D_HEAD), lambda i, j: (0, 0))
    tab = pl.BlockSpec((tm, D_HEAD), lambda i, j: (i, 0))
    n_g = GATE_W // ct
    return pl.pallas_call(
        _inproj_kernel,
        grid=(s // tm, N_COL_TILES),
        in_specs=[
            pl.BlockSpec((tm, d), lambda i, j: (i, 0)),
            row, row, row, hrow, hrow, tab, tab,
            pl.BlockSpec((None, d, ct), lambda i, j: (l, 0, j)),
        ],
        out_specs=[
            pl.BlockSpec((tm, ct), lambda i, j: (i, jnp.clip(j - 1, 0, Q_TILES - 1))),
            pl.BlockSpec((tm, ct), lambda i, j: (i, 0)),
            pl.BlockSpec((tm, ct), lambda i, j: (i, 0)),
            pl.BlockSpec((tm, ct), lambda i, j: (i, 0)),
            pl.BlockSpec((tm, ct), lambda i, j: (i, jnp.clip(j - G_TILE0, 0, n_g - 1))),
        ],
        out_shape=[
            jax.ShapeDtypeStruct((s, ATTN_W), BF16),
            jax.ShapeDtypeStruct((s, 2 * KV_W), BF16),
            jax.ShapeDtypeStruct((s, FOURIER_W), BF16),
            jax.ShapeDtypeStruct((s, POOL_W), BF16),
            jax.ShapeDtypeStruct((s, GATE_W), BF16),
        ],
        scratch_shapes=[pltpu.VMEM((tm, d), BF16), pltpu.VMEM((2, tm, ct), F32)],
        compiler_params=_params("arbitrary", "arbitrary"),
        name="inproj",
    )(x, nw, shift, scale, qg, kg, cos, sin, w_in)


def _qk(q, k):
    return lax.dot_general(q, k, (((1,), (1,)), ((), ())), preferred_element_type=F32)


def _k_cols(kvh):
    return slice(kvh * D_HEAD, (kvh + 1) * D_HEAD)


def _v_cols(kvh):
    return slice(KV_W + kvh * D_HEAD, KV_W + (kvh + 1) * D_HEAD)


def _sink_column(sink_ref, kvh, rows):
    head = lax.broadcasted_iota(jnp.int32, (GROUP * rows, 1), 0) // rows
    col = jnp.full((GROUP * rows, 1), sink_ref[kvh * GROUP], F32)
    for hh in range(1, GROUP):
        col = jnp.where(head == hh, sink_ref[kvh * GROUP + hh], col)
    return col * LOG2_E


def _stack_heads(q_ref, kvh):
    heads = range(kvh * GROUP, (kvh + 1) * GROUP)
    return jnp.concatenate([q_ref[:, hh * D_HEAD:(hh + 1) * D_HEAD] for hh in heads], axis=0)


def _softmax_pv(scores, values, sink_col):
    def lane_tiles(a):
        return [a[:, t:t + D_HEAD] for t in range(0, a.shape[1], D_HEAD)]

    m_tile = functools.reduce(jnp.maximum, [t for s in scores for t in lane_tiles(s)])
    m = jnp.maximum(sink_col, jnp.max(m_tile, axis=-1, keepdims=True))
    e_tile = None
    acc = None
    for s, v in zip(scores, values):
        e = jnp.exp2(s - m)
        for t in lane_tiles(e):
            e_tile = t if e_tile is None else e_tile + t
        pv = jnp.dot(e.astype(BF16), v, preferred_element_type=F32)
        acc = pv if acc is None else acc + pv
    denom = jnp.exp2(sink_col - m) + jnp.sum(e_tile, axis=-1, keepdims=True)
    return acc / denom


def _unstack_heads(out, o_ref, kvh, rows):
    for g in range(GROUP):
        hh = kvh * GROUP + g
        o_ref[:, hh * D_HEAD:(hh + 1) * D_HEAD] = out[g * rows:(g + 1) * rows].astype(o_ref.dtype)


def _window_attn_kernel(sink_ref, q_ref, kvp_ref, kvc_ref, kvn_ref, kvx_ref, o_ref):
    i = pl.program_id(0)
    nb = pl.num_programs(0)
    shape = (GROUP * BLOCK, BLOCK)
    qi = lax.broadcasted_iota(jnp.int32, shape, 0) % BLOCK
    kj = lax.broadcasted_iota(jnp.int32, shape, 1)
    keep_prev = (kj >= qi) & (i > 0)
    keep_next = (kj <= qi) & (i < nb - 1)
    for kvh in range(N_KV_HEADS):
        q = _stack_heads(q_ref, kvh)
        ks, vs = _k_cols(kvh), _v_cols(kvh)
        s_prev = jnp.where(keep_prev, _qk(q, kvp_ref[:, ks]), NEG_INF)
        s_cur = _qk(q, kvc_ref[:, ks])
        s_next = jnp.where(keep_next, _qk(q, kvn_ref[:, ks]), NEG_INF)
        s_ctx = _qk(q, kvx_ref[:, ks])
        out = _softmax_pv([s_prev, s_cur, s_next, s_ctx],
                          [kvp_ref[:, vs], kvc_ref[:, vs], kvn_ref[:, vs], kvx_ref[:, vs]],
                          _sink_column(sink_ref, kvh, BLOCK))
        _unstack_heads(out, o_ref, kvh, BLOCK)


def _window_attn_call(q, kv, kv_ctx, sink):
    s = q.shape[0]
    c = kv_ctx.shape[0]
    nb = s // BLOCK
    kblk = lambda off: pl.BlockSpec((BLOCK, 2 * KV_W), lambda i: (jnp.clip(i + off, 0, nb - 1), 0))
    return pl.pallas_call(
        _window_attn_kernel,
        grid=(nb,),
        in_specs=[
            pl.BlockSpec(memory_space=pltpu.SMEM),
            pl.BlockSpec((BLOCK, ATTN_W), lambda i: (i, 0)),
            kblk(-1), kblk(0), kblk(1),
            pl.BlockSpec((c, 2 * KV_W), lambda i: (0, 0)),
        ],
        out_specs=pl.BlockSpec((BLOCK, ATTN_W), lambda i: (i, 0)),
        out_shape=jax.ShapeDtypeStruct((s, ATTN_W), BF16),
        compiler_params=_params("arbitrary"),
        name="window_attn",
    )(sink, q, kv, kv, kv, kv_ctx)


def _ctx_attn_kernel(sink_ref, q_ref, kv_ref, o_ref):
    rows = q_ref.shape[0]
    for kvh in range(N_KV_HEADS):
        q = _stack_heads(q_ref, kvh)
        out = _softmax_pv([_qk(q, kv_ref[:, _k_cols(kvh)])], [kv_ref[:, _v_cols(kvh)]],
                          _sink_column(sink_ref, kvh, rows))
        _unstack_heads(out, o_ref, kvh, rows)


def _ctx_attn_call(q, kv, sink):
    c = q.shape[0]
    return pl.pallas_call(
        _ctx_attn_kernel,
        grid=(1,),
        in_specs=[
            pl.BlockSpec(memory_space=pltpu.SMEM),
            pl.BlockSpec((c, ATTN_W), lambda i: (0, 0)),
            pl.BlockSpec((c, 2 * KV_W), lambda i: (0, 0)),
        ],
        out_specs=pl.BlockSpec((c, ATTN_W), lambda i: (0, 0)),
        out_shape=jax.ShapeDtypeStruct((c, ATTN_W), BF16),
        compiler_params=_params("arbitrary"),
        name="ctx_attn",
    )(sink, q, kv)


def _dft_tables(s_len):
    n1 = DFT_ROWS if s_len > 1024 else 1
    n2 = s_len // n1
    k1 = np.arange(n1)
    ang1 = 2.0 * np.pi * ((k1[:, None] * k1[None, :]) % n1) / n1
    w1 = np.concatenate([np.cos(ang1), -np.sin(ang1)], axis=0)
    w1 = np.kron(w1, np.eye(DFT_S2_GROUP))
    k2 = np.arange(n2)
    phase = (k2[None, None, :] * (k1[:, None, None] + n1 * k2[None, :, None])) % s_len
    ang2 = 2.0 * np.pi * phase / s_len
    gr, gi = np.cos(ang2), -np.sin(ang2)
    g = np.concatenate([np.concatenate([gr, -gi], axis=2),
                        np.concatenate([gi, gr], axis=2)], axis=1)
    if n1 == 1:
        g = g[:, :, :n2]
    c = np.arange(FOURIER_GROUP_W)
    angc = 2.0 * np.pi * ((c[:, None] * c[None, :]) % FOURIER_GROUP_W) / FOURIER_GROUP_W
    norm = 1.0 / np.sqrt(float(s_len) * FOURIER_GROUP_W)
    return (w1.astype(np.float32), g.astype(np.float32),
            (np.cos(angc) * norm).astype(np.float32), (np.sin(angc) * norm).astype(np.float32))


DFT_S2_GROUP = 16


def _dft1_kernel(w_ref, u_ref, o_ref):
    n1, grp, cw = u_ref.shape
    y = jnp.dot(w_ref[...], u_ref[...].reshape(n1 * grp, cw), preferred_element_type=F32)
    o_ref[...] = y.astype(o_ref.dtype).reshape(o_ref.shape)


def _dft1_call(w1k, uf):
    s, cw = uf.shape
    n1 = DFT_ROWS
    n2 = s // n1
    grp = DFT_S2_GROUP
    return pl.pallas_call(
        _dft1_kernel,
        grid=(n2 // grp,),
        in_specs=[pl.BlockSpec((2 * n1 * grp, n1 * grp), lambda j: (0, 0)),
                  pl.BlockSpec((n1, grp, cw), lambda j: (0, j, 0))],
        out_specs=pl.BlockSpec((2 * n1, grp, cw), lambda j: (0, j, 0)),
        out_shape=jax.ShapeDtypeStruct((2 * n1, n2, cw), BF16),
        compiler_params=_params("arbitrary"),
        name="dft_rows",
    )(w1k, uf.reshape(n1, n2, cw))


def _dft2_kernel(g_ref, *refs):
    x_refs, (cc_ref, sc_ref, o_ref) = refs[:-3], refs[-3:]
    kb = g_ref.shape[0]
    n_in = x_refs[0].shape[0] // kb
    n = g_ref.shape[1] // 2
    cw = FOURIER_W
    for b in range(kb):
        rows = slice(b * n_in, (b + 1) * n_in)
        x = (x_refs[0][rows] if len(x_refs) == 1
             else jnp.concatenate([r[rows] for r in x_refs], axis=0))
        y = jnp.dot(g_ref[b], x, preferred_element_type=F32)
        yr, yi = y[:n].astype(BF16), y[n:].astype(BF16)
        for gi in range(N_FOURIER_GROUPS):
            sl = slice(gi * FOURIER_GROUP_W, (gi + 1) * FOURIER_GROUP_W)
            o = (jnp.dot(yr[:, sl], cc_ref[...], preferred_element_type=F32)
                 + jnp.dot(yi[:, sl], sc_ref[...], preferred_element_type=F32))
            o_ref[:, b * cw + gi * FOURIER_GROUP_W:b * cw + (gi + 1) * FOURIER_GROUP_W] = o.astype(o_ref.dtype)


def _dft2_call(g, x1, cc, sc, s_len):
    n1 = DFT_ROWS
    n2 = s_len // n1
    cw = FOURIER_W
    gw = FOURIER_GROUP_W
    kb = 8
    out = pl.pallas_call(
        _dft2_kernel,
        grid=(n1 // kb,),
        in_specs=[pl.BlockSpec((kb, 2 * n2, 2 * n2), lambda k: (k, 0, 0)),
                  pl.BlockSpec((kb * n2, cw), lambda k: (k, 0)),
                  pl.BlockSpec((kb * n2, cw), lambda k: (n1 // kb + k, 0)),
                  pl.BlockSpec((gw, gw), lambda k: (0, 0)),
                  pl.BlockSpec((gw, gw), lambda k: (0, 0))],
        out_specs=pl.BlockSpec((n2, kb * cw), lambda k: (0, k)),
        out_shape=jax.ShapeDtypeStruct((n2, n1 * cw), BF16),
        compiler_params=_params("arbitrary"),
        name="dft_cols",
    )(g, x1.reshape(2 * n1 * n2, cw), x1.reshape(2 * n1 * n2, cw), cc, sc)
    return out.reshape(s_len, cw)


def _dft_small_call(g, uf, cc, sc):
    c, cw = uf.shape
    gw = FOURIER_GROUP_W
    return pl.pallas_call(
        _dft2_kernel,
        grid=(1,),
        in_specs=[pl.BlockSpec((1, 2 * c, c), lambda k: (0, 0, 0)),
                  pl.BlockSpec((c, cw), lambda k: (0, 0)),
                  pl.BlockSpec((gw, gw), lambda k: (0, 0)),
                  pl.BlockSpec((gw, gw), lambda k: (0, 0))],
        out_specs=pl.BlockSpec((c, cw), lambda k: (0, 0)),
        out_shape=jax.ShapeDtypeStruct((c, cw), BF16),
        compiler_params=_params("arbitrary"),
        name="dft_ctx",
    )(g, uf, cc, sc)


def _merge_kernel(a_ref, f_ref, up_ref, upp_ref, upn_ref, ga_ref, gf_ref, gp_ref, x_ref,
                  gate_ref, ps_ref, wo_ref, wf_ref, wp_ref, wout_ref, o_ref, ext_ref, *, s_len):
    i = pl.program_id(0)
    tm = up_ref.shape[0]
    h = POOL_HALO
    ext_ref[0:h] = jnp.where(i > 0, upp_ref[...].astype(F32), 0.0)
    ext_ref[h:h + tm] = up_ref[...].astype(F32)
    ext_ref[h + tm:] = jnp.where(i < pl.num_programs(0) - 1, upn_ref[...].astype(F32), 0.0)

    t = i * tm + lax.broadcasted_iota(jnp.int32, (tm, 1), 0)
    pooled = []
    for gi, w in enumerate(POOL_WINDOWS):
        sl = slice(gi * POOL_GROUP_W, (gi + 1) * POOL_GROUP_W)
        total = ext_ref[h - w // 2:h - w // 2 + tm, sl]
        for off in range(-(w // 2) + 1, w - w // 2):
            total = total + ext_ref[h + off:h + off + tm, sl]
        count = jnp.minimum(t + (w - w // 2), s_len) - jnp.maximum(t - w // 2, 0)
        u = ext_ref[h:h + tm, sl]
        pooled.append((total / count.astype(F32) - u).astype(BF16))
    yp = jnp.concatenate(
        [jnp.dot(p, wp_ref[gi], preferred_element_type=F32) for gi, p in enumerate(pooled)], axis=-1)

    ya = jnp.dot(a_ref[...], wo_ref[...], preferred_element_type=F32)
    yf = jnp.dot(f_ref[...], wf_ref[...], preferred_element_type=F32)
    y = (jax.nn.sigmoid(ga_ref[...].astype(F32)) * ya
         + jax.nn.sigmoid(gf_ref[...].astype(F32)) * yf
         + jax.nn.sigmoid(gp_ref[...].astype(F32)) * (yp * ps_ref[...]))
    mix = jnp.dot(y.astype(BF16), wout_ref[...], preferred_element_type=F32)
    o_ref[...] = x_ref[...] + gate_ref[...] * mix


def _merge_call(a, f, up, g, x, gate, pool_scale, w_attn_o, w_fourier, w_pool, w_out, l, tm):
    s, d = x.shape
    h = POOL_HALO
    n_halo = s // h
    row = pl.BlockSpec((1, d), lambda i: (0, 0))
    once = dict(pipeline_mode=pl.Buffered(1))
    return pl.pallas_call(
        functools.partial(_merge_kernel, s_len=s),
        grid=(s // tm,),
        in_specs=[
            pl.BlockSpec((tm, ATTN_W), lambda i: (i, 0)),
            pl.BlockSpec((tm, FOURIER_W), lambda i: (i, 0)),
            pl.BlockSpec((tm, POOL_W), lambda i: (i, 0)),
            pl.BlockSpec((h, POOL_W), lambda i: (jnp.maximum(i * (tm // h) - 1, 0), 0)),
            pl.BlockSpec((h, POOL_W), lambda i: (jnp.minimum((i + 1) * (tm // h), n_halo - 1), 0)),
            pl.BlockSpec((tm, d), lambda i: (i, 0)),
            pl.BlockSpec((tm, d), lambda i: (i, 1)),
            pl.BlockSpec((tm, d), lambda i: (i, 2)),
            pl.BlockSpec((tm, d), lambda i: (i, 0)),
            row, row,
            pl.BlockSpec((None, ATTN_W, d), lambda i: (l, 0, 0), **once),
            pl.BlockSpec((None, FOURIER_W, d), lambda i: (l, 0, 0), **once),
            pl.BlockSpec((None, len(POOL_WINDOWS), POOL_GROUP_W, POOL_OUT_GROUP_W),
                         lambda i: (l, 0, 0, 0), **once),
            pl.BlockSpec((None, d, d), lambda i: (l, 0, 0), **once),
        ],
        out_specs=pl.BlockSpec((tm, d), lambda i: (i, 0)),
        out_shape=jax.ShapeDtypeStruct((s, d), F32),
        scratch_shapes=[pltpu.VMEM((tm + 2 * h, POOL_W), F32)],
        compiler_params=_params("arbitrary"),
        name="merge",
    )(a, f, up, up, up, g, g, g, x, gate, pool_scale, w_attn_o, w_fourier, w_pool, w_out)


def _rope_tables(s_len):
    rows_n = s_len // GRID_W
    rows = jnp.repeat(jnp.arange(rows_n), GRID_W).astype(F32)
    cols = jnp.tile(jnp.arange(GRID_W), rows_n).astype(F32)
    n_freq = D_HEAD // 4
    freqs = ROPE_BASE ** (-jnp.arange(n_freq, dtype=F32) / n_freq)
    ar, ac = rows[:, None] * freqs, cols[:, None] * freqs
    cos = jnp.concatenate([jnp.cos(ar), jnp.cos(ar), jnp.cos(ac), jnp.cos(ac)], axis=-1)
    sin = jnp.concatenate([-jnp.sin(ar), jnp.sin(ar), -jnp.sin(ac), jnp.sin(ac)], axis=-1)
    return cos, sin


def kernel(x, c, ctx, c_ctx, w_ada, b_ada, norm_w, ffn_w_gate, ffn_w_up, ffn_w_down,
           w_in, q_gain, k_gain, sink, w_attn_o, w_fourier, w_pool, pool_scale, w_out):
    b, s_len, d = x.shape
    c_len = ctx.shape[1]
    assert b == 1 and d == D_MODEL and s_len % (DFT_ROWS * BLOCK) == 0
    x, ctx = x[0], ctx[0]

    cond = jnp.zeros((8, d), F32).at[0].set(c[0]).at[1].set(c_ctx)
    mod = _ada_call(cond, w_ada, b_ada).reshape(DEPTH, 8, N_MOD, 1, d)

    ffn_f32 = (ffn_w_gate, ffn_w_up, ffn_w_down)
    ffn_w = (ffn_f32, 0, 0)
    w_in_b, w_ao, w_fo = w_in, w_attn_o.astype(BF16), w_fourier.astype(BF16)
    w_po, w_ou = w_pool.astype(BF16), w_out.astype(BF16)

    cos_x, sin_x = _rope_tables(s_len)
    cos_c, sin_c = jnp.ones((c_len, D_HEAD), F32), jnp.zeros((c_len, D_HEAD), F32)
    w1, g_x, cc, sc = (jnp.asarray(t).astype(BF16) for t in _dft_tables(s_len))
    _, g_c, cc_c, sc_c = (jnp.asarray(t).astype(BF16) for t in _dft_tables(c_len))

    tm_x, tm_m = 1024, 256
    for l in range(DEPTH):
        last = l == DEPTH - 1
        mx, mc = mod[l, 0], mod[l, 1]
        nw = norm_w[l][:, None, :]
        qg, kg = q_gain[l][None, :], k_gain[l][None, :]
        ps = pool_scale[l][None, :]

        x, ffn_w_post = _ffn_call(x, nw[0], mx[0], mx[1], mx[2], ffn_w, tm_x, (ffn_f32, l, 1))
        ctx, _ = _ffn_call(ctx, nw[0], mc[0], mc[1], mc[2], ffn_w, c_len)

        qx, kvx, ufx, upx, gx = _inproj_call(x, nw[1], mx[3], mx[4], qg, kg, cos_x, sin_x, w_in_b, l, tm_x)
        qc, kvc, ufc, upc, gc = _inproj_call(ctx, nw[1], mc[3], mc[4], qg, kg, cos_c, sin_c, w_in_b, l, c_len)

        ax = _window_attn_call(qx, kvx, kvc, sink[l])
        fx = _dft2_call(g_x, _dft1_call(w1, ufx), cc, sc, s_len)
        x = _merge_call(ax, fx, upx, gx, x, mx[5], ps, w_ao, w_fo, w_po, w_ou, l, tm_m)
        if not last:
            ac = _ctx_attn_call(qc, kvc, sink[l])
            fc = _dft_small_call(g_c, ufc, cc_c, sc_c)
            ctx = _merge_call(ac, fc, upc, gc, ctx, mc[5], ps, w_ao, w_fo, w_po, w_ou, l, c_len)

        x, ffn_w_next = _ffn_call(x, nw[2], mx[6], mx[7], mx[8], ffn_w_post, tm_x,
                                  None if last else (ffn_f32, l + 1, 0))
        if not last:
            ctx, _ = _ffn_call(ctx, nw[2], mc[6], mc[7], mc[8], ffn_w_post, c_len)
        ffn_w = ffn_w_next
    return x[None]
```

```python
import functools

import numpy as np
import jax
import jax.numpy as jnp
from jax import lax
from jax.experimental import pallas as pl
from jax.experimental.pallas import tpu as pltpu

D_MODEL = 2048
DEPTH = 4
GRID_W = 64
N_HEADS = 8
N_KV_HEADS = 2
GROUP = N_HEADS // N_KV_HEADS
D_HEAD = 128
ATTN_W = N_HEADS * D_HEAD
KV_W = N_KV_HEADS * D_HEAD
WINDOW = 128
BLOCK = 128
FOURIER_W = D_MODEL // 4
N_FOURIER_GROUPS = 4
FOURIER_GROUP_W = FOURIER_W // N_FOURIER_GROUPS
POOL_WINDOWS = (2, 4, 8, 16)
POOL_W = D_MODEL // 4
POOL_GROUP_W = POOL_W // len(POOL_WINDOWS)
POOL_OUT_GROUP_W = D_MODEL // len(POOL_WINDOWS)
N_BRANCHES = 3
GATE_W = N_BRANCHES * D_MODEL
IN_W = ATTN_W + 2 * KV_W + FOURIER_W + POOL_W + GATE_W
D_FF = 5632
N_MOD = 9
ROPE_BASE = 10000.0
EPS = 1e-6
NEG_INF = -1e30

BF16 = jnp.bfloat16
F32 = jnp.float32

VMEM_LIMIT_BYTES = 60 * 1024 * 1024
POOL_HALO = 16
COL_TILE = 512
FFN_TILE_F = 512
LOG2_E = 1.4426950408889634
Q_SCALE = LOG2_E * D_HEAD ** -0.5
DFT_ROWS = GRID_W
DFT_S2_GROUP = 16


def _params(*semantics):
    return pltpu.CompilerParams(dimension_semantics=semantics, vmem_limit_bytes=VMEM_LIMIT_BYTES)


def _modulate(x, gain, shift, scale):
    y = x * lax.rsqrt(jnp.mean(x * x, axis=-1, keepdims=True) + EPS)
    return (y * gain) * (1 + scale) + shift


def _ada_kernel(cond_ref, w_ref, b_ref, o_ref):
    h = jax.nn.silu(cond_ref[...]).astype(BF16)
    o_ref[0] = jnp.dot(h, w_ref[0].astype(BF16), preferred_element_type=F32) + b_ref[0]


def _ada_call(cond, w_ada, b_ada):
    depth, d, n = w_ada.shape
    tn = 1024
    return pl.pallas_call(
        _ada_kernel,
        grid=(depth, n // tn),
        in_specs=[
            pl.BlockSpec((8, d), lambda l, j: (0, 0)),
            pl.BlockSpec((1, d, tn), lambda l, j: (l, 0, j)),
            pl.BlockSpec((1, 1, tn), lambda l, j: (l, 0, j)),
        ],
        out_specs=pl.BlockSpec((1, 8, tn), lambda l, j: (l, 0, j)),
        out_shape=jax.ShapeDtypeStruct((depth, 8, n), F32),
        compiler_params=_params("arbitrary", "arbitrary"),
        name="adaln",
    )(cond, w_ada, b_ada.reshape(depth, 1, n))


def _ffn_kernel(*refs, convert_next):
    if convert_next:
        (x_hbm, nw_ref, sh_ref, sc_ref, gt_ref, wg_ref, wu_ref, wd_ref, ng_ref, nu_ref, nd_ref,
         o_hbm, cg_ref, cu_ref, cd_ref, xs_ref, acc_ref, h_ref, sem) = refs
    else:
        (x_hbm, nw_ref, sh_ref, sc_ref, gt_ref, wg_ref, wu_ref, wd_ref,
         o_hbm, xs_ref, acc_ref, h_ref, sem) = refs
    i, j = pl.program_id(0), pl.program_id(1)
    n_tiles, n_ff = pl.num_programs(0), pl.num_programs(1)
    tm = xs_ref.shape[0]

    def x_copy(tile):
        return pltpu.make_async_copy(x_hbm.at[pl.ds(tile * tm, tm)], xs_ref, sem.at[0])

    def o_copy(tile):
        return pltpu.make_async_copy(acc_ref, o_hbm.at[pl.ds(tile * tm, tm)], sem.at[1])

    def gate_up():
        h = h_ref[...]
        g = jnp.dot(h, wg_ref[...].astype(BF16), preferred_element_type=F32)
        u = jnp.dot(h, wu_ref[...].astype(BF16), preferred_element_type=F32)
        return (jax.nn.silu(g) * u).astype(BF16)

    def down(a):
        return (0.5 * gt_ref[...]) * jnp.dot(a, wd_ref[...].astype(BF16), preferred_element_type=F32)

    def convert():
        if convert_next:
            cg_ref[...] = ng_ref[...].astype(BF16)
            cu_ref[...] = nu_ref[...].astype(BF16)
            cd_ref[...] = nd_ref[...].astype(BF16)

    @pl.when(j == 0)
    def _():
        @pl.when(i == 0)
        def _():
            x_copy(0).start()

        x_copy(i).wait()
        h_ref[...] = _modulate(xs_ref[...], nw_ref[...], sh_ref[...], sc_ref[...]).astype(BF16)
        a = gate_up()
        convert()

        @pl.when(i > 0)
        def _():
            o_copy(i - 1).wait()

        acc_ref[...] = xs_ref[...] + down(a)

        @pl.when(i + 1 < n_tiles)
        def _():
            x_copy(i + 1).start()

    @pl.when(j > 0)
    def _():
        acc_ref[...] += down(gate_up())
        convert()

    @pl.when(j == n_ff - 1)
    def _():
        o_copy(i).start()

        @pl.when(i == n_tiles - 1)
        def _():
            o_copy(i).wait()


def _ffn_call(x, nw, shift, scale, gate, w, tm, next_f32=None):
    s, d = x.shape
    n_tiles = s // tm
    row = pl.BlockSpec((1, d), lambda i, j: (0, 0))
    if not isinstance(w[0], tuple):
        tf = FFN_TILE_F
        w_args = list(w)
        w_specs = [pl.BlockSpec((d, tf), lambda i, j: (0, j)),
                   pl.BlockSpec((d, tf), lambda i, j: (0, j)),
                   pl.BlockSpec((tf, d), lambda i, j: (j, 0))]
    else:
        tf = FFN_TILE_F // 2
        w_args, l0, w0 = w
        w_args = list(w_args)
        w_specs = [pl.BlockSpec((None, None, d, tf), lambda i, j: (l0, w0, 0, j)),
                   pl.BlockSpec((None, None, d, tf), lambda i, j: (l0, w0, 0, j)),
                   pl.BlockSpec((None, None, tf, d), lambda i, j: (l0, w0, j, 0))]
    in_specs = [pl.BlockSpec(memory_space=pl.ANY), row, row, row, row, *w_specs]
    out_specs = [pl.BlockSpec(memory_space=pl.ANY)]
    out_shape = [jax.ShapeDtypeStruct((s, d), F32)]
    args = [x, nw, shift, scale, gate, *w_args]
    if next_f32 is not None:
        (ng, nu, nd), l2, w2 = next_f32
        rb = d // n_tiles
        in_specs += [
            pl.BlockSpec((None, None, rb, tf), lambda i, j: (l2, w2, i, j)),
            pl.BlockSpec((None, None, rb, tf), lambda i, j: (l2, w2, i, j)),
            pl.BlockSpec((None, None, tf, rb), lambda i, j: (l2, w2, j, i)),
        ]
        out_specs += [
            pl.BlockSpec((rb, tf), lambda i, j: (i, j)),
            pl.BlockSpec((rb, tf), lambda i, j: (i, j)),
            pl.BlockSpec((tf, rb), lambda i, j: (j, i)),
        ]
        out_shape += [jax.ShapeDtypeStruct((d, D_FF), BF16), jax.ShapeDtypeStruct((d, D_FF), BF16),
                      jax.ShapeDtypeStruct((D_FF, d), BF16)]
        args += [ng, nu, nd]
    outs = pl.pallas_call(
        functools.partial(_ffn_kernel, convert_next=next_f32 is not None),
        grid=(n_tiles, D_FF // tf),
        in_specs=in_specs,
        out_specs=out_specs,
        out_shape=out_shape,
        scratch_shapes=[pltpu.VMEM((tm, d), F32), pltpu.VMEM((tm, d), F32), pltpu.VMEM((tm, d), BF16),
                        pltpu.SemaphoreType.DMA((2,))],
        compiler_params=_params("arbitrary", "arbitrary"),
        name="ffn",
    )(*args)
    return outs[0], (tuple(outs[1:]) if next_f32 is not None else None)


Q_TILES = ATTN_W // COL_TILE
KV_TILE = Q_TILES
UF_TILE = KV_TILE + 1
UP_TILE = UF_TILE + 1
G_TILE0 = UP_TILE + 1
N_COL_TILES = IN_W // COL_TILE


def _norm_rope(z, gain, cos, sin_signed):
    y = z * lax.rsqrt(jnp.mean(z * z, axis=-1, keepdims=True) + EPS) * gain
    lane = lax.broadcasted_iota(jnp.int32, y.shape, 1)
    first_half = (lane & (D_HEAD // 4)) == 0
    partner = jnp.where(first_half, pltpu.roll(y, D_HEAD - D_HEAD // 4, 1), pltpu.roll(y, D_HEAD // 4, 1))
    return y * cos + partner * sin_signed


def _inproj_kernel(x_ref, nw_ref, sh_ref, sc_ref, qg_ref, kg_ref, cos_ref, sin_ref, w_ref, w2_ref,
                   q_ref, kv_ref, uf_ref, up_ref, g_ref, h_ref, z_ref):
    j = pl.program_id(1)

    def proj(w=w_ref):
        return jnp.dot(h_ref[...], w[...].astype(BF16), preferred_element_type=F32)

    def q_epilogue(z):
        cos, sin = cos_ref[...], sin_ref[...]
        for hh in range(COL_TILE // D_HEAD):
            sl = slice(hh * D_HEAD, (hh + 1) * D_HEAD)
            q = _norm_rope(z[:, sl], qg_ref[...], cos, sin) * Q_SCALE
            q_ref[:, sl] = q.astype(BF16)

    def kv_epilogue(z):
        cos, sin = cos_ref[...], sin_ref[...]
        for hh in range(N_KV_HEADS):
            sl = slice(hh * D_HEAD, (hh + 1) * D_HEAD)
            kv_ref[:, sl] = _norm_rope(z[:, sl], kg_ref[...], cos, sin).astype(BF16)
        kv_ref[:, KV_W:] = z[:, KV_W:].astype(BF16)

    @pl.when(j == 0)
    def _():
        h_ref[...] = _modulate(x_ref[...], nw_ref[...], sh_ref[...], sc_ref[...]).astype(BF16)
        z_ref[0] = proj()

    for t in range(1, KV_TILE + 1):
        @pl.when(j == t)
        def _(t=t):
            z_ref[t % 2] = proj()
            q_epilogue(z_ref[(t - 1) % 2])

    @pl.when(j == UF_TILE)
    def _():
        uf_ref[...] = proj().astype(BF16)
        kv_epilogue(z_ref[KV_TILE % 2])

    @pl.when(j == UP_TILE)
    def _():
        up_ref[...] = proj().astype(BF16)

    @pl.when(j >= G_TILE0)
    def _():
        g_ref[:, :COL_TILE] = proj().astype(BF16)
        g_ref[:, COL_TILE:] = proj(w2_ref).astype(BF16)


def _inproj_call(x, nw, shift, scale, qg, kg, cos, sin, w_in, l, tm):
    s, d = x.shape
    ct = COL_TILE
    row = pl.BlockSpec((1, d), lambda i, j: (0, 0))
    hrow = pl.BlockSpec((1, D_HEAD), lambda i, j: (0, 0))
    tab = pl.BlockSpec((tm, D_HEAD), lambda i, j: (i, 0))
    n_gp = GATE_W // (2 * ct)
    gate_pair = lambda j: jnp.maximum(j - G_TILE0, 0)
    return pl.pallas_call(
        _inproj_kernel,
        grid=(s // tm, G_TILE0 + n_gp),
        in_specs=[
            pl.BlockSpec((tm, d), lambda i, j: (i, 0)),
            row, row, row, hrow, hrow, tab, tab,
            pl.BlockSpec((None, d, ct), lambda i, j: (l, 0, jnp.minimum(j, G_TILE0) + 2 * gate_pair(j))),
            pl.BlockSpec((None, d, ct), lambda i, j: (l, 0, G_TILE0 + 1 + 2 * gate_pair(j))),
        ],
        out_specs=[
            pl.BlockSpec((tm, ct), lambda i, j: (i, jnp.clip(j - 1, 0, Q_TILES - 1))),
            pl.BlockSpec((tm, ct), lambda i, j: (i, 0)),
            pl.BlockSpec((tm, ct), lambda i, j: (i, 0)),
            pl.BlockSpec((tm, ct), lambda i, j: (i, 0)),
            pl.BlockSpec((tm, 2 * ct), lambda i, j: (i, gate_pair(j))),
        ],
        out_shape=[
            jax.ShapeDtypeStruct((s, ATTN_W), BF16),
            jax.ShapeDtypeStruct((s, 2 * KV_W), BF16),
            jax.ShapeDtypeStruct((s, FOURIER_W), BF16),
            jax.ShapeDtypeStruct((s, POOL_W), BF16),
            jax.ShapeDtypeStruct((s, GATE_W), BF16),
        ],
        scratch_shapes=[pltpu.VMEM((tm, d), BF16), pltpu.VMEM((2, tm, ct), F32)],
        compiler_params=_params("arbitrary", "arbitrary"),
        name="inproj",
    )(x, nw, shift, scale, qg, kg, cos, sin, w_in, w_in)


def _qk(q, k):
    return lax.dot_general(q, k, (((1,), (1,)), ((), ())), preferred_element_type=F32)


def _k_cols(kvh):
    return slice(kvh * D_HEAD, (kvh + 1) * D_HEAD)


def _v_cols(kvh):
    return slice(KV_W + kvh * D_HEAD, KV_W + (kvh + 1) * D_HEAD)


def _sink_column(sink_ref, kvh, rows):
    head = lax.broadcasted_iota(jnp.int32, (GROUP * rows, 1), 0) // rows
    col = jnp.full((GROUP * rows, 1), sink_ref[kvh * GROUP], F32)
    for hh in range(1, GROUP):
        col = jnp.where(head == hh, sink_ref[kvh * GROUP + hh], col)
    return col * LOG2_E


def _stack_heads(q_ref, kvh):
    heads = range(kvh * GROUP, (kvh + 1) * GROUP)
    return jnp.concatenate([q_ref[:, hh * D_HEAD:(hh + 1) * D_HEAD] for hh in heads], axis=0)


def _softmax_pv(scores, values, sink_col):
    def lane_tiles(a):
        return [a[:, t:t + D_HEAD] for t in range(0, a.shape[1], D_HEAD)]

    m_tile = functools.reduce(jnp.maximum, [t for s in scores for t in lane_tiles(s)])
    m = jnp.maximum(sink_col, jnp.max(m_tile, axis=-1, keepdims=True))
    e_tile = None
    acc = None
    for s, v in zip(scores, values):
        e = jnp.exp2(s - m)
        for t in lane_tiles(e):
            e_tile = t if e_tile is None else e_tile + t
        pv = jnp.dot(e.astype(BF16), v, preferred_element_type=F32)
        acc = pv if acc is None else acc + pv
    denom = jnp.exp2(sink_col - m) + jnp.sum(e_tile, axis=-1, keepdims=True)
    return acc / denom


def _unstack_heads(out, o_ref, kvh, rows):
    for g in range(GROUP):
        hh = kvh * GROUP + g
        o_ref[:, hh * D_HEAD:(hh + 1) * D_HEAD] = out[g * rows:(g + 1) * rows].astype(o_ref.dtype)


def _window_attn_kernel(sink_ref, q_ref, kvp_ref, kvc_ref, kvn_ref, kvx_ref, o_ref):
    i = pl.program_id(0)
    nb = pl.num_programs(0)
    shape = (GROUP * BLOCK, BLOCK)
    qi = lax.broadcasted_iota(jnp.int32, shape, 0) % BLOCK
    kj = lax.broadcasted_iota(jnp.int32, shape, 1)
    keep_prev = (kj >= qi) & (i > 0)
    keep_next = (kj <= qi) & (i < nb - 1)
    for kvh in range(N_KV_HEADS):
        q = _stack_heads(q_ref, kvh)
        ks, vs = _k_cols(kvh), _v_cols(kvh)
        s_prev = jnp.where(keep_prev, _qk(q, kvp_ref[:, ks]), NEG_INF)
        s_cur = _qk(q, kvc_ref[:, ks])
        s_next = jnp.where(keep_next, _qk(q, kvn_ref[:, ks]), NEG_INF)
        s_ctx = _qk(q, kvx_ref[:, ks])
        out = _softmax_pv([s_prev, s_cur, s_next, s_ctx],
                          [kvp_ref[:, vs], kvc_ref[:, vs], kvn_ref[:, vs], kvx_ref[:, vs]],
                          _sink_column(sink_ref, kvh, BLOCK))
        _unstack_heads(out, o_ref, kvh, BLOCK)


def _window_attn_call(q, kv, kv_ctx, sink):
    s = q.shape[0]
    c = kv_ctx.shape[0]
    nb = s // BLOCK
    kblk = lambda off: pl.BlockSpec((BLOCK, 2 * KV_W), lambda i: (jnp.clip(i + off, 0, nb - 1), 0))
    return pl.pallas_call(
        _window_attn_kernel,
        grid=(nb,),
        in_specs=[
            pl.BlockSpec(memory_space=pltpu.SMEM),
            pl.BlockSpec((BLOCK, ATTN_W), lambda i: (i, 0)),
            kblk(-1), kblk(0), kblk(1),
            pl.BlockSpec((c, 2 * KV_W), lambda i: (0, 0)),
        ],
        out_specs=pl.BlockSpec((BLOCK, ATTN_W), lambda i: (i, 0)),
        out_shape=jax.ShapeDtypeStruct((s, ATTN_W), BF16),
        compiler_params=_params("arbitrary"),
        name="window_attn",
    )(sink, q, kv, kv, kv, kv_ctx)


def _ctx_attn_kernel(sink_ref, q_ref, kv_ref, o_ref):
    rows = q_ref.shape[0]
    for kvh in range(N_KV_HEADS):
        q = _stack_heads(q_ref, kvh)
        out = _softmax_pv([_qk(q, kv_ref[:, _k_cols(kvh)])], [kv_ref[:, _v_cols(kvh)]],
                          _sink_column(sink_ref, kvh, rows))
        _unstack_heads(out, o_ref, kvh, rows)


def _ctx_attn_call(q, kv, sink):
    c = q.shape[0]
    return pl.pallas_call(
        _ctx_attn_kernel,
        grid=(1,),
        in_specs=[
            pl.BlockSpec(memory_space=pltpu.SMEM),
            pl.BlockSpec((c, ATTN_W), lambda i: (0, 0)),
            pl.BlockSpec((c, 2 * KV_W), lambda i: (0, 0)),
        ],
        out_specs=pl.BlockSpec((c, ATTN_W), lambda i: (0, 0)),
        out_shape=jax.ShapeDtypeStruct((c, ATTN_W), BF16),
        compiler_params=_params("arbitrary"),
        name="ctx_attn",
    )(sink, q, kv)


def _dft_tables(s_len):
    n1 = DFT_ROWS if s_len > 1024 else 1
    n2 = s_len // n1
    k1 = np.arange(n1)
    ang1 = 2.0 * np.pi * ((k1[:, None] * k1[None, :]) % n1) / n1
    w1 = np.concatenate([np.cos(ang1), -np.sin(ang1)], axis=0)
    w1 = np.kron(w1, np.eye(DFT_S2_GROUP))
    k2 = np.arange(n2)
    phase = (k2[None, None, :] * (k1[:, None, None] + n1 * k2[None, :, None])) % s_len
    ang2 = 2.0 * np.pi * phase / s_len
    gr, gi = np.cos(ang2), -np.sin(ang2)
    g = np.concatenate([np.concatenate([gr, -gi], axis=2),
                        np.concatenate([gi, gr], axis=2)], axis=1)
    if n1 == 1:
        g = g[:, :, :n2]
    c = np.arange(FOURIER_GROUP_W)
    angc = 2.0 * np.pi * ((c[:, None] * c[None, :]) % FOURIER_GROUP_W) / FOURIER_GROUP_W
    norm = 1.0 / np.sqrt(float(s_len) * FOURIER_GROUP_W)
    return (w1.astype(np.float32), g.astype(np.float32),
            (np.cos(angc) * norm).astype(np.float32), (np.sin(angc) * norm).astype(np.float32))


def _dft1_kernel(w_ref, u_ref, o_ref):
    n1, grp, cw = u_ref.shape
    y = jnp.dot(w_ref[...], u_ref[...].reshape(n1 * grp, cw), preferred_element_type=F32)
    o_ref[...] = y.astype(o_ref.dtype).reshape(o_ref.shape)


def _dft1_call(w1k, uf):
    s, cw = uf.shape
    n1 = DFT_ROWS
    n2 = s // n1
    grp = DFT_S2_GROUP
    return pl.pallas_call(
        _dft1_kernel,
        grid=(n2 // grp,),
        in_specs=[pl.BlockSpec((2 * n1 * grp, n1 * grp), lambda j: (0, 0)),
                  pl.BlockSpec((n1, grp, cw), lambda j: (0, j, 0))],
        out_specs=pl.BlockSpec((2 * n1, grp, cw), lambda j: (0, j, 0)),
        out_shape=jax.ShapeDtypeStruct((2 * n1, n2, cw), BF16),
        compiler_params=_params("arbitrary"),
        name="dft_rows",
    )(w1k, uf.reshape(n1, n2, cw))


def _dft2_kernel(g_ref, *refs):
    x_refs, (cc_ref, sc_ref, o_ref) = refs[:-3], refs[-3:]
    kb = g_ref.shape[0]
    n_in = x_refs[0].shape[0] // kb
    n = g_ref.shape[1] // 2
    cw = FOURIER_W
    for b in range(kb):
        rows = slice(b * n_in, (b + 1) * n_in)
        x = (x_refs[0][rows] if len(x_refs) == 1
             else jnp.concatenate([r[rows] for r in x_refs], axis=0))
        y = jnp.dot(g_ref[b], x, preferred_element_type=F32)
        yr, yi = y[:n].astype(BF16), y[n:].astype(BF16)
        for gi in range(N_FOURIER_GROUPS):
            sl = slice(gi * FOURIER_GROUP_W, (gi + 1) * FOURIER_GROUP_W)
            o = (jnp.dot(yr[:, sl], cc_ref[...], preferred_element_type=F32)
                 + jnp.dot(yi[:, sl], sc_ref[...], preferred_element_type=F32))
            o_ref[:, b * cw + gi * FOURIER_GROUP_W:b * cw + (gi + 1) * FOURIER_GROUP_W] = o.astype(o_ref.dtype)


def _dft2_call(g, x1, cc, sc, s_len):
    n1 = DFT_ROWS
    n2 = s_len // n1
    cw = FOURIER_W
    gw = FOURIER_GROUP_W
    kb = 8
    out = pl.pallas_call(
        _dft2_kernel,
        grid=(n1 // kb,),
        in_specs=[pl.BlockSpec((kb, 2 * n2, 2 * n2), lambda k: (k, 0, 0)),
                  pl.BlockSpec((kb * n2, cw), lambda k: (k, 0)),
                  pl.BlockSpec((kb * n2, cw), lambda k: (n1 // kb + k, 0)),
                  pl.BlockSpec((gw, gw), lambda k: (0, 0)),
                  pl.BlockSpec((gw, gw), lambda k: (0, 0))],
        out_specs=pl.BlockSpec((n2, kb * cw), lambda k: (0, k)),
        out_shape=jax.ShapeDtypeStruct((n2, n1 * cw), BF16),
        compiler_params=_params("arbitrary"),
        name="dft_cols",
    )(g, x1.reshape(2 * n1 * n2, cw), x1.reshape(2 * n1 * n2, cw), cc, sc)
    return out.reshape(s_len, cw)


def _dft_small_call(g, uf, cc, sc):
    c, cw = uf.shape
    gw = FOURIER_GROUP_W
    return pl.pallas_call(
        _dft2_kernel,
        grid=(1,),
        in_specs=[pl.BlockSpec((1, 2 * c, c), lambda k: (0, 0, 0)),
                  pl.BlockSpec((c, cw), lambda k: (0, 0)),
                  pl.BlockSpec((gw, gw), lambda k: (0, 0)),
                  pl.BlockSpec((gw, gw), lambda k: (0, 0))],
        out_specs=pl.BlockSpec((c, cw), lambda k: (0, 0)),
        out_shape=jax.ShapeDtypeStruct((c, cw), BF16),
        compiler_params=_params("arbitrary"),
        name="dft_ctx",
    )(g, uf, cc, sc)


def _merge_kernel(a_ref, f_ref, up_ref, upp_ref, upn_ref, ga_ref, gf_ref, gp_ref, x_ref,
                  gate_ref, ps_ref, wo_ref, wf_ref, wp_ref, wout_ref, o_ref, ext_ref, *, s_len):
    i = pl.program_id(0)
    tm = up_ref.shape[0]
    h = POOL_HALO
    ext_ref[0:h] = jnp.where(i > 0, upp_ref[...].astype(F32), 0.0)
    ext_ref[h:h + tm] = up_ref[...].astype(F32)
    ext_ref[h + tm:] = jnp.where(i < pl.num_programs(0) - 1, upn_ref[...].astype(F32), 0.0)

    t = i * tm + lax.broadcasted_iota(jnp.int32, (tm, 1), 0)
    pooled = []
    for gi, w in enumerate(POOL_WINDOWS):
        sl = slice(gi * POOL_GROUP_W, (gi + 1) * POOL_GROUP_W)
        total = ext_ref[h - w // 2:h - w // 2 + tm, sl]
        for off in range(-(w // 2) + 1, w - w // 2):
            total = total + ext_ref[h + off:h + off + tm, sl]
        count = jnp.minimum(t + (w - w // 2), s_len) - jnp.maximum(t - w // 2, 0)
        u = ext_ref[h:h + tm, sl]
        pooled.append((total / count.astype(F32) - u).astype(BF16))
    yp = jnp.concatenate(
        [jnp.dot(p, wp_ref[gi], preferred_element_type=F32) for gi, p in enumerate(pooled)], axis=-1)

    ya = jnp.dot(a_ref[...], wo_ref[...], preferred_element_type=F32)
    yf = jnp.dot(f_ref[...], wf_ref[...], preferred_element_type=F32)
    y = (jax.nn.sigmoid(ga_ref[...].astype(F32)) * ya
         + jax.nn.sigmoid(gf_ref[...].astype(F32)) * yf
         + jax.nn.sigmoid(gp_ref[...].astype(F32)) * (yp * ps_ref[...]))
    mix = jnp.dot(y.astype(BF16), wout_ref[...], preferred_element_type=F32)
    o_ref[...] = x_ref[...] + gate_ref[...] * mix


def _merge_call(a, f, up, g, x, gate, pool_scale, w_attn_o, w_fourier, w_pool, w_out, l, tm):
    s, d = x.shape
    h = POOL_HALO
    n_halo = s // h
    row = pl.BlockSpec((1, d), lambda i: (0, 0))
    once = dict(pipeline_mode=pl.Buffered(1))
    return pl.pallas_call(
        functools.partial(_merge_kernel, s_len=s),
        grid=(s // tm,),
        in_specs=[
            pl.BlockSpec((tm, ATTN_W), lambda i: (i, 0)),
            pl.BlockSpec((tm, FOURIER_W), lambda i: (i, 0)),
            pl.BlockSpec((tm, POOL_W), lambda i: (i, 0)),
            pl.BlockSpec((h, POOL_W), lambda i: (jnp.maximum(i * (tm // h) - 1, 0), 0)),
            pl.BlockSpec((h, POOL_W), lambda i: (jnp.minimum((i + 1) * (tm // h), n_halo - 1), 0)),
            pl.BlockSpec((tm, d), lambda i: (i, 0)),
            pl.BlockSpec((tm, d), lambda i: (i, 1)),
            pl.BlockSpec((tm, d), lambda i: (i, 2)),
            pl.BlockSpec((tm, d), lambda i: (i, 0)),
            row, row,
            pl.BlockSpec((None, ATTN_W, d), lambda i: (l, 0, 0), **once),
            pl.BlockSpec((None, FOURIER_W, d), lambda i: (l, 0, 0), **once),
            pl.BlockSpec((None, len(POOL_WINDOWS), POOL_GROUP_W, POOL_OUT_GROUP_W),
                         lambda i: (l, 0, 0, 0), **once),
            pl.BlockSpec((None, d, d), lambda i: (l, 0, 0), **once),
        ],
        out_specs=pl.BlockSpec((tm, d), lambda i: (i, 0)),
        out_shape=jax.ShapeDtypeStruct((s, d), F32),
        scratch_shapes=[pltpu.VMEM((tm + 2 * h, POOL_W), F32)],
        compiler_params=_params("arbitrary"),
        name="merge",
    )(a, f, up, up, up, g, g, g, x, gate, pool_scale, w_attn_o, w_fourier, w_pool, w_out)


def _rope_tables(s_len):
    rows_n = s_len // GRID_W
    rows = jnp.repeat(jnp.arange(rows_n), GRID_W).astype(F32)
    cols = jnp.tile(jnp.arange(GRID_W), rows_n).astype(F32)
    n_freq = D_HEAD // 4
    freqs = ROPE_BASE ** (-jnp.arange(n_freq, dtype=F32) / n_freq)
    ar, ac = rows[:, None] * freqs, cols[:, None] * freqs
    cos = jnp.concatenate([jnp.cos(ar), jnp.cos(ar), jnp.cos(ac), jnp.cos(ac)], axis=-1)
    sin = jnp.concatenate([-jnp.sin(ar), jnp.sin(ar), -jnp.sin(ac), jnp.sin(ac)], axis=-1)
    return cos, sin


def kernel(x, c, ctx, c_ctx, w_ada, b_ada, norm_w, ffn_w_gate, ffn_w_up, ffn_w_down,
           w_in, q_gain, k_gain, sink, w_attn_o, w_fourier, w_pool, pool_scale, w_out):
    b, s_len, d = x.shape
    c_len = ctx.shape[1]
    assert b == 1 and d == D_MODEL and s_len % (DFT_ROWS * BLOCK) == 0
    x, ctx = x[0], ctx[0]

    cond = jnp.zeros((8, d), F32).at[0].set(c[0]).at[1].set(c_ctx)
    mod = _ada_call(cond, w_ada, b_ada).reshape(DEPTH, 8, N_MOD, 1, d)

    ffn_f32 = (ffn_w_gate, ffn_w_up, ffn_w_down)
    ffn_w = (ffn_f32, 0, 0)
    w_ao, w_fo = w_attn_o.astype(BF16), w_fourier.astype(BF16)
    w_po, w_ou = w_pool.astype(BF16), w_out.astype(BF16)

    cos_x, sin_x = _rope_tables(s_len)
    cos_c, sin_c = jnp.ones((c_len, D_HEAD), F32), jnp.zeros((c_len, D_HEAD), F32)
    w1, g_x, cc, sc = (jnp.asarray(t).astype(BF16) for t in _dft_tables(s_len))
    _, g_c, cc_c, sc_c = (jnp.asarray(t).astype(BF16) for t in _dft_tables(c_len))

    tm_x, tm_m = 1024, 256
    for l in range(DEPTH):
        last = l == DEPTH - 1
        mx, mc = mod[l, 0], mod[l, 1]
        nw = norm_w[l][:, None, :]
        qg, kg = q_gain[l][None, :], k_gain[l][None, :]
        ps = pool_scale[l][None, :]

        x, ffn_w_post = _ffn_call(x, nw[0], mx[0], mx[1], mx[2], ffn_w, tm_x, (ffn_f32, l, 1))
        ctx, _ = _ffn_call(ctx, nw[0], mc[0], mc[1], mc[2], ffn_w, c_len)

        qx, kvx, ufx, upx, gx = _inproj_call(x, nw[1], mx[3], mx[4], qg, kg, cos_x, sin_x, w_in, l, tm_x)
        qc, kvc, ufc, upc, gc = _inproj_call(ctx, nw[1], mc[3], mc[4], qg, kg, cos_c, sin_c, w_in, l, c_len)

        ax = _window_attn_call(qx, kvx, kvc, sink[l])
        fx = _dft2_call(g_x, _dft1_call(w1, ufx), cc, sc, s_len)
        x = _merge_call(ax, fx, upx, gx, x, mx[5], ps, w_ao, w_fo, w_po, w_ou, l, tm_m)
        if not last:
            ac = _ctx_attn_call(qc, kvc, sink[l])
            fc = _dft_small_call(g_c, ufc, cc_c, sc_c)
            ctx = _merge_call(ac, fc, upc, gc, ctx, mc[5], ps, w_ao, w_fo, w_po, w_ou, l, c_len)

        x, ffn_w_next = _ffn_call(x, nw[2], mx[6], mx[7], mx[8], ffn_w_post, tm_x,
                                  None if last else (ffn_f32, l + 1, 0))
        if not last:
            ctx, _ = _ffn_call(ctx, nw[2], mc[6], mc[7], mc[8], ffn_w_post, c_len)
        ffn_w = ffn_w_next
    return x[None]
```

```python
import functools

import numpy as np
import jax
import jax.numpy as jnp
from jax import lax
from jax.experimental import pallas as pl
from jax.experimental.pallas import tpu as pltpu

D_MODEL = 2048
DEPTH = 4
GRID_W = 64
N_HEADS = 8
N_KV_HEADS = 2
GROUP = N_HEADS // N_KV_HEADS
D_HEAD = 128
ATTN_W = N_HEADS * D_HEAD
KV_W = N_KV_HEADS * D_HEAD
WINDOW = 128
BLOCK = 128
FOURIER_W = D_MODEL // 4
N_FOURIER_GROUPS = 4
FOURIER_GROUP_W = FOURIER_W // N_FOURIER_GROUPS
POOL_WINDOWS = (2, 4, 8, 16)
POOL_W = D_MODEL // 4
POOL_GROUP_W = POOL_W // len(POOL_WINDOWS)
POOL_OUT_GROUP_W = D_MODEL // len(POOL_WINDOWS)
N_BRANCHES = 3
GATE_W = N_BRANCHES * D_MODEL
IN_W = ATTN_W + 2 * KV_W + FOURIER_W + POOL_W + GATE_W
D_FF = 5632
N_MOD = 9
ROPE_BASE = 10000.0
EPS = 1e-6
NEG_INF = -1e30

BF16 = jnp.bfloat16
F32 = jnp.float32

VMEM_LIMIT_BYTES = 60 * 1024 * 1024
POOL_HALO = 16
COL_TILE = 512
FFN_TILE_F = 512
LOG2_E = 1.4426950408889634
Q_SCALE = LOG2_E * D_HEAD ** -0.5
DFT_ROWS = GRID_W
DFT_S2_GROUP = 16


def _params(*semantics):
    return pltpu.CompilerParams(dimension_semantics=semantics, vmem_limit_bytes=VMEM_LIMIT_BYTES)


def _modulate(x, gain, shift, scale):
    y = x * lax.rsqrt(jnp.mean(x * x, axis=-1, keepdims=True) + EPS)
    return (y * gain) * (1 + scale) + shift


def _ada_kernel(cond_ref, w_ref, b_ref, o_ref):
    h = jax.nn.silu(cond_ref[...]).astype(BF16)
    o_ref[0] = jnp.dot(h, w_ref[0].astype(BF16), preferred_element_type=F32) + b_ref[0]


def _ada_call(cond, w_ada, b_ada):
    depth, d, n = w_ada.shape
    tn = 1024
    return pl.pallas_call(
        _ada_kernel,
        grid=(depth, n // tn),
        in_specs=[
            pl.BlockSpec((8, d), lambda l, j: (0, 0)),
            pl.BlockSpec((1, d, tn), lambda l, j: (l, 0, j)),
            pl.BlockSpec((1, 1, tn), lambda l, j: (l, 0, j)),
        ],
        out_specs=pl.BlockSpec((1, 8, tn), lambda l, j: (l, 0, j)),
        out_shape=jax.ShapeDtypeStruct((depth, 8, n), F32),
        compiler_params=_params("arbitrary", "arbitrary"),
        name="adaln",
    )(cond, w_ada, b_ada.reshape(depth, 1, n))


def _ffn_kernel(*refs, convert_next):
    if convert_next:
        (x_hbm, nw_ref, sh_ref, sc_ref, gt_ref, wg_ref, wu_ref, wd_ref, ng_ref, nu_ref, nd_ref,
         o_hbm, cg_ref, cu_ref, cd_ref, xs_ref, acc_ref, h_ref, sem) = refs
    else:
        (x_hbm, nw_ref, sh_ref, sc_ref, gt_ref, wg_ref, wu_ref, wd_ref,
         o_hbm, xs_ref, acc_ref, h_ref, sem) = refs
    i, j = pl.program_id(0), pl.program_id(1)
    n_tiles, n_ff = pl.num_programs(0), pl.num_programs(1)
    tm = xs_ref.shape[0]

    def x_copy(tile):
        return pltpu.make_async_copy(x_hbm.at[pl.ds(tile * tm, tm)], xs_ref, sem.at[0])

    def o_copy(tile):
        return pltpu.make_async_copy(acc_ref, o_hbm.at[pl.ds(tile * tm, tm)], sem.at[1])

    def gate_up():
        h = h_ref[...]
        g = jnp.dot(h, wg_ref[...].astype(BF16), preferred_element_type=F32)
        u = jnp.dot(h, wu_ref[...].astype(BF16), preferred_element_type=F32)
        return (jax.nn.silu(g) * u).astype(BF16)

    def down(a):
        return (0.5 * gt_ref[...]) * jnp.dot(a, wd_ref[...].astype(BF16), preferred_element_type=F32)

    def convert():
        if convert_next:
            cg_ref[...] = ng_ref[...].astype(BF16)
            cu_ref[...] = nu_ref[...].astype(BF16)
            cd_ref[...] = nd_ref[...].astype(BF16)

    @pl.when(j == 0)
    def _():
        @pl.when(i == 0)
        def _():
            x_copy(0).start()

        x_copy(i).wait()
        h_ref[...] = _modulate(xs_ref[...], nw_ref[...], sh_ref[...], sc_ref[...]).astype(BF16)
        a = gate_up()
        convert()

        @pl.when(i > 0)
        def _():
            o_copy(i - 1).wait()

        acc_ref[...] = xs_ref[...] + down(a)

        @pl.when(i + 1 < n_tiles)
        def _():
            x_copy(i + 1).start()

    @pl.when(j > 0)
    def _():
        acc_ref[...] += down(gate_up())
        convert()

    @pl.when(j == n_ff - 1)
    def _():
        o_copy(i).start()

        @pl.when(i == n_tiles - 1)
        def _():
            o_copy(i).wait()


def _ffn_call(x, nw, shift, scale, gate, w, tm, next_f32=None):
    s, d = x.shape
    n_tiles = s // tm
    row = pl.BlockSpec((1, d), lambda i, j: (0, 0))
    if not isinstance(w[0], tuple):
        tf = FFN_TILE_F
        w_args = list(w)
        w_specs = [pl.BlockSpec((d, tf), lambda i, j: (0, j)),
                   pl.BlockSpec((d, tf), lambda i, j: (0, j)),
                   pl.BlockSpec((tf, d), lambda i, j: (j, 0))]
    else:
        tf = FFN_TILE_F
        w_args, l0, w0 = w
        w_args = list(w_args)
        w_specs = [pl.BlockSpec((None, None, d, tf), lambda i, j: (l0, w0, 0, j)),
                   pl.BlockSpec((None, None, d, tf), lambda i, j: (l0, w0, 0, j)),
                   pl.BlockSpec((None, None, tf, d), lambda i, j: (l0, w0, j, 0))]
    in_specs = [pl.BlockSpec(memory_space=pl.ANY), row, row, row, row, *w_specs]
    out_specs = [pl.BlockSpec(memory_space=pl.ANY)]
    out_shape = [jax.ShapeDtypeStruct((s, d), F32)]
    args = [x, nw, shift, scale, gate, *w_args]
    if next_f32 is not None:
        (ng, nu, nd), l2, w2 = next_f32
        rb = d // n_tiles
        in_specs += [
            pl.BlockSpec((None, None, rb, tf), lambda i, j: (l2, w2, i, j)),
            pl.BlockSpec((None, None, rb, tf), lambda i, j: (l2, w2, i, j)),
            pl.BlockSpec((None, None, tf, rb), lambda i, j: (l2, w2, j, i)),
        ]
        out_specs += [
            pl.BlockSpec((rb, tf), lambda i, j: (i, j)),
            pl.BlockSpec((rb, tf), lambda i, j: (i, j)),
            pl.BlockSpec((tf, rb), lambda i, j: (j, i)),
        ]
        out_shape += [jax.ShapeDtypeStruct((d, D_FF), BF16), jax.ShapeDtypeStruct((d, D_FF), BF16),
                      jax.ShapeDtypeStruct((D_FF, d), BF16)]
        args += [ng, nu, nd]
    outs = pl.pallas_call(
        functools.partial(_ffn_kernel, convert_next=next_f32 is not None),
        grid=(n_tiles, D_FF // tf),
        in_specs=in_specs,
        out_specs=out_specs,
        out_shape=out_shape,
        scratch_shapes=[pltpu.VMEM((tm, d), F32), pltpu.VMEM((tm, d), F32), pltpu.VMEM((tm, d), BF16),
                        pltpu.SemaphoreType.DMA((2,))],
        compiler_params=_params("arbitrary", "arbitrary"),
        name="ffn",
    )(*args)
    return outs[0], (tuple(outs[1:]) if next_f32 is not None else None)


Q_TILES = ATTN_W // COL_TILE
KV_TILE = Q_TILES
UF_TILE = KV_TILE + 1
UP_TILE = UF_TILE + 1
G_TILE0 = UP_TILE + 1
N_COL_TILES = IN_W // COL_TILE


def _norm_rope(z, gain, cos, sin_signed):
    y = z * lax.rsqrt(jnp.mean(z * z, axis=-1, keepdims=True) + EPS) * gain
    lane = lax.broadcasted_iota(jnp.int32, y.shape, 1)
    first_half = (lane & (D_HEAD // 4)) == 0
    partner = jnp.where(first_half, pltpu.roll(y, D_HEAD - D_HEAD // 4, 1), pltpu.roll(y, D_HEAD // 4, 1))
    return y * cos + partner * sin_signed


STEP_Q, STEP_KV_UF, STEP_UP, STEP_G0 = 0, 1, 2, 3


def _inproj_kernel(x_ref, nw_ref, sh_ref, sc_ref, qg_ref, kg_ref, cos_ref, sin_ref, w_ref, w2_ref,
                   q_ref, kv_ref, uf_ref, up_ref, g_ref, h_ref, z_ref):
    j = pl.program_id(1)

    def proj(w):
        return jnp.dot(h_ref[...], w[...].astype(BF16), preferred_element_type=F32)

    def q_epilogue(z, col0):
        cos, sin = cos_ref[...], sin_ref[...]
        for hh in range(COL_TILE // D_HEAD):
            sl = slice(hh * D_HEAD, (hh + 1) * D_HEAD)
            q = _norm_rope(z[:, sl], qg_ref[...], cos, sin) * Q_SCALE
            q_ref[:, col0 + hh * D_HEAD:col0 + (hh + 1) * D_HEAD] = q.astype(BF16)

    def kv_epilogue(z):
        cos, sin = cos_ref[...], sin_ref[...]
        for hh in range(N_KV_HEADS):
            sl = slice(hh * D_HEAD, (hh + 1) * D_HEAD)
            kv_ref[:, sl] = _norm_rope(z[:, sl], kg_ref[...], cos, sin).astype(BF16)
        kv_ref[:, KV_W:] = z[:, KV_W:].astype(BF16)

    @pl.when(j == STEP_Q)
    def _():
        h_ref[...] = _modulate(x_ref[...], nw_ref[...], sh_ref[...], sc_ref[...]).astype(BF16)
        z_ref[0] = proj(w_ref)
        z_ref[1] = proj(w2_ref)

    @pl.when(j == STEP_KV_UF)
    def _():
        z_ref[2] = proj(w_ref)
        uf_ref[...] = proj(w2_ref).astype(BF16)
        for t in range(Q_TILES):
            q_epilogue(z_ref[t], t * COL_TILE)

    @pl.when(j == STEP_UP)
    def _():
        up_ref[...] = proj(w_ref).astype(BF16)
        kv_epilogue(z_ref[2])

    @pl.when(j >= STEP_G0)
    def _():
        g_ref[:, :COL_TILE] = proj(w_ref).astype(BF16)
        g_ref[:, COL_TILE:] = proj(w2_ref).astype(BF16)


def _inproj_call(x, nw, shift, scale, qg, kg, cos, sin, w_in, l, tm):
    s, d = x.shape
    ct = COL_TILE
    assert (Q_TILES, KV_TILE, UF_TILE, UP_TILE, G_TILE0) == (2, 2, 3, 4, 5) and (N_COL_TILES - G_TILE0) % 2 == 0
    row = pl.BlockSpec((1, d), lambda i, j: (0, 0))
    hrow = pl.BlockSpec((1, D_HEAD), lambda i, j: (0, 0))
    tab = pl.BlockSpec((tm, D_HEAD), lambda i, j: (i, 0))
    n_steps = STEP_G0 + (N_COL_TILES - G_TILE0) // 2
    tile_a = lambda j: 2 * j - (j >= STEP_G0).astype(jnp.int32)
    tile_b = lambda j: jnp.where(j < STEP_UP, 2 * j + 1, jnp.maximum(2 * j, G_TILE0 + 1))
    return pl.pallas_call(
        _inproj_kernel,
        grid=(s // tm, n_steps),
        in_specs=[
            pl.BlockSpec((tm, d), lambda i, j: (i, 0)),
            row, row, row, hrow, hrow, tab, tab,
            pl.BlockSpec((None, d, ct), lambda i, j: (l, 0, tile_a(j))),
            pl.BlockSpec((None, d, ct), lambda i, j: (l, 0, tile_b(j))),
        ],
        out_specs=[
            pl.BlockSpec((tm, Q_TILES * ct), lambda i, j: (i, 0)),
            pl.BlockSpec((tm, ct), lambda i, j: (i, 0)),
            pl.BlockSpec((tm, ct), lambda i, j: (i, 0)),
            pl.BlockSpec((tm, ct), lambda i, j: (i, 0)),
            pl.BlockSpec((tm, 2 * ct), lambda i, j: (i, jnp.maximum(j - STEP_G0, 0))),
        ],
        out_shape=[
            jax.ShapeDtypeStruct((s, ATTN_W), BF16),
            jax.ShapeDtypeStruct((s, 2 * KV_W), BF16),
            jax.ShapeDtypeStruct((s, FOURIER_W), BF16),
            jax.ShapeDtypeStruct((s, POOL_W), BF16),
            jax.ShapeDtypeStruct((s, GATE_W), BF16),
        ],
        scratch_shapes=[pltpu.VMEM((tm, d), BF16), pltpu.VMEM((3, tm, ct), F32)],
        compiler_params=_params("arbitrary", "arbitrary"),
        name="inproj",
    )(x, nw, shift, scale, qg, kg, cos, sin, w_in, w_in)


def _qk(q, k):
    return lax.dot_general(q, k, (((1,), (1,)), ((), ())), preferred_element_type=F32)


def _k_cols(kvh):
    return slice(kvh * D_HEAD, (kvh + 1) * D_HEAD)


def _v_cols(kvh):
    return slice(KV_W + kvh * D_HEAD, KV_W + (kvh + 1) * D_HEAD)


def _sink_column(sink_ref, kvh, rows):
    head = lax.broadcasted_iota(jnp.int32, (GROUP * rows, 1), 0) // rows
    col = jnp.full((GROUP * rows, 1), sink_ref[kvh * GROUP], F32)
    for hh in range(1, GROUP):
        col = jnp.where(head == hh, sink_ref[kvh * GROUP + hh], col)
    return col * LOG2_E


def _stack_heads(q_ref, kvh):
    heads = range(kvh * GROUP, (kvh + 1) * GROUP)
    return jnp.concatenate([q_ref[:, hh * D_HEAD:(hh + 1) * D_HEAD] for hh in heads], axis=0)


def _softmax_pv(scores, values, sink_col):
    def lane_tiles(a):
        return [a[:, t:t + D_HEAD] for t in range(0, a.shape[1], D_HEAD)]

    m_tile = functools.reduce(jnp.maximum, [t for s in scores for t in lane_tiles(s)])
    m = jnp.maximum(sink_col, jnp.max(m_tile, axis=-1, keepdims=True))
    e_tile = None
    acc = None
    for s, v in zip(scores, values):
        e = jnp.exp2(s - m)
        for t in lane_tiles(e):
            e_tile = t if e_tile is None else e_tile + t
        pv = jnp.dot(e.astype(BF16), v, preferred_element_type=F32)
        acc = pv if acc is None else acc + pv
    denom = jnp.exp2(sink_col - m) + jnp.sum(e_tile, axis=-1, keepdims=True)
    return acc / denom


def _unstack_heads(out, o_ref, kvh, rows):
    for g in range(GROUP):
        hh = kvh * GROUP + g
        o_ref[:, hh * D_HEAD:(hh + 1) * D_HEAD] = out[g * rows:(g + 1) * rows].astype(o_ref.dtype)


ATTN_Q_BLOCKS = 2


def _window_attn_kernel(sink_ref, q_ref, *refs):
    kv_refs, kvx_ref, o_ref = refs[:-2], refs[-2], refs[-1]
    i = pl.program_id(0)
    n_steps = pl.num_programs(0)
    shape = (GROUP * BLOCK, BLOCK)
    qi = lax.broadcasted_iota(jnp.int32, shape, 0) % BLOCK
    kj = lax.broadcasted_iota(jnp.int32, shape, 1)
    for b in range(ATTN_Q_BLOCKS):
        kvp_ref, kvc_ref, kvn_ref = kv_refs[b:b + 3]
        keep_prev = (kj >= qi) if b > 0 else (kj >= qi) & (i > 0)
        keep_next = (kj <= qi) if b < ATTN_Q_BLOCKS - 1 else (kj <= qi) & (i < n_steps - 1)
        rows = slice(b * BLOCK, (b + 1) * BLOCK)
        for kvh in range(N_KV_HEADS):
            q = _stack_heads(q_ref.at[rows], kvh)
            ks, vs = _k_cols(kvh), _v_cols(kvh)
            s_prev = jnp.where(keep_prev, _qk(q, kvp_ref[:, ks]), NEG_INF)
            s_cur = _qk(q, kvc_ref[:, ks])
            s_next = jnp.where(keep_next, _qk(q, kvn_ref[:, ks]), NEG_INF)
            s_ctx = _qk(q, kvx_ref[:, ks])
            out = _softmax_pv([s_prev, s_cur, s_next, s_ctx],
                              [kvp_ref[:, vs], kvc_ref[:, vs], kvn_ref[:, vs], kvx_ref[:, vs]],
                              _sink_column(sink_ref, kvh, BLOCK))
            _unstack_heads(out, o_ref.at[rows], kvh, BLOCK)


def _window_attn_call(q, kv, kv_ctx, sink):
    s = q.shape[0]
    c = kv_ctx.shape[0]
    nb = s // BLOCK
    qb = ATTN_Q_BLOCKS
    kblk = lambda off: pl.BlockSpec((BLOCK, 2 * KV_W), lambda i: (jnp.clip(i * qb + off, 0, nb - 1), 0))
    return pl.pallas_call(
        _window_attn_kernel,
        grid=(nb // qb,),
        in_specs=[
            pl.BlockSpec(memory_space=pltpu.SMEM),
            pl.BlockSpec((qb * BLOCK, ATTN_W), lambda i: (i, 0)),
            *[kblk(off) for off in range(-1, qb + 1)],
            pl.BlockSpec((c, 2 * KV_W), lambda i: (0, 0)),
        ],
        out_specs=pl.BlockSpec((qb * BLOCK, ATTN_W), lambda i: (i, 0)),
        out_shape=jax.ShapeDtypeStruct((s, ATTN_W), BF16),
        compiler_params=_params("arbitrary"),
        name="window_attn",
    )(sink, q, *([kv] * (qb + 2)), kv_ctx)


def _ctx_attn_kernel(sink_ref, q_ref, kv_ref, o_ref):
    rows = q_ref.shape[0]
    for kvh in range(N_KV_HEADS):
        q = _stack_heads(q_ref, kvh)
        out = _softmax_pv([_qk(q, kv_ref[:, _k_cols(kvh)])], [kv_ref[:, _v_cols(kvh)]],
                          _sink_column(sink_ref, kvh, rows))
        _unstack_heads(out, o_ref, kvh, rows)


def _ctx_attn_call(q, kv, sink):
    c = q.shape[0]
    return pl.pallas_call(
        _ctx_attn_kernel,
        grid=(1,),
        in_specs=[
            pl.BlockSpec(memory_space=pltpu.SMEM),
            pl.BlockSpec((c, ATTN_W), lambda i: (0, 0)),
            pl.BlockSpec((c, 2 * KV_W), lambda i: (0, 0)),
        ],
        out_specs=pl.BlockSpec((c, ATTN_W), lambda i: (0, 0)),
        out_shape=jax.ShapeDtypeStruct((c, ATTN_W), BF16),
        compiler_params=_params("arbitrary"),
        name="ctx_attn",
    )(sink, q, kv)


def _dft_tables(s_len):
    n1 = DFT_ROWS if s_len > 1024 else 1
    n2 = s_len // n1
    k1 = np.arange(n1)
    ang1 = 2.0 * np.pi * ((k1[:, None] * k1[None, :]) % n1) / n1
    w1 = np.concatenate([np.cos(ang1), -np.sin(ang1)], axis=0)
    w1 = np.kron(w1, np.eye(DFT_S2_GROUP))
    k2 = np.arange(n2)
    phase = (k2[None, None, :] * (k1[:, None, None] + n1 * k2[None, :, None])) % s_len
    ang2 = 2.0 * np.pi * phase / s_len
    gr, gi = np.cos(ang2), -np.sin(ang2)
    g = np.concatenate([np.concatenate([gr, -gi], axis=2),
                        np.concatenate([gi, gr], axis=2)], axis=1)
    if n1 == 1:
        g = g[:, :, :n2]
    c = np.arange(FOURIER_GROUP_W)
    angc = 2.0 * np.pi * ((c[:, None] * c[None, :]) % FOURIER_GROUP_W) / FOURIER_GROUP_W
    norm = 1.0 / np.sqrt(float(s_len) * FOURIER_GROUP_W)
    return (w1.astype(np.float32), g.astype(np.float32),
            (np.cos(angc) * norm).astype(np.float32), (np.sin(angc) * norm).astype(np.float32))


def _dft1_kernel(w_ref, u_ref, o_ref):
    n1, grp, cw = u_ref.shape
    y = jnp.dot(w_ref[...], u_ref[...].reshape(n1 * grp, cw), preferred_element_type=F32)
    o_ref[...] = y.astype(o_ref.dtype).reshape(o_ref.shape)


def _dft1_call(w1k, uf):
    s, cw = uf.shape
    n1 = DFT_ROWS
    n2 = s // n1
    grp = DFT_S2_GROUP
    return pl.pallas_call(
        _dft1_kernel,
        grid=(n2 // grp,),
        in_specs=[pl.BlockSpec((2 * n1 * grp, n1 * grp), lambda j: (0, 0)),
                  pl.BlockSpec((n1, grp, cw), lambda j: (0, j, 0))],
        out_specs=pl.BlockSpec((2 * n1, grp, cw), lambda j: (0, j, 0)),
        out_shape=jax.ShapeDtypeStruct((2 * n1, n2, cw), BF16),
        compiler_params=_params("arbitrary"),
        name="dft_rows",
    )(w1k, uf.reshape(n1, n2, cw))


def _dft2_kernel(g_ref, *refs):
    x_refs, (cc_ref, sc_ref, o_ref) = refs[:-3], refs[-3:]
    kb = g_ref.shape[0]
    n_in = x_refs[0].shape[0] // kb
    n = g_ref.shape[1] // 2
    cw = FOURIER_W
    for b in range(kb):
        rows = slice(b * n_in, (b + 1) * n_in)
        x = (x_refs[0][rows] if len(x_refs) == 1
             else jnp.concatenate([r[rows] for r in x_refs], axis=0))
        y = jnp.dot(g_ref[b], x, preferred_element_type=F32)
        yr, yi = y[:n].astype(BF16), y[n:].astype(BF16)
        for gi in range(N_FOURIER_GROUPS):
            sl = slice(gi * FOURIER_GROUP_W, (gi + 1) * FOURIER_GROUP_W)
            o = (jnp.dot(yr[:, sl], cc_ref[...], preferred_element_type=F32)
                 + jnp.dot(yi[:, sl], sc_ref[...], preferred_element_type=F32))
            o_ref[:, b * cw + gi * FOURIER_GROUP_W:b * cw + (gi + 1) * FOURIER_GROUP_W] = o.astype(o_ref.dtype)


def _dft2_call(g, x1, cc, sc, s_len):
    n1 = DFT_ROWS
    n2 = s_len // n1
    cw = FOURIER_W
    gw = FOURIER_GROUP_W
    kb = 8
    out = pl.pallas_call(
        _dft2_kernel,
        grid=(n1 // kb,),
        in_specs=[pl.BlockSpec((kb, 2 * n2, 2 * n2), lambda k: (k, 0, 0)),
                  pl.BlockSpec((kb * n2, cw), lambda k: (k, 0)),
                  pl.BlockSpec((kb * n2, cw), lambda k: (n1 // kb + k, 0)),
                  pl.BlockSpec((gw, gw), lambda k: (0, 0)),
                  pl.BlockSpec((gw, gw), lambda k: (0, 0))],
        out_specs=pl.BlockSpec((n2, kb * cw), lambda k: (0, k)),
        out_shape=jax.ShapeDtypeStruct((n2, n1 * cw), BF16),
        compiler_params=_params("arbitrary"),
        name="dft_cols",
    )(g, x1.reshape(2 * n1 * n2, cw), x1.reshape(2 * n1 * n2, cw), cc, sc)
    return out.reshape(s_len, cw)


def _dft_small_call(g, uf, cc, sc):
    c, cw = uf.shape
    gw = FOURIER_GROUP_W
    return pl.pallas_call(
        _dft2_kernel,
        grid=(1,),
        in_specs=[pl.BlockSpec((1, 2 * c, c), lambda k: (0, 0, 0)),
                  pl.BlockSpec((c, cw), lambda k: (0, 0)),
                  pl.BlockSpec((gw, gw), lambda k: (0, 0)),
                  pl.BlockSpec((gw, gw), lambda k: (0, 0))],
        out_specs=pl.BlockSpec((c, cw), lambda k: (0, 0)),
        out_shape=jax.ShapeDtypeStruct((c, cw), BF16),
        compiler_params=_params("arbitrary"),
        name="dft_ctx",
    )(g, uf, cc, sc)


def _merge_kernel(a_ref, f_ref, up_ref, upp_ref, upn_ref, ga_ref, gf_ref, gp_ref, x_ref,
                  gate_ref, ps_ref, wo_ref, wf_ref, wp_ref, wout_ref, o_ref, ext_ref, *, s_len):
    i = pl.program_id(0)
    tm = up_ref.shape[0]
    h = POOL_HALO
    ext_ref[0:h] = jnp.where(i > 0, upp_ref[...].astype(F32), 0.0)
    ext_ref[h:h + tm] = up_ref[...].astype(F32)
    ext_ref[h + tm:] = jnp.where(i < pl.num_programs(0) - 1, upn_ref[...].astype(F32), 0.0)

    t = i * tm + lax.broadcasted_iota(jnp.int32, (tm, 1), 0)
    pooled = []
    for gi, w in enumerate(POOL_WINDOWS):
        sl = slice(gi * POOL_GROUP_W, (gi + 1) * POOL_GROUP_W)
        total = ext_ref[h - w // 2:h - w // 2 + tm, sl]
        for off in range(-(w // 2) + 1, w - w // 2):
            total = total + ext_ref[h + off:h + off + tm, sl]
        count = jnp.minimum(t + (w - w // 2), s_len) - jnp.maximum(t - w // 2, 0)
        u = ext_ref[h:h + tm, sl]
        pooled.append((total / count.astype(F32) - u).astype(BF16))
    yp = jnp.concatenate(
        [jnp.dot(p, wp_ref[gi], preferred_element_type=F32) for gi, p in enumerate(pooled)], axis=-1)

    ya = jnp.dot(a_ref[...], wo_ref[...], preferred_element_type=F32)
    yf = jnp.dot(f_ref[...], wf_ref[...], preferred_element_type=F32)
    y = (jax.nn.sigmoid(ga_ref[...].astype(F32)) * ya
         + jax.nn.sigmoid(gf_ref[...].astype(F32)) * yf
         + jax.nn.sigmoid(gp_ref[...].astype(F32)) * (yp * ps_ref[...]))
    mix = jnp.dot(y.astype(BF16), wout_ref[...], preferred_element_type=F32)
    o_ref[...] = x_ref[...] + gate_ref[...] * mix


def _merge_call(a, f, up, g, x, gate, pool_scale, w_attn_o, w_fourier, w_pool, w_out, l, tm):
    s, d = x.shape
    h = POOL_HALO
    n_halo = s // h
    row = pl.BlockSpec((1, d), lambda i: (0, 0))
    once = dict(pipeline_mode=pl.Buffered(1))
    return pl.pallas_call(
        functools.partial(_merge_kernel, s_len=s),
        grid=(s // tm,),
        in_specs=[
            pl.BlockSpec((tm, ATTN_W), lambda i: (i, 0)),
            pl.BlockSpec((tm, FOURIER_W), lambda i: (i, 0)),
            pl.BlockSpec((tm, POOL_W), lambda i: (i, 0)),
            pl.BlockSpec((h, POOL_W), lambda i: (jnp.maximum(i * (tm // h) - 1, 0), 0)),
            pl.BlockSpec((h, POOL_W), lambda i: (jnp.minimum((i + 1) * (tm // h), n_halo - 1), 0)),
            pl.BlockSpec((tm, d), lambda i: (i, 0)),
            pl.BlockSpec((tm, d), lambda i: (i, 1)),
            pl.BlockSpec((tm, d), lambda i: (i, 2)),
            pl.BlockSpec((tm, d), lambda i: (i, 0)),
            row, row,
            pl.BlockSpec((None, ATTN_W, d), lambda i: (l, 0, 0), **once),
            pl.BlockSpec((None, FOURIER_W, d), lambda i: (l, 0, 0), **once),
            pl.BlockSpec((None, len(POOL_WINDOWS), POOL_GROUP_W, POOL_OUT_GROUP_W),
                         lambda i: (l, 0, 0, 0), **once),
            pl.BlockSpec((None, d, d), lambda i: (l, 0, 0), **once),
        ],
        out_specs=pl.BlockSpec((tm, d), lambda i: (i, 0)),
        out_shape=jax.ShapeDtypeStruct((s, d), F32),
        scratch_shapes=[pltpu.VMEM((tm + 2 * h, POOL_W), F32)],
        compiler_params=_params("arbitrary"),
        name="merge",
    )(a, f, up, up, up, g, g, g, x, gate, pool_scale, w_attn_o, w_fourier, w_pool, w_out)


def _rope_tables(s_len):
    rows_n = s_len // GRID_W
    rows = jnp.repeat(jnp.arange(rows_n), GRID_W).astype(F32)
    cols = jnp.tile(jnp.arange(GRID_W), rows_n).astype(F32)
    n_freq = D_HEAD // 4
    freqs = ROPE_BASE ** (-jnp.arange(n_freq, dtype=F32) / n_freq)
    ar, ac = rows[:, None] * freqs, cols[:, None] * freqs
    cos = jnp.concatenate([jnp.cos(ar), jnp.cos(ar), jnp.cos(ac), jnp.cos(ac)], axis=-1)
    sin = jnp.concatenate([-jnp.sin(ar), jnp.sin(ar), -jnp.sin(ac), jnp.sin(ac)], axis=-1)
    return cos, sin


def kernel(x, c, ctx, c_ctx, w_ada, b_ada, norm_w, ffn_w_gate, ffn_w_up, ffn_w_down,
           w_in, q_gain, k_gain, sink, w_attn_o, w_fourier, w_pool, pool_scale, w_out):
    b, s_len, d = x.shape
    c_len = ctx.shape[1]
    assert b == 1 and d == D_MODEL and s_len % (DFT_ROWS * BLOCK) == 0
    x, ctx = x[0], ctx[0]

    cond = jnp.zeros((8, d), F32).at[0].set(c[0]).at[1].set(c_ctx)
    mod = _ada_call(cond, w_ada, b_ada).reshape(DEPTH, 8, N_MOD, 1, d)

    ffn_f32 = (ffn_w_gate, ffn_w_up, ffn_w_down)
    ffn_w = (ffn_f32, 0, 0)
    w_ao, w_fo = w_attn_o.astype(BF16), w_fourier.astype(BF16)
    w_po, w_ou = w_pool.astype(BF16), w_out.astype(BF16)

    cos_x, sin_x = _rope_tables(s_len)
    cos_c, sin_c = jnp.ones((c_len, D_HEAD), F32), jnp.zeros((c_len, D_HEAD), F32)
    w1, g_x, cc, sc = (jnp.asarray(t).astype(BF16) for t in _dft_tables(s_len))
    _, g_c, cc_c, sc_c = (jnp.asarray(t).astype(BF16) for t in _dft_tables(c_len))

    tm_x, tm_m = 1024, 256
    for l in range(DEPTH):
        last = l == DEPTH - 1
        mx, mc = mod[l, 0], mod[l, 1]
        nw = norm_w[l][:, None, :]
        qg, kg = q_gain[l][None, :], k_gain[l][None, :]
        ps = pool_scale[l][None, :]

        x, ffn_w_post = _ffn_call(x, nw[0], mx[0], mx[1], mx[2], ffn_w, tm_x, (ffn_f32, l, 1))
        ctx, _ = _ffn_call(ctx, nw[0], mc[0], mc[1], mc[2], ffn_w, c_len)

        qx, kvx, ufx, upx, gx = _inproj_call(x, nw[1], mx[3], mx[4], qg, kg, cos_x, sin_x, w_in, l, tm_x)
        qc, kvc, ufc, upc, gc = _inproj_call(ctx, nw[1], mc[3], mc[4], qg, kg, cos_c, sin_c, w_in, l, c_len)

        ax = _window_attn_call(qx, kvx, kvc, sink[l])
        fx = _dft2_call(g_x, _dft1_call(w1, ufx), cc, sc, s_len)
        x = _merge_call(ax, fx, upx, gx, x, mx[5], ps, w_ao, w_fo, w_po, w_ou, l, tm_m)
        if not last:
            ac = _ctx_attn_call(qc, kvc, sink[l])
            fc = _dft_small_call(g_c, ufc, cc_c, sc_c)
            ctx = _merge_call(ac, fc, upc, gc, ctx, mc[5], ps, w_ao, w_fo, w_po, w_ou, l, c_len)

        x, ffn_w_next = _ffn_call(x, nw[2], mx[6], mx[7], mx[8], ffn_w_post, tm_x,
                                  None if last else (ffn_f32, l + 1, 0))
        if not last:
            ctx, _ = _ffn_call(ctx, nw[2], mc[6], mc[7], mc[8], ffn_w_post, c_len)
        ffn_w = ffn_w_next
    return x[None]
```

```python
import functools

import numpy as np
import jax
import jax.numpy as jnp
from jax import lax
from jax.experimental import pallas as pl
from jax.experimental.pallas import tpu as pltpu

D_MODEL = 2048
DEPTH = 4
GRID_W = 64
N_HEADS = 8
N_KV_HEADS = 2
GROUP = N_HEADS // N_KV_HEADS
D_HEAD = 128
ATTN_W = N_HEADS * D_HEAD
KV_W = N_KV_HEADS * D_HEAD
WINDOW = 128
BLOCK = 128
FOURIER_W = D_MODEL // 4
N_FOURIER_GROUPS = 4
FOURIER_GROUP_W = FOURIER_W // N_FOURIER_GROUPS
POOL_WINDOWS = (2, 4, 8, 16)
POOL_W = D_MODEL // 4
POOL_GROUP_W = POOL_W // len(POOL_WINDOWS)
POOL_OUT_GROUP_W = D_MODEL // len(POOL_WINDOWS)
N_BRANCHES = 3
GATE_W = N_BRANCHES * D_MODEL
IN_W = ATTN_W + 2 * KV_W + FOURIER_W + POOL_W + GATE_W
D_FF = 5632
N_MOD = 9
ROPE_BASE = 10000.0
EPS = 1e-6
NEG_INF = -1e30

BF16 = jnp.bfloat16
F32 = jnp.float32

VMEM_LIMIT_BYTES = 60 * 1024 * 1024
POOL_HALO = 16
COL_TILE = 512
FFN_TILE_F = 512
LOG2_E = 1.4426950408889634
Q_SCALE = LOG2_E * D_HEAD ** -0.5
DFT_ROWS = GRID_W
DFT_S2_GROUP = 16


def _params(*semantics):
    return pltpu.CompilerParams(dimension_semantics=semantics, vmem_limit_bytes=VMEM_LIMIT_BYTES)


def _modulate(x, gain, shift, scale):
    y = x * lax.rsqrt(jnp.mean(x * x, axis=-1, keepdims=True) + EPS)
    return (y * gain) * (1 + scale) + shift


def _ada_kernel(cond_ref, w_ref, b_ref, o_ref):
    h = jax.nn.silu(cond_ref[...]).astype(BF16)
    o_ref[0] = jnp.dot(h, w_ref[0].astype(BF16), preferred_element_type=F32) + b_ref[0]


def _ada_call(cond, w_ada, b_ada):
    depth, d, n = w_ada.shape
    tn = 1024
    return pl.pallas_call(
        _ada_kernel,
        grid=(depth, n // tn),
        in_specs=[
            pl.BlockSpec((8, d), lambda l, j: (0, 0)),
            pl.BlockSpec((1, d, tn), lambda l, j: (l, 0, j)),
            pl.BlockSpec((1, 1, tn), lambda l, j: (l, 0, j)),
        ],
        out_specs=pl.BlockSpec((1, 8, tn), lambda l, j: (l, 0, j)),
        out_shape=jax.ShapeDtypeStruct((depth, 8, n), F32),
        compiler_params=_params("arbitrary", "arbitrary"),
        name="adaln",
    )(cond, w_ada, b_ada.reshape(depth, 1, n))


def _ffn_kernel(*refs, convert_next):
    if convert_next:
        (x_hbm, nw_ref, sh_ref, sc_ref, gt_ref, wg_ref, wu_ref, wd_ref, ng_ref, nu_ref, nd_ref,
         o_hbm, cg_ref, cu_ref, cd_ref, xs_ref, acc_ref, h_ref, sem) = refs
    else:
        (x_hbm, nw_ref, sh_ref, sc_ref, gt_ref, wg_ref, wu_ref, wd_ref,
         o_hbm, xs_ref, acc_ref, h_ref, sem) = refs
    i, j = pl.program_id(0), pl.program_id(1)
    n_tiles, n_ff = pl.num_programs(0), pl.num_programs(1)
    tm = xs_ref.shape[0]

    def x_copy(tile):
        return pltpu.make_async_copy(x_hbm.at[pl.ds(tile * tm, tm)], xs_ref, sem.at[0])

    def o_copy(tile):
        return pltpu.make_async_copy(acc_ref, o_hbm.at[pl.ds(tile * tm, tm)], sem.at[1])

    def gate_up():
        h = h_ref[...]
        g = jnp.dot(h, wg_ref[...].astype(BF16), preferred_element_type=F32)
        u = jnp.dot(h, wu_ref[...].astype(BF16), preferred_element_type=F32)
        return (jax.nn.silu(g) * u).astype(BF16)

    def down(a):
        return (0.5 * gt_ref[...]) * jnp.dot(a, wd_ref[...].astype(BF16), preferred_element_type=F32)

    def convert():
        if convert_next:
            cg_ref[...] = ng_ref[...].astype(BF16)
            cu_ref[...] = nu_ref[...].astype(BF16)
            cd_ref[...] = nd_ref[...].astype(BF16)

    @pl.when(j == 0)
    def _():
        @pl.when(i == 0)
        def _():
            x_copy(0).start()

        x_copy(i).wait()
        h_ref[...] = _modulate(xs_ref[...], nw_ref[...], sh_ref[...], sc_ref[...]).astype(BF16)
        a = gate_up()
        convert()

        @pl.when(i > 0)
        def _():
            o_copy(i - 1).wait()

        acc_ref[...] = xs_ref[...] + down(a)

        @pl.when(i + 1 < n_tiles)
        def _():
            x_copy(i + 1).start()

    @pl.when(j > 0)
    def _():
        acc_ref[...] += down(gate_up())
        convert()

    @pl.when(j == n_ff - 1)
    def _():
        o_copy(i).start()

        @pl.when(i == n_tiles - 1)
        def _():
            o_copy(i).wait()


def _ffn_call(x, nw, shift, scale, gate, w, tm, next_f32=None):
    s, d = x.shape
    n_tiles = s // tm
    row = pl.BlockSpec((1, d), lambda i, j: (0, 0))
    if not isinstance(w[0], tuple):
        tf = FFN_TILE_F
        w_args = list(w)
        w_specs = [pl.BlockSpec((d, tf), lambda i, j: (0, j)),
                   pl.BlockSpec((d, tf), lambda i, j: (0, j)),
                   pl.BlockSpec((tf, d), lambda i, j: (j, 0))]
    else:
        tf = FFN_TILE_F
        w_args, l0, w0 = w
        w_args = list(w_args)
        w_specs = [pl.BlockSpec((None, None, d, tf), lambda i, j: (l0, w0, 0, j)),
                   pl.BlockSpec((None, None, d, tf), lambda i, j: (l0, w0, 0, j)),
                   pl.BlockSpec((None, None, tf, d), lambda i, j: (l0, w0, j, 0))]
    in_specs = [pl.BlockSpec(memory_space=pl.ANY), row, row, row, row, *w_specs]
    out_specs = [pl.BlockSpec(memory_space=pl.ANY)]
    out_shape = [jax.ShapeDtypeStruct((s, d), F32)]
    args = [x, nw, shift, scale, gate, *w_args]
    if next_f32 is not None:
        (ng, nu, nd), l2, w2 = next_f32
        rb = d // n_tiles
        in_specs += [
            pl.BlockSpec((None, None, rb, tf), lambda i, j: (l2, w2, i, j)),
            pl.BlockSpec((None, None, rb, tf), lambda i, j: (l2, w2, i, j)),
            pl.BlockSpec((None, None, tf, rb), lambda i, j: (l2, w2, j, i)),
        ]
        out_specs += [
            pl.BlockSpec((rb, tf), lambda i, j: (i, j)),
            pl.BlockSpec((rb, tf), lambda i, j: (i, j)),
            pl.BlockSpec((tf, rb), lambda i, j: (j, i)),
        ]
        out_shape += [jax.ShapeDtypeStruct((d, D_FF), BF16), jax.ShapeDtypeStruct((d, D_FF), BF16),
                      jax.ShapeDtypeStruct((D_FF, d), BF16)]
        args += [ng, nu, nd]
    outs = pl.pallas_call(
        functools.partial(_ffn_kernel, convert_next=next_f32 is not None),
        grid=(n_tiles, D_FF // tf),
        in_specs=in_specs,
        out_specs=out_specs,
        out_shape=out_shape,
        scratch_shapes=[pltpu.VMEM((tm, d), F32), pltpu.VMEM((tm, d), F32), pltpu.VMEM((tm, d), BF16),
                        pltpu.SemaphoreType.DMA((2,))],
        compiler_params=_params("arbitrary", "arbitrary"),
        name="ffn",
    )(*args)
    return outs[0], (tuple(outs[1:]) if next_f32 is not None else None)


Q_TILES = ATTN_W // COL_TILE
KV_TILE = Q_TILES
UF_TILE = KV_TILE + 1
UP_TILE = UF_TILE + 1
G_TILE0 = UP_TILE + 1
N_COL_TILES = IN_W // COL_TILE


def _norm_rope(z, gain, cos, sin_signed):
    y = z * lax.rsqrt(jnp.mean(z * z, axis=-1, keepdims=True) + EPS) * gain
    lane = lax.broadcasted_iota(jnp.int32, y.shape, 1)
    first_half = (lane & (D_HEAD // 4)) == 0
    partner = jnp.where(first_half, pltpu.roll(y, D_HEAD - D_HEAD // 4, 1), pltpu.roll(y, D_HEAD // 4, 1))
    return y * cos + partner * sin_signed


STEP_Q, STEP_KV_UF, STEP_UP, STEP_G0 = 0, 1, 2, 3


def _inproj_kernel(x_ref, nw_ref, sh_ref, sc_ref, qg_ref, kg_ref, cos_ref, sin_ref, w_ref, w2_ref,
                   q_ref, kv_ref, uf_ref, up_ref, g_ref, h_ref, z_ref):
    j = pl.program_id(1)

    def proj(w):
        return jnp.dot(h_ref[...], w[...].astype(BF16), preferred_element_type=F32)

    def q_epilogue(z, col0):
        cos, sin = cos_ref[...], sin_ref[...]
        for hh in range(COL_TILE // D_HEAD):
            sl = slice(hh * D_HEAD, (hh + 1) * D_HEAD)
            q = _norm_rope(z[:, sl], qg_ref[...], cos, sin) * Q_SCALE
            q_ref[:, col0 + hh * D_HEAD:col0 + (hh + 1) * D_HEAD] = q.astype(BF16)

    def kv_epilogue(z):
        cos, sin = cos_ref[...], sin_ref[...]
        for hh in range(N_KV_HEADS):
            sl = slice(hh * D_HEAD, (hh + 1) * D_HEAD)
            kv_ref[:, sl] = _norm_rope(z[:, sl], kg_ref[...], cos, sin).astype(BF16)
        kv_ref[:, KV_W:] = z[:, KV_W:].astype(BF16)

    @pl.when(j == STEP_Q)
    def _():
        h_ref[...] = _modulate(x_ref[...], nw_ref[...], sh_ref[...], sc_ref[...]).astype(BF16)
        z_ref[0] = proj(w_ref)
        z_ref[1] = proj(w2_ref)

    @pl.when(j == STEP_KV_UF)
    def _():
        z_ref[2] = proj(w_ref)
        uf_ref[...] = proj(w2_ref).astype(BF16)
        for t in range(Q_TILES):
            q_epilogue(z_ref[t], t * COL_TILE)

    @pl.when(j == STEP_UP)
    def _():
        up_ref[...] = proj(w_ref).astype(BF16)
        kv_epilogue(z_ref[2])

    @pl.when(j >= STEP_G0)
    def _():
        g_ref[:, :COL_TILE] = proj(w_ref).astype(BF16)
        g_ref[:, COL_TILE:] = proj(w2_ref).astype(BF16)


def _inproj_call(x, nw, shift, scale, qg, kg, cos, sin, w_in, l, tm):
    s, d = x.shape
    ct = COL_TILE
    assert (Q_TILES, KV_TILE, UF_TILE, UP_TILE, G_TILE0) == (2, 2, 3, 4, 5) and (N_COL_TILES - G_TILE0) % 2 == 0
    row = pl.BlockSpec((1, d), lambda i, j: (0, 0))
    hrow = pl.BlockSpec((1, D_HEAD), lambda i, j: (0, 0))
    tab = pl.BlockSpec((tm, D_HEAD), lambda i, j: (i, 0))
    n_steps = STEP_G0 + (N_COL_TILES - G_TILE0) // 2
    tile_a = lambda j: 2 * j - (j >= STEP_G0).astype(jnp.int32)
    tile_b = lambda j: jnp.where(j < STEP_UP, 2 * j + 1, jnp.maximum(2 * j, G_TILE0 + 1))
    return pl.pallas_call(
        _inproj_kernel,
        grid=(s // tm, n_steps),
        in_specs=[
            pl.BlockSpec((tm, d), lambda i, j: (i, 0)),
            row, row, row, hrow, hrow, tab, tab,
            pl.BlockSpec((None, d, ct), lambda i, j: (l, 0, tile_a(j))),
            pl.BlockSpec((None, d, ct), lambda i, j: (l, 0, tile_b(j))),
        ],
        out_specs=[
            pl.BlockSpec((tm, Q_TILES * ct), lambda i, j: (i, 0)),
            pl.BlockSpec((tm, ct), lambda i, j: (i, 0)),
            pl.BlockSpec((tm, ct), lambda i, j: (i, 0)),
            pl.BlockSpec((tm, ct), lambda i, j: (i, 0)),
            pl.BlockSpec((tm, 2 * ct), lambda i, j: (i, jnp.maximum(j - STEP_G0, 0))),
        ],
        out_shape=[
            jax.ShapeDtypeStruct((s, ATTN_W), BF16),
            jax.ShapeDtypeStruct((s, 2 * KV_W), BF16),
            jax.ShapeDtypeStruct((s, FOURIER_W), BF16),
            jax.ShapeDtypeStruct((s, POOL_W), BF16),
            jax.ShapeDtypeStruct((s, GATE_W), BF16),
        ],
        scratch_shapes=[pltpu.VMEM((tm, d), BF16), pltpu.VMEM((3, tm, ct), F32)],
        compiler_params=_params("arbitrary", "arbitrary"),
        name="inproj",
    )(x, nw, shift, scale, qg, kg, cos, sin, w_in, w_in)


def _qk(q, k):
    return lax.dot_general(q, k, (((1,), (1,)), ((), ())), preferred_element_type=F32)


def _k_cols(kvh):
    return slice(kvh * D_HEAD, (kvh + 1) * D_HEAD)


def _v_cols(kvh):
    return slice(KV_W + kvh * D_HEAD, KV_W + (kvh + 1) * D_HEAD)


def _sink_column(sink_ref, kvh, rows):
    head = lax.broadcasted_iota(jnp.int32, (GROUP * rows, 1), 0) // rows
    col = jnp.full((GROUP * rows, 1), sink_ref[kvh * GROUP], F32)
    for hh in range(1, GROUP):
        col = jnp.where(head == hh, sink_ref[kvh * GROUP + hh], col)
    return col * LOG2_E


def _stack_heads(q_ref, kvh):
    heads = range(kvh * GROUP, (kvh + 1) * GROUP)
    return jnp.concatenate([q_ref[:, hh * D_HEAD:(hh + 1) * D_HEAD] for hh in heads], axis=0)


def _softmax_pv(scores, values, sink_col):
    def lane_tiles(a):
        return [a[:, t:t + D_HEAD] for t in range(0, a.shape[1], D_HEAD)]

    m_tile = functools.reduce(jnp.maximum, [t for s in scores for t in lane_tiles(s)])
    m = jnp.maximum(sink_col, jnp.max(m_tile, axis=-1, keepdims=True))
    e_tile = None
    acc = None
    for s, v in zip(scores, values):
        e = jnp.exp2(s - m)
        for t in lane_tiles(e):
            e_tile = t if e_tile is None else e_tile + t
        pv = jnp.dot(e.astype(BF16), v, preferred_element_type=F32)
        acc = pv if acc is None else acc + pv
    denom = jnp.exp2(sink_col - m) + jnp.sum(e_tile, axis=-1, keepdims=True)
    return acc / denom


def _unstack_heads(out, o_ref, kvh, rows):
    for g in range(GROUP):
        hh = kvh * GROUP + g
        o_ref[:, hh * D_HEAD:(hh + 1) * D_HEAD] = out[g * rows:(g + 1) * rows].astype(o_ref.dtype)


ATTN_Q_BLOCKS = 4


def _window_attn_kernel(sink_ref, q_ref, *refs):
    kv_refs, kvx_ref, o_ref = refs[:-2], refs[-2], refs[-1]
    i = pl.program_id(0)
    n_steps = pl.num_programs(0)
    shape = (GROUP * BLOCK, BLOCK)
    qi = lax.broadcasted_iota(jnp.int32, shape, 0) % BLOCK
    kj = lax.broadcasted_iota(jnp.int32, shape, 1)
    for b in range(ATTN_Q_BLOCKS):
        kvp_ref, kvc_ref, kvn_ref = kv_refs[b:b + 3]
        keep_prev = (kj >= qi) if b > 0 else (kj >= qi) & (i > 0)
        keep_next = (kj <= qi) if b < ATTN_Q_BLOCKS - 1 else (kj <= qi) & (i < n_steps - 1)
        rows = slice(b * BLOCK, (b + 1) * BLOCK)
        for kvh in range(N_KV_HEADS):
            q = _stack_heads(q_ref.at[rows], kvh)
            ks, vs = _k_cols(kvh), _v_cols(kvh)
            s_prev = jnp.where(keep_prev, _qk(q, kvp_ref[:, ks]), NEG_INF)
            s_cur = _qk(q, kvc_ref[:, ks])
            s_next = jnp.where(keep_next, _qk(q, kvn_ref[:, ks]), NEG_INF)
            s_ctx = _qk(q, kvx_ref[:, ks])
            out = _softmax_pv([s_prev, s_cur, s_next, s_ctx],
                              [kvp_ref[:, vs], kvc_ref[:, vs], kvn_ref[:, vs], kvx_ref[:, vs]],
                              _sink_column(sink_ref, kvh, BLOCK))
            _unstack_heads(out, o_ref.at[rows], kvh, BLOCK)


def _window_attn_call(q, kv, kv_ctx, sink):
    s = q.shape[0]
    c = kv_ctx.shape[0]
    nb = s // BLOCK
    qb = ATTN_Q_BLOCKS
    kblk = lambda off: pl.BlockSpec((BLOCK, 2 * KV_W), lambda i: (jnp.clip(i * qb + off, 0, nb - 1), 0))
    return pl.pallas_call(
        _window_attn_kernel,
        grid=(nb // qb,),
        in_specs=[
            pl.BlockSpec(memory_space=pltpu.SMEM),
            pl.BlockSpec((qb * BLOCK, ATTN_W), lambda i: (i, 0)),
            *[kblk(off) for off in range(-1, qb + 1)],
            pl.BlockSpec((c, 2 * KV_W), lambda i: (0, 0)),
        ],
        out_specs=pl.BlockSpec((qb * BLOCK, ATTN_W), lambda i: (i, 0)),
        out_shape=jax.ShapeDtypeStruct((s, ATTN_W), BF16),
        compiler_params=_params("arbitrary"),
        name="window_attn",
    )(sink, q, *([kv] * (qb + 2)), kv_ctx)


def _ctx_attn_kernel(sink_ref, q_ref, kv_ref, o_ref):
    rows = q_ref.shape[0]
    for kvh in range(N_KV_HEADS):
        q = _stack_heads(q_ref, kvh)
        out = _softmax_pv([_qk(q, kv_ref[:, _k_cols(kvh)])], [kv_ref[:, _v_cols(kvh)]],
                          _sink_column(sink_ref, kvh, rows))
        _unstack_heads(out, o_ref, kvh, rows)


def _ctx_attn_call(q, kv, sink):
    c = q.shape[0]
    return pl.pallas_call(
        _ctx_attn_kernel,
        grid=(1,),
        in_specs=[
            pl.BlockSpec(memory_space=pltpu.SMEM),
            pl.BlockSpec((c, ATTN_W), lambda i: (0, 0)),
            pl.BlockSpec((c, 2 * KV_W), lambda i: (0, 0)),
        ],
        out_specs=pl.BlockSpec((c, ATTN_W), lambda i: (0, 0)),
        out_shape=jax.ShapeDtypeStruct((c, ATTN_W), BF16),
        compiler_params=_params("arbitrary"),
        name="ctx_attn",
    )(sink, q, kv)


def _dft_tables(s_len):
    n1 = DFT_ROWS if s_len > 1024 else 1
    n2 = s_len // n1
    k1 = np.arange(n1)
    ang1 = 2.0 * np.pi * ((k1[:, None] * k1[None, :]) % n1) / n1
    w1 = np.concatenate([np.cos(ang1), -np.sin(ang1)], axis=0)
    w1 = np.kron(w1, np.eye(DFT_S2_GROUP))
    k2 = np.arange(n2)
    phase = (k2[None, None, :] * (k1[:, None, None] + n1 * k2[None, :, None])) % s_len
    ang2 = 2.0 * np.pi * phase / s_len
    gr, gi = np.cos(ang2), -np.sin(ang2)
    g = np.concatenate([np.concatenate([gr, -gi], axis=2),
                        np.concatenate([gi, gr], axis=2)], axis=1)
    if n1 == 1:
        g = g[:, :, :n2]
    c = np.arange(FOURIER_GROUP_W)
    angc = 2.0 * np.pi * ((c[:, None] * c[None, :]) % FOURIER_GROUP_W) / FOURIER_GROUP_W
    norm = 1.0 / np.sqrt(float(s_len) * FOURIER_GROUP_W)
    return (w1.astype(np.float32), g.astype(np.float32),
            (np.cos(angc) * norm).astype(np.float32), (np.sin(angc) * norm).astype(np.float32))


def _dft1_kernel(w_ref, u_ref, o_ref):
    n1, grp, cw = u_ref.shape
    y = jnp.dot(w_ref[...], u_ref[...].reshape(n1 * grp, cw), preferred_element_type=F32)
    o_ref[...] = y.astype(o_ref.dtype).reshape(o_ref.shape)


def _dft1_call(w1k, uf):
    s, cw = uf.shape
    n1 = DFT_ROWS
    n2 = s // n1
    grp = DFT_S2_GROUP
    return pl.pallas_call(
        _dft1_kernel,
        grid=(n2 // grp,),
        in_specs=[pl.BlockSpec((2 * n1 * grp, n1 * grp), lambda j: (0, 0)),
                  pl.BlockSpec((n1, grp, cw), lambda j: (0, j, 0))],
        out_specs=pl.BlockSpec((2 * n1, grp, cw), lambda j: (0, j, 0)),
        out_shape=jax.ShapeDtypeStruct((2 * n1, n2, cw), BF16),
        compiler_params=_params("arbitrary"),
        name="dft_rows",
    )(w1k, uf.reshape(n1, n2, cw))


def _dft2_kernel(g_ref, *refs):
    x_refs, (cc_ref, sc_ref, o_ref) = refs[:-3], refs[-3:]
    kb = g_ref.shape[0]
    n_in = x_refs[0].shape[0] // kb
    n = g_ref.shape[1] // 2
    cw = FOURIER_W
    for b in range(kb):
        rows = slice(b * n_in, (b + 1) * n_in)
        x = (x_refs[0][rows] if len(x_refs) == 1
             else jnp.concatenate([r[rows] for r in x_refs], axis=0))
        y = jnp.dot(g_ref[b], x, preferred_element_type=F32)
        yr, yi = y[:n].astype(BF16), y[n:].astype(BF16)
        for gi in range(N_FOURIER_GROUPS):
            sl = slice(gi * FOURIER_GROUP_W, (gi + 1) * FOURIER_GROUP_W)
            o = (jnp.dot(yr[:, sl], cc_ref[...], preferred_element_type=F32)
                 + jnp.dot(yi[:, sl], sc_ref[...], preferred_element_type=F32))
            o_ref[:, b * cw + gi * FOURIER_GROUP_W:b * cw + (gi + 1) * FOURIER_GROUP_W] = o.astype(o_ref.dtype)


def _dft2_call(g, x1, cc, sc, s_len):
    n1 = DFT_ROWS
    n2 = s_len // n1
    cw = FOURIER_W
    gw = FOURIER_GROUP_W
    kb = 8
    out = pl.pallas_call(
        _dft2_kernel,
        grid=(n1 // kb,),
        in_specs=[pl.BlockSpec((kb, 2 * n2, 2 * n2), lambda k: (k, 0, 0)),
                  pl.BlockSpec((kb * n2, cw), lambda k: (k, 0)),
                  pl.BlockSpec((kb * n2, cw), lambda k: (n1 // kb + k, 0)),
                  pl.BlockSpec((gw, gw), lambda k: (0, 0)),
                  pl.BlockSpec((gw, gw), lambda k: (0, 0))],
        out_specs=pl.BlockSpec((n2, kb * cw), lambda k: (0, k)),
        out_shape=jax.ShapeDtypeStruct((n2, n1 * cw), BF16),
        compiler_params=_params("arbitrary"),
        name="dft_cols",
    )(g, x1.reshape(2 * n1 * n2, cw), x1.reshape(2 * n1 * n2, cw), cc, sc)
    return out.reshape(s_len, cw)


def _dft_small_call(g, uf, cc, sc):
    c, cw = uf.shape
    gw = FOURIER_GROUP_W
    return pl.pallas_call(
        _dft2_kernel,
        grid=(1,),
        in_specs=[pl.BlockSpec((1, 2 * c, c), lambda k: (0, 0, 0)),
                  pl.BlockSpec((c, cw), lambda k: (0, 0)),
                  pl.BlockSpec((gw, gw), lambda k: (0, 0)),
                  pl.BlockSpec((gw, gw), lambda k: (0, 0))],
        out_specs=pl.BlockSpec((c, cw), lambda k: (0, 0)),
        out_shape=jax.ShapeDtypeStruct((c, cw), BF16),
        compiler_params=_params("arbitrary"),
        name="dft_ctx",
    )(g, uf, cc, sc)


def _merge_kernel(a_ref, f_ref, up_ref, upp_ref, upn_ref, ga_ref, gf_ref, gp_ref, x_ref,
                  gate_ref, ps_ref, wo_ref, wf_ref, wp_ref, wout_ref, o_ref, ext_ref, *, s_len):
    i = pl.program_id(0)
    tm = up_ref.shape[0]
    h = POOL_HALO
    ext_ref[0:h] = jnp.where(i > 0, upp_ref[...].astype(F32), 0.0)
    ext_ref[h:h + tm] = up_ref[...].astype(F32)
    ext_ref[h + tm:] = jnp.where(i < pl.num_programs(0) - 1, upn_ref[...].astype(F32), 0.0)

    t = i * tm + lax.broadcasted_iota(jnp.int32, (tm, 1), 0)
    pooled = []
    for gi, w in enumerate(POOL_WINDOWS):
        sl = slice(gi * POOL_GROUP_W, (gi + 1) * POOL_GROUP_W)
        total = ext_ref[h - w // 2:h - w // 2 + tm, sl]
        for off in range(-(w // 2) + 1, w - w // 2):
            total = total + ext_ref[h + off:h + off + tm, sl]
        count = jnp.minimum(t + (w - w // 2), s_len) - jnp.maximum(t - w // 2, 0)
        u = ext_ref[h:h + tm, sl]
        pooled.append((total / count.astype(F32) - u).astype(BF16))
    yp = jnp.concatenate(
        [jnp.dot(p, wp_ref[gi], preferred_element_type=F32) for gi, p in enumerate(pooled)], axis=-1)

    ya = jnp.dot(a_ref[...], wo_ref[...], preferred_element_type=F32)
    yf = jnp.dot(f_ref[...], wf_ref[...], preferred_element_type=F32)
    y = (jax.nn.sigmoid(ga_ref[...].astype(F32)) * ya
         + jax.nn.sigmoid(gf_ref[...].astype(F32)) * yf
         + jax.nn.sigmoid(gp_ref[...].astype(F32)) * (yp * ps_ref[...]))
    mix = jnp.dot(y.astype(BF16), wout_ref[...], preferred_element_type=F32)
    o_ref[...] = x_ref[...] + gate_ref[...] * mix


def _merge_call(a, f, up, g, x, gate, pool_scale, w_attn_o, w_fourier, w_pool, w_out, l, tm):
    s, d = x.shape
    h = POOL_HALO
    n_halo = s // h
    row = pl.BlockSpec((1, d), lambda i: (0, 0))
    once = dict(pipeline_mode=pl.Buffered(1))
    return pl.pallas_call(
        functools.partial(_merge_kernel, s_len=s),
        grid=(s // tm,),
        in_specs=[
            pl.BlockSpec((tm, ATTN_W), lambda i: (i, 0)),
            pl.BlockSpec((tm, FOURIER_W), lambda i: (i, 0)),
            pl.BlockSpec((tm, POOL_W), lambda i: (i, 0)),
            pl.BlockSpec((h, POOL_W), lambda i: (jnp.maximum(i * (tm // h) - 1, 0), 0)),
            pl.BlockSpec((h, POOL_W), lambda i: (jnp.minimum((i + 1) * (tm // h), n_halo - 1), 0)),
            pl.BlockSpec((tm, d), lambda i: (i, 0)),
            pl.BlockSpec((tm, d), lambda i: (i, 1)),
            pl.BlockSpec((tm, d), lambda i: (i, 2)),
            pl.BlockSpec((tm, d), lambda i: (i, 0)),
            row, row,
            pl.BlockSpec((None, ATTN_W, d), lambda i: (l, 0, 0), **once),
            pl.BlockSpec((None, FOURIER_W, d), lambda i: (l, 0, 0), **once),
            pl.BlockSpec((None, len(POOL_WINDOWS), POOL_GROUP_W, POOL_OUT_GROUP_W),
                         lambda i: (l, 0, 0, 0), **once),
            pl.BlockSpec((None, d, d), lambda i: (l, 0, 0), **once),
        ],
        out_specs=pl.BlockSpec((tm, d), lambda i: (i, 0)),
        out_shape=jax.ShapeDtypeStruct((s, d), F32),
        scratch_shapes=[pltpu.VMEM((tm + 2 * h, POOL_W), F32)],
        compiler_params=_params("arbitrary"),
        name="merge",
    )(a, f, up, up, up, g, g, g, x, gate, pool_scale, w_attn_o, w_fourier, w_pool, w_out)


def _rope_tables(s_len):
    rows_n = s_len // GRID_W
    n_freq = D_HEAD // 4
    freqs = ROPE_BASE ** (-jnp.arange(n_freq, dtype=F32) / n_freq)
    ar = jnp.arange(rows_n).astype(F32)[:, None] * freqs
    ac = jnp.arange(GRID_W).astype(F32)[:, None] * freqs
    by_row = lambda t: jnp.repeat(t, GRID_W, axis=0)
    by_col = lambda t: jnp.tile(t, (rows_n, 1))
    cr, sr, cc, sc = by_row(jnp.cos(ar)), by_row(jnp.sin(ar)), by_col(jnp.cos(ac)), by_col(jnp.sin(ac))
    cos = jnp.concatenate([cr, cr, cc, cc], axis=-1)
    sin = jnp.concatenate([-sr, sr, -sc, sc], axis=-1)
    return cos, sin


def kernel(x, c, ctx, c_ctx, w_ada, b_ada, norm_w, ffn_w_gate, ffn_w_up, ffn_w_down,
           w_in, q_gain, k_gain, sink, w_attn_o, w_fourier, w_pool, pool_scale, w_out):
    b, s_len, d = x.shape
    c_len = ctx.shape[1]
    assert b == 1 and d == D_MODEL and s_len % (DFT_ROWS * BLOCK) == 0
    x, ctx = x[0], ctx[0]

    cond = jnp.zeros((8, d), F32).at[0].set(c[0]).at[1].set(c_ctx)
    mod = _ada_call(cond, w_ada, b_ada).reshape(DEPTH, 8, N_MOD, 1, d)

    ffn_f32 = (ffn_w_gate, ffn_w_up, ffn_w_down)
    ffn_w = (ffn_f32, 0, 0)
    w_ao, w_fo = w_attn_o.astype(BF16), w_fourier.astype(BF16)
    w_po, w_ou = w_pool.astype(BF16), w_out.astype(BF16)

    cos_x, sin_x = _rope_tables(s_len)
    cos_c, sin_c = jnp.ones((c_len, D_HEAD), F32), jnp.zeros((c_len, D_HEAD), F32)
    w1, g_x, cc, sc = (jnp.asarray(t).astype(BF16) for t in _dft_tables(s_len))
    _, g_c, cc_c, sc_c = (jnp.asarray(t).astype(BF16) for t in _dft_tables(c_len))

    tm_x, tm_m = 1024, 256
    for l in range(DEPTH):
        last = l == DEPTH - 1
        mx, mc = mod[l, 0], mod[l, 1]
        nw = norm_w[l][:, None, :]
        qg, kg = q_gain[l][None, :], k_gain[l][None, :]
        ps = pool_scale[l][None, :]

        x, ffn_w_post = _ffn_call(x, nw[0], mx[0], mx[1], mx[2], ffn_w, tm_x, (ffn_f32, l, 1))
        ctx, _ = _ffn_call(ctx, nw[0], mc[0], mc[1], mc[2], ffn_w, c_len)

        qx, kvx, ufx, upx, gx = _inproj_call(x, nw[1], mx[3], mx[4], qg, kg, cos_x, sin_x, w_in, l, tm_x)
        qc, kvc, ufc, upc, gc = _inproj_call(ctx, nw[1], mc[3], mc[4], qg, kg, cos_c, sin_c, w_in, l, c_len)

        ax = _window_attn_call(qx, kvx, kvc, sink[l])
        fx = _dft2_call(g_x, _dft1_call(w1, ufx), cc, sc, s_len)
        x = _merge_call(ax, fx, upx, gx, x, mx[5], ps, w_ao, w_fo, w_po, w_ou, l, tm_m)
        if not last:
            ac = _ctx_attn_call(qc, kvc, sink[l])
            fc = _dft_small_call(g_c, ufc, cc_c, sc_c)
            ctx = _merge_call(ac, fc, upc, gc, ctx, mc[5], ps, w_ao, w_fo, w_po, w_ou, l, c_len)

        x, ffn_w_next = _ffn_call(x, nw[2], mx[6], mx[7], mx[8], ffn_w_post, tm_x,
                                  None if last else (ffn_f32, l + 1, 0))
        if not last:
            ctx, _ = _ffn_call(ctx, nw[2], mc[6], mc[7], mc[8], ffn_w_post, c_len)
        ffn_w = ffn_w_next
    return x[None]
```

```python
import functools

import numpy as np
import jax
import jax.numpy as jnp
from jax import lax
from jax.experimental import pallas as pl
from jax.experimental.pallas import tpu as pltpu

D_MODEL = 2048
DEPTH = 4
GRID_W = 64
N_HEADS = 8
N_KV_HEADS = 2
GROUP = N_HEADS // N_KV_HEADS
D_HEAD = 128
ATTN_W = N_HEADS * D_HEAD
KV_W = N_KV_HEADS * D_HEAD
WINDOW = 128
BLOCK = 128
FOURIER_W = D_MODEL // 4
N_FOURIER_GROUPS = 4
FOURIER_GROUP_W = FOURIER_W // N_FOURIER_GROUPS
POOL_WINDOWS = (2, 4, 8, 16)
POOL_W = D_MODEL // 4
POOL_GROUP_W = POOL_W // len(POOL_WINDOWS)
POOL_OUT_GROUP_W = D_MODEL // len(POOL_WINDOWS)
N_BRANCHES = 3
GATE_W = N_BRANCHES * D_MODEL
IN_W = ATTN_W + 2 * KV_W + FOURIER_W + POOL_W + GATE_W
D_FF = 5632
N_MOD = 9
ROPE_BASE = 10000.0
EPS = 1e-6
NEG_INF = -1e30

BF16 = jnp.bfloat16
F32 = jnp.float32

VMEM_LIMIT_BYTES = 60 * 1024 * 1024
ROW_TILE = 1024
MERGE_ROW_TILE = 256
ADA_TILE_N = 1024
POOL_HALO = 16
COL_TILE = 512
FFN_TILE_F = 512
LOG2_E = 1.4426950408889634
Q_SCALE = LOG2_E * D_HEAD ** -0.5
DFT_ROWS = GRID_W
DFT_S2_GROUP = 16


def _params(*semantics):
    return pltpu.CompilerParams(dimension_semantics=semantics, vmem_limit_bytes=VMEM_LIMIT_BYTES)


def _modulate(x, gain, shift, scale):
    y = x * lax.rsqrt(jnp.mean(x * x, axis=-1, keepdims=True) + EPS)
    return (y * gain) * (1 + scale) + shift


def _ada_kernel(cond_ref, w_ref, b_ref, o_ref):
    h = jax.nn.silu(cond_ref[...]).astype(BF16)
    o_ref[0] = jnp.dot(h, w_ref[0].astype(BF16), preferred_element_type=F32) + b_ref[0]


def _ada_call(cond, w_ada, b_ada):
    depth, d, n = w_ada.shape
    tn = ADA_TILE_N
    return pl.pallas_call(
        _ada_kernel,
        grid=(depth, n // tn),
        in_specs=[
            pl.BlockSpec((8, d), lambda l, j: (0, 0)),
            pl.BlockSpec((1, d, tn), lambda l, j: (l, 0, j)),
            pl.BlockSpec((1, 1, tn), lambda l, j: (l, 0, j)),
        ],
        out_specs=pl.BlockSpec((1, 8, tn), lambda l, j: (l, 0, j)),
        out_shape=jax.ShapeDtypeStruct((depth, 8, n), F32),
        compiler_params=_params("arbitrary", "arbitrary"),
        name="adaln",
    )(cond, w_ada, b_ada.reshape(depth, 1, n))


def _ffn_kernel(*refs, convert_next):
    if convert_next:
        (x_hbm, nw_ref, sh_ref, sc_ref, gt_ref, wg_ref, wu_ref, wd_ref, ng_ref, nu_ref, nd_ref,
         o_hbm, cg_ref, cu_ref, cd_ref, xs_ref, acc_ref, h_ref, sem) = refs
    else:
        (x_hbm, nw_ref, sh_ref, sc_ref, gt_ref, wg_ref, wu_ref, wd_ref,
         o_hbm, xs_ref, acc_ref, h_ref, sem) = refs
    i, j = pl.program_id(0), pl.program_id(1)
    n_tiles, n_ff = pl.num_programs(0), pl.num_programs(1)
    tm = xs_ref.shape[0]

    def x_copy(tile):
        return pltpu.make_async_copy(x_hbm.at[pl.ds(tile * tm, tm)], xs_ref, sem.at[0])

    def o_copy(tile):
        return pltpu.make_async_copy(acc_ref, o_hbm.at[pl.ds(tile * tm, tm)], sem.at[1])

    def gate_up():
        h = h_ref[...]
        g = jnp.dot(h, wg_ref[...].astype(BF16), preferred_element_type=F32)
        u = jnp.dot(h, wu_ref[...].astype(BF16), preferred_element_type=F32)
        return (jax.nn.silu(g) * u).astype(BF16)

    def down(a):
        return (0.5 * gt_ref[...]) * jnp.dot(a, wd_ref[...].astype(BF16), preferred_element_type=F32)

    def convert():
        if convert_next:
            cg_ref[...] = ng_ref[...].astype(BF16)
            cu_ref[...] = nu_ref[...].astype(BF16)
            cd_ref[...] = nd_ref[...].astype(BF16)

    @pl.when(j == 0)
    def _():
        @pl.when(i == 0)
        def _():
            x_copy(0).start()

        x_copy(i).wait()
        h_ref[...] = _modulate(xs_ref[...], nw_ref[...], sh_ref[...], sc_ref[...]).astype(BF16)
        a = gate_up()
        convert()

        @pl.when(i > 0)
        def _():
            o_copy(i - 1).wait()

        acc_ref[...] = xs_ref[...] + down(a)

        @pl.when(i + 1 < n_tiles)
        def _():
            x_copy(i + 1).start()

    @pl.when(j > 0)
    def _():
        acc_ref[...] += down(gate_up())
        convert()

    @pl.when(j == n_ff - 1)
    def _():
        o_copy(i).start()

        @pl.when(i == n_tiles - 1)
        def _():
            o_copy(i).wait()


def _ffn_call(x, nw, shift, scale, gate, w, tm, next_f32=None):
    s, d = x.shape
    n_tiles = s // tm
    row = pl.BlockSpec((1, d), lambda i, j: (0, 0))
    if not isinstance(w[0], tuple):
        tf = FFN_TILE_F
        w_args = list(w)
        w_specs = [pl.BlockSpec((d, tf), lambda i, j: (0, j)),
                   pl.BlockSpec((d, tf), lambda i, j: (0, j)),
                   pl.BlockSpec((tf, d), lambda i, j: (j, 0))]
    else:
        tf = FFN_TILE_F
        w_args, l0, w0 = w
        w_args = list(w_args)
        w_specs = [pl.BlockSpec((None, None, d, tf), lambda i, j: (l0, w0, 0, j)),
                   pl.BlockSpec((None, None, d, tf), lambda i, j: (l0, w0, 0, j)),
                   pl.BlockSpec((None, None, tf, d), lambda i, j: (l0, w0, j, 0))]
    in_specs = [pl.BlockSpec(memory_space=pl.ANY), row, row, row, row, *w_specs]
    out_specs = [pl.BlockSpec(memory_space=pl.ANY)]
    out_shape = [jax.ShapeDtypeStruct((s, d), F32)]
    args = [x, nw, shift, scale, gate, *w_args]
    if next_f32 is not None:
        (ng, nu, nd), l2, w2 = next_f32
        rb = d // n_tiles
        in_specs += [
            pl.BlockSpec((None, None, rb, tf), lambda i, j: (l2, w2, i, j)),
            pl.BlockSpec((None, None, rb, tf), lambda i, j: (l2, w2, i, j)),
            pl.BlockSpec((None, None, tf, rb), lambda i, j: (l2, w2, j, i)),
        ]
        out_specs += [
            pl.BlockSpec((rb, tf), lambda i, j: (i, j)),
            pl.BlockSpec((rb, tf), lambda i, j: (i, j)),
            pl.BlockSpec((tf, rb), lambda i, j: (j, i)),
        ]
        out_shape += [jax.ShapeDtypeStruct((d, D_FF), BF16), jax.ShapeDtypeStruct((d, D_FF), BF16),
                      jax.ShapeDtypeStruct((D_FF, d), BF16)]
        args += [ng, nu, nd]
    outs = pl.pallas_call(
        functools.partial(_ffn_kernel, convert_next=next_f32 is not None),
        grid=(n_tiles, D_FF // tf),
        in_specs=in_specs,
        out_specs=out_specs,
        out_shape=out_shape,
        scratch_shapes=[pltpu.VMEM((tm, d), F32), pltpu.VMEM((tm, d), F32), pltpu.VMEM((tm, d), BF16),
                        pltpu.SemaphoreType.DMA((2,))],
        compiler_params=_params("arbitrary", "arbitrary"),
        name="ffn",
    )(*args)
    return outs[0], (tuple(outs[1:]) if next_f32 is not None else None)


Q_TILES = ATTN_W // COL_TILE
KV_TILE = Q_TILES
UF_TILE = KV_TILE + 1
UP_TILE = UF_TILE + 1
G_TILE0 = UP_TILE + 1
N_COL_TILES = IN_W // COL_TILE


def _norm_rope(z, gain, cos, sin_signed):
    y = z * lax.rsqrt(jnp.mean(z * z, axis=-1, keepdims=True) + EPS) * gain
    lane = lax.broadcasted_iota(jnp.int32, y.shape, 1)
    first_half = (lane & (D_HEAD // 4)) == 0
    partner = jnp.where(first_half, pltpu.roll(y, D_HEAD - D_HEAD // 4, 1), pltpu.roll(y, D_HEAD // 4, 1))
    return y * cos + partner * sin_signed


STEP_Q, STEP_KV_UF, STEP_UP, STEP_G0 = 0, 1, 2, 3


def _inproj_kernel(x_ref, nw_ref, sh_ref, sc_ref, qg_ref, kg_ref, cos_ref, sin_ref, w_ref, w2_ref,
                   q_ref, kv_ref, uf_ref, up_ref, g_ref, h_ref, z_ref):
    j = pl.program_id(1)

    def proj(w):
        return jnp.dot(h_ref[...], w[...].astype(BF16), preferred_element_type=F32)

    def q_epilogue(z, col0):
        cos, sin = cos_ref[...], sin_ref[...]
        for hh in range(COL_TILE // D_HEAD):
            sl = slice(hh * D_HEAD, (hh + 1) * D_HEAD)
            q = _norm_rope(z[:, sl], qg_ref[...], cos, sin) * Q_SCALE
            q_ref[:, col0 + hh * D_HEAD:col0 + (hh + 1) * D_HEAD] = q.astype(BF16)

    def kv_epilogue(z):
        cos, sin = cos_ref[...], sin_ref[...]
        for hh in range(N_KV_HEADS):
            sl = slice(hh * D_HEAD, (hh + 1) * D_HEAD)
            kv_ref[:, sl] = _norm_rope(z[:, sl], kg_ref[...], cos, sin).astype(BF16)
        kv_ref[:, KV_W:] = z[:, KV_W:].astype(BF16)

    @pl.when(j == STEP_Q)
    def _():
        h_ref[...] = _modulate(x_ref[...], nw_ref[...], sh_ref[...], sc_ref[...]).astype(BF16)
        z_ref[0] = proj(w_ref)
        z_ref[1] = proj(w2_ref)

    @pl.when(j == STEP_KV_UF)
    def _():
        z_ref[2] = proj(w_ref)
        uf_ref[...] = proj(w2_ref).astype(BF16)
        for t in range(Q_TILES):
            q_epilogue(z_ref[t], t * COL_TILE)

    @pl.when(j == STEP_UP)
    def _():
        up_ref[...] = proj(w_ref).astype(BF16)
        kv_epilogue(z_ref[2])

    @pl.when(j >= STEP_G0)
    def _():
        g_ref[:, :COL_TILE] = proj(w_ref).astype(BF16)
        g_ref[:, COL_TILE:] = proj(w2_ref).astype(BF16)


def _inproj_call(x, nw, shift, scale, qg, kg, cos, sin, w_in, l, tm):
    s, d = x.shape
    ct = COL_TILE
    assert (Q_TILES, KV_TILE, UF_TILE, UP_TILE, G_TILE0) == (2, 2, 3, 4, 5) and (N_COL_TILES - G_TILE0) % 2 == 0
    row = pl.BlockSpec((1, d), lambda i, j: (0, 0))
    hrow = pl.BlockSpec((1, D_HEAD), lambda i, j: (0, 0))
    tab = pl.BlockSpec((tm, D_HEAD), lambda i, j: (i, 0))
    n_steps = STEP_G0 + (N_COL_TILES - G_TILE0) // 2
    tile_a = lambda j: 2 * j - (j >= STEP_G0).astype(jnp.int32)
    tile_b = lambda j: jnp.where(j < STEP_UP, 2 * j + 1, jnp.maximum(2 * j, G_TILE0 + 1))
    return pl.pallas_call(
        _inproj_kernel,
        grid=(s // tm, n_steps),
        in_specs=[
            pl.BlockSpec((tm, d), lambda i, j: (i, 0)),
            row, row, row, hrow, hrow, tab, tab,
            pl.BlockSpec((None, d, ct), lambda i, j: (l, 0, tile_a(j))),
            pl.BlockSpec((None, d, ct), lambda i, j: (l, 0, tile_b(j))),
        ],
        out_specs=[
            pl.BlockSpec((tm, Q_TILES * ct), lambda i, j: (i, 0)),
            pl.BlockSpec((tm, ct), lambda i, j: (i, 0)),
            pl.BlockSpec((tm, ct), lambda i, j: (i, 0)),
            pl.BlockSpec((tm, ct), lambda i, j: (i, 0)),
            pl.BlockSpec((tm, 2 * ct), lambda i, j: (i, jnp.maximum(j - STEP_G0, 0))),
        ],
        out_shape=[
            jax.ShapeDtypeStruct((s, ATTN_W), BF16),
            jax.ShapeDtypeStruct((s, 2 * KV_W), BF16),
            jax.ShapeDtypeStruct((s, FOURIER_W), BF16),
            jax.ShapeDtypeStruct((s, POOL_W), BF16),
            jax.ShapeDtypeStruct((s, GATE_W), BF16),
        ],
        scratch_shapes=[pltpu.VMEM((tm, d), BF16), pltpu.VMEM((3, tm, ct), F32)],
        compiler_params=_params("arbitrary", "arbitrary"),
        name="inproj",
    )(x, nw, shift, scale, qg, kg, cos, sin, w_in, w_in)


def _qk(q, k):
    return lax.dot_general(q, k, (((1,), (1,)), ((), ())), preferred_element_type=F32)


def _k_cols(kvh):
    return slice(kvh * D_HEAD, (kvh + 1) * D_HEAD)


def _v_cols(kvh):
    return slice(KV_W + kvh * D_HEAD, KV_W + (kvh + 1) * D_HEAD)


def _sink_column(sink_ref, kvh, rows):
    head = lax.broadcasted_iota(jnp.int32, (GROUP * rows, 1), 0) // rows
    col = jnp.full((GROUP * rows, 1), sink_ref[kvh * GROUP], F32)
    for hh in range(1, GROUP):
        col = jnp.where(head == hh, sink_ref[kvh * GROUP + hh], col)
    return col * LOG2_E


def _stack_heads(q_ref, kvh):
    heads = range(kvh * GROUP, (kvh + 1) * GROUP)
    return jnp.concatenate([q_ref[:, hh * D_HEAD:(hh + 1) * D_HEAD] for hh in heads], axis=0)


def _softmax_pv(scores, values, sink_col):
    def lane_tiles(a):
        return [a[:, t:t + D_HEAD] for t in range(0, a.shape[1], D_HEAD)]

    m_tile = functools.reduce(jnp.maximum, [t for s in scores for t in lane_tiles(s)])
    m = jnp.maximum(sink_col, jnp.max(m_tile, axis=-1, keepdims=True))
    e_tile = None
    acc = None
    for s, v in zip(scores, values):
        e = jnp.exp2(s - m)
        for t in lane_tiles(e):
            e_tile = t if e_tile is None else e_tile + t
        pv = jnp.dot(e.astype(BF16), v, preferred_element_type=F32)
        acc = pv if acc is None else acc + pv
    denom = jnp.exp2(sink_col - m) + jnp.sum(e_tile, axis=-1, keepdims=True)
    return acc / denom


def _unstack_heads(out, o_ref, kvh, rows):
    for g in range(GROUP):
        hh = kvh * GROUP + g
        o_ref[:, hh * D_HEAD:(hh + 1) * D_HEAD] = out[g * rows:(g + 1) * rows].astype(o_ref.dtype)


ATTN_Q_BLOCKS = 4


def _window_attn_kernel(sink_ref, q_ref, *refs):
    kv_refs, kvx_ref, o_ref = refs[:-2], refs[-2], refs[-1]
    i = pl.program_id(0)
    n_steps = pl.num_programs(0)
    shape = (GROUP * BLOCK, BLOCK)
    qi = lax.broadcasted_iota(jnp.int32, shape, 0) % BLOCK
    kj = lax.broadcasted_iota(jnp.int32, shape, 1)
    for b in range(ATTN_Q_BLOCKS):
        kvp_ref, kvc_ref, kvn_ref = kv_refs[b:b + 3]
        keep_prev = (kj >= qi) if b > 0 else (kj >= qi) & (i > 0)
        keep_next = (kj <= qi) if b < ATTN_Q_BLOCKS - 1 else (kj <= qi) & (i < n_steps - 1)
        rows = slice(b * BLOCK, (b + 1) * BLOCK)
        for kvh in range(N_KV_HEADS):
            q = _stack_heads(q_ref.at[rows], kvh)
            ks, vs = _k_cols(kvh), _v_cols(kvh)
            s_prev = jnp.where(keep_prev, _qk(q, kvp_ref[:, ks]), NEG_INF)
            s_cur = _qk(q, kvc_ref[:, ks])
            s_next = jnp.where(keep_next, _qk(q, kvn_ref[:, ks]), NEG_INF)
            s_ctx = _qk(q, kvx_ref[:, ks])
            out = _softmax_pv([s_prev, s_cur, s_next, s_ctx],
                              [kvp_ref[:, vs], kvc_ref[:, vs], kvn_ref[:, vs], kvx_ref[:, vs]],
                              _sink_column(sink_ref, kvh, BLOCK))
            _unstack_heads(out, o_ref.at[rows], kvh, BLOCK)


def _window_attn_call(q, kv, kv_ctx, sink):
    s = q.shape[0]
    c = kv_ctx.shape[0]
    nb = s // BLOCK
    qb = ATTN_Q_BLOCKS
    kblk = lambda off: pl.BlockSpec((BLOCK, 2 * KV_W), lambda i: (jnp.clip(i * qb + off, 0, nb - 1), 0))
    return pl.pallas_call(
        _window_attn_kernel,
        grid=(nb // qb,),
        in_specs=[
            pl.BlockSpec(memory_space=pltpu.SMEM),
            pl.BlockSpec((qb * BLOCK, ATTN_W), lambda i: (i, 0)),
            *[kblk(off) for off in range(-1, qb + 1)],
            pl.BlockSpec((c, 2 * KV_W), lambda i: (0, 0)),
        ],
        out_specs=pl.BlockSpec((qb * BLOCK, ATTN_W), lambda i: (i, 0)),
        out_shape=jax.ShapeDtypeStruct((s, ATTN_W), BF16),
        compiler_params=_params("arbitrary"),
        name="window_attn",
    )(sink, q, *([kv] * (qb + 2)), kv_ctx)


def _ctx_attn_kernel(sink_ref, q_ref, kv_ref, o_ref):
    rows = q_ref.shape[0]
    for kvh in range(N_KV_HEADS):
        q = _stack_heads(q_ref, kvh)
        out = _softmax_pv([_qk(q, kv_ref[:, _k_cols(kvh)])], [kv_ref[:, _v_cols(kvh)]],
                          _sink_column(sink_ref, kvh, rows))
        _unstack_heads(out, o_ref, kvh, rows)


def _ctx_attn_call(q, kv, sink):
    c = q.shape[0]
    return pl.pallas_call(
        _ctx_attn_kernel,
        grid=(1,),
        in_specs=[
            pl.BlockSpec(memory_space=pltpu.SMEM),
            pl.BlockSpec((c, ATTN_W), lambda i: (0, 0)),
            pl.BlockSpec((c, 2 * KV_W), lambda i: (0, 0)),
        ],
        out_specs=pl.BlockSpec((c, ATTN_W), lambda i: (0, 0)),
        out_shape=jax.ShapeDtypeStruct((c, ATTN_W), BF16),
        compiler_params=_params("arbitrary"),
        name="ctx_attn",
    )(sink, q, kv)


def _dft_tables(s_len):
    n1 = DFT_ROWS if s_len > 1024 else 1
    n2 = s_len // n1
    k1 = np.arange(n1)
    ang1 = 2.0 * np.pi * ((k1[:, None] * k1[None, :]) % n1) / n1
    w1 = np.concatenate([np.cos(ang1), -np.sin(ang1)], axis=0)
    w1 = np.kron(w1, np.eye(DFT_S2_GROUP))
    k2 = np.arange(n2)
    phase = (k2[None, None, :] * (k1[:, None, None] + n1 * k2[None, :, None])) % s_len
    ang2 = 2.0 * np.pi * phase / s_len
    gr, gi = np.cos(ang2), -np.sin(ang2)
    g = np.concatenate([np.concatenate([gr, -gi], axis=2),
                        np.concatenate([gi, gr], axis=2)], axis=1)
    if n1 == 1:
        g = g[:, :, :n2]
    c = np.arange(FOURIER_GROUP_W)
    angc = 2.0 * np.pi * ((c[:, None] * c[None, :]) % FOURIER_GROUP_W) / FOURIER_GROUP_W
    norm = 1.0 / np.sqrt(float(s_len) * FOURIER_GROUP_W)
    return (w1.astype(np.float32), g.astype(np.float32),
            (np.cos(angc) * norm).astype(np.float32), (np.sin(angc) * norm).astype(np.float32))


DFT_K1_GROUP = 8


def _dft_cols(g, x, cc_ref, sc_ref, o_ref, col0):
    y = jnp.dot(g, x, preferred_element_type=F32)
    n = g.shape[0] // 2
    yr, yi = y[:n].astype(BF16), y[n:].astype(BF16)
    for gi in range(N_FOURIER_GROUPS):
        sl = slice(gi * FOURIER_GROUP_W, (gi + 1) * FOURIER_GROUP_W)
        o = (jnp.dot(yr[:, sl], cc_ref[...], preferred_element_type=F32)
             + jnp.dot(yi[:, sl], sc_ref[...], preferred_element_type=F32))
        o_ref[:, col0 + gi * FOURIER_GROUP_W:col0 + (gi + 1) * FOURIER_GROUP_W] = o.astype(o_ref.dtype)


def _dft_seq_kernel(w_ref, u_ref, g_ref, cc_ref, sc_ref, o_ref, x1_ref):
    j = pl.program_id(0)
    n1, grp, cw = u_ref.shape
    n_a = x1_ref.shape[1] // grp
    kb = g_ref.shape[0]

    @pl.when(j < n_a)
    def _():
        y = jnp.dot(w_ref[...], u_ref[...].reshape(n1 * grp, cw), preferred_element_type=F32)
        s2 = pl.multiple_of(j * grp, grp)
        x1_ref[:, pl.ds(s2, grp), :] = y.astype(BF16).reshape(2 * n1, grp, cw)

    @pl.when(j >= n_a)
    def _():
        k0 = (j - n_a) * kb
        for b in range(kb):
            x = jnp.concatenate([x1_ref[k0 + b], x1_ref[n1 + k0 + b]], axis=0)
            _dft_cols(g_ref[b], x, cc_ref, sc_ref, o_ref, b * cw)


def _dft_seq_call(w1k, g, uf, cc, sc):
    s, cw = uf.shape
    n1 = DFT_ROWS
    n2 = s // n1
    grp, kb, gw = DFT_S2_GROUP, DFT_K1_GROUP, FOURIER_GROUP_W
    n_a, n_b = n2 // grp, n1 // kb
    stage2 = lambda j: jnp.maximum(j - n_a, 0)
    out = pl.pallas_call(
        _dft_seq_kernel,
        grid=(n_a + n_b,),
        in_specs=[pl.BlockSpec((2 * n1 * grp, n1 * grp), lambda j: (0, 0)),
                  pl.BlockSpec((n1, grp, cw), lambda j: (0, jnp.minimum(j, n_a - 1), 0)),
                  pl.BlockSpec((kb, 2 * n2, 2 * n2), lambda j: (stage2(j), 0, 0)),
                  pl.BlockSpec((gw, gw), lambda j: (0, 0)),
                  pl.BlockSpec((gw, gw), lambda j: (0, 0))],
        out_specs=pl.BlockSpec((n2, kb * cw), lambda j: (0, stage2(j))),
        out_shape=jax.ShapeDtypeStruct((n2, n1 * cw), BF16),
        scratch_shapes=[pltpu.VMEM((2 * n1, n2, cw), BF16)],
        compiler_params=_params("arbitrary"),
        name="dft_seq",
    )(w1k, uf.reshape(n1, n2, cw), g, cc, sc)
    return out.reshape(s, cw)


def _dft_ctx_kernel(g_ref, u_ref, cc_ref, sc_ref, o_ref):
    _dft_cols(g_ref[0], u_ref[...], cc_ref, sc_ref, o_ref, 0)


def _dft_small_call(g, uf, cc, sc):
    c, cw = uf.shape
    gw = FOURIER_GROUP_W
    return pl.pallas_call(
        _dft_ctx_kernel,
        grid=(1,),
        in_specs=[pl.BlockSpec((1, 2 * c, c), lambda k: (0, 0, 0)),
                  pl.BlockSpec((c, cw), lambda k: (0, 0)),
                  pl.BlockSpec((gw, gw), lambda k: (0, 0)),
                  pl.BlockSpec((gw, gw), lambda k: (0, 0))],
        out_specs=pl.BlockSpec((c, cw), lambda k: (0, 0)),
        out_shape=jax.ShapeDtypeStruct((c, cw), BF16),
        compiler_params=_params("arbitrary"),
        name="dft_ctx",
    )(g, uf, cc, sc)


def _merge_kernel(a_ref, f_ref, up_ref, upp_ref, upn_ref, ga_ref, gf_ref, gp_ref, x_ref,
                  gate_ref, ps_ref, wo_ref, wf_ref, wp_ref, wout_ref, o_ref, ext_ref, *, s_len):
    i = pl.program_id(0)
    tm = up_ref.shape[0]
    h = POOL_HALO
    ext_ref[0:h] = jnp.where(i > 0, upp_ref[...].astype(F32), 0.0)
    ext_ref[h:h + tm] = up_ref[...].astype(F32)
    ext_ref[h + tm:] = jnp.where(i < pl.num_programs(0) - 1, upn_ref[...].astype(F32), 0.0)

    t = i * tm + lax.broadcasted_iota(jnp.int32, (tm, 1), 0)
    pooled = []
    for gi, w in enumerate(POOL_WINDOWS):
        sl = slice(gi * POOL_GROUP_W, (gi + 1) * POOL_GROUP_W)
        total = ext_ref[h - w // 2:h - w // 2 + tm, sl]
        for off in range(-(w // 2) + 1, w - w // 2):
            total = total + ext_ref[h + off:h + off + tm, sl]
        count = jnp.minimum(t + (w - w // 2), s_len) - jnp.maximum(t - w // 2, 0)
        u = ext_ref[h:h + tm, sl]
        pooled.append((total / count.astype(F32) - u).astype(BF16))
    yp = jnp.concatenate(
        [jnp.dot(p, wp_ref[gi], preferred_element_type=F32) for gi, p in enumerate(pooled)], axis=-1)

    ya = jnp.dot(a_ref[...], wo_ref[...], preferred_element_type=F32)
    yf = jnp.dot(f_ref[...], wf_ref[...], preferred_element_type=F32)
    y = (jax.nn.sigmoid(ga_ref[...].astype(F32)) * ya
         + jax.nn.sigmoid(gf_ref[...].astype(F32)) * yf
         + jax.nn.sigmoid(gp_ref[...].astype(F32)) * (yp * ps_ref[...]))
    mix = jnp.dot(y.astype(BF16), wout_ref[...], preferred_element_type=F32)
    o_ref[...] = x_ref[...] + gate_ref[...] * mix


def _merge_call(a, f, up, g, x, gate, pool_scale, w_attn_o, w_fourier, w_pool, w_out, l, tm):
    s, d = x.shape
    h = POOL_HALO
    n_halo = s // h
    row = pl.BlockSpec((1, d), lambda i: (0, 0))
    once = dict(pipeline_mode=pl.Buffered(1))
    return pl.pallas_call(
        functools.partial(_merge_kernel, s_len=s),
        grid=(s // tm,),
        in_specs=[
            pl.BlockSpec((tm, ATTN_W), lambda i: (i, 0)),
            pl.BlockSpec((tm, FOURIER_W), lambda i: (i, 0)),
            pl.BlockSpec((tm, POOL_W), lambda i: (i, 0)),
            pl.BlockSpec((h, POOL_W), lambda i: (jnp.maximum(i * (tm // h) - 1, 0), 0)),
            pl.BlockSpec((h, POOL_W), lambda i: (jnp.minimum((i + 1) * (tm // h), n_halo - 1), 0)),
            pl.BlockSpec((tm, d), lambda i: (i, 0)),
            pl.BlockSpec((tm, d), lambda i: (i, 1)),
            pl.BlockSpec((tm, d), lambda i: (i, 2)),
            pl.BlockSpec((tm, d), lambda i: (i, 0)),
            row, row,
            pl.BlockSpec((None, ATTN_W, d), lambda i: (l, 0, 0), **once),
            pl.BlockSpec((None, FOURIER_W, d), lambda i: (l, 0, 0), **once),
            pl.BlockSpec((None, len(POOL_WINDOWS), POOL_GROUP_W, POOL_OUT_GROUP_W),
                         lambda i: (l, 0, 0, 0), **once),
            pl.BlockSpec((None, d, d), lambda i: (l, 0, 0), **once),
        ],
        out_specs=pl.BlockSpec((tm, d), lambda i: (i, 0)),
        out_shape=jax.ShapeDtypeStruct((s, d), F32),
        scratch_shapes=[pltpu.VMEM((tm + 2 * h, POOL_W), F32)],
        compiler_params=_params("arbitrary"),
        name="merge",
    )(a, f, up, up, up, g, g, g, x, gate, pool_scale, w_attn_o, w_fourier, w_pool, w_out)


def _rope_tables(s_len):
    rows_n = s_len // GRID_W
    n_freq = D_HEAD // 4
    freqs = ROPE_BASE ** (-jnp.arange(n_freq, dtype=F32) / n_freq)
    ar = jnp.arange(rows_n).astype(F32)[:, None] * freqs
    ac = jnp.arange(GRID_W).astype(F32)[:, None] * freqs
    by_row = lambda t: jnp.repeat(t, GRID_W, axis=0)
    by_col = lambda t: jnp.tile(t, (rows_n, 1))
    cr, sr, cc, sc = by_row(jnp.cos(ar)), by_row(jnp.sin(ar)), by_col(jnp.cos(ac)), by_col(jnp.sin(ac))
    cos = jnp.concatenate([cr, cr, cc, cc], axis=-1)
    sin = jnp.concatenate([-sr, sr, -sc, sc], axis=-1)
    return cos, sin


def kernel(x, c, ctx, c_ctx, w_ada, b_ada, norm_w, ffn_w_gate, ffn_w_up, ffn_w_down,
           w_in, q_gain, k_gain, sink, w_attn_o, w_fourier, w_pool, pool_scale, w_out):
    b, s_len, d = x.shape
    c_len = ctx.shape[1]
    assert b == 1 and d == D_MODEL and s_len % (DFT_ROWS * BLOCK) == 0
    x, ctx = x[0], ctx[0]

    cond = jnp.zeros((8, d), F32).at[0].set(c[0]).at[1].set(c_ctx)
    mod = _ada_call(cond, w_ada, b_ada).reshape(DEPTH, 8, N_MOD, 1, d)

    ffn_f32 = (ffn_w_gate, ffn_w_up, ffn_w_down)
    ffn_w = (ffn_f32, 0, 0)
    w_ao, w_fo = w_attn_o.astype(BF16), w_fourier.astype(BF16)
    w_po, w_ou = w_pool.astype(BF16), w_out.astype(BF16)

    cos_x, sin_x = _rope_tables(s_len)
    cos_c, sin_c = jnp.ones((c_len, D_HEAD), F32), jnp.zeros((c_len, D_HEAD), F32)
    w1, g_x, cc, sc = (jnp.asarray(t).astype(BF16) for t in _dft_tables(s_len))
    _, g_c, cc_c, sc_c = (jnp.asarray(t).astype(BF16) for t in _dft_tables(c_len))

    tm_x, tm_m = ROW_TILE, MERGE_ROW_TILE
    for l in range(DEPTH):
        last = l == DEPTH - 1
        mx, mc = mod[l, 0], mod[l, 1]
        nw = norm_w[l][:, None, :]
        qg, kg = q_gain[l][None, :], k_gain[l][None, :]
        ps = pool_scale[l][None, :]

        x, ffn_w_post = _ffn_call(x, nw[0], mx[0], mx[1], mx[2], ffn_w, tm_x, (ffn_f32, l, 1))
        ctx, _ = _ffn_call(ctx, nw[0], mc[0], mc[1], mc[2], ffn_w, c_len)

        qx, kvx, ufx, upx, gx = _inproj_call(x, nw[1], mx[3], mx[4], qg, kg, cos_x, sin_x, w_in, l, tm_x)
        qc, kvc, ufc, upc, gc = _inproj_call(ctx, nw[1], mc[3], mc[4], qg, kg, cos_c, sin_c, w_in, l, c_len)

        ax = _window_attn_call(qx, kvx, kvc, sink[l])
        fx = _dft_seq_call(w1, g_x, ufx, cc, sc)
        x = _merge_call(ax, fx, upx, gx, x, mx[5], ps, w_ao, w_fo, w_po, w_ou, l, tm_m)
        if not last:
            ac = _ctx_attn_call(qc, kvc, sink[l])
            fc = _dft_small_call(g_c, ufc, cc_c, sc_c)
            ctx = _merge_call(ac, fc, upc, gc, ctx, mc[5], ps, w_ao, w_fo, w_po, w_ou, l, c_len)

        x, ffn_w_next = _ffn_call(x, nw[2], mx[6], mx[7], mx[8], ffn_w_post, tm_x,
                                  None if last else (ffn_f32, l + 1, 0))
        if not last:
            ctx, _ = _ffn_call(ctx, nw[2], mc[6], mc[7], mc[8], ffn_w_post, c_len)
        ffn_w = ffn_w_next
    return x[None]
```

```python
import functools

import numpy as np
import jax
import jax.numpy as jnp
from jax import lax
from jax.experimental import pallas as pl
from jax.experimental.pallas import tpu as pltpu

D_MODEL = 2048
DEPTH = 4
GRID_W = 64
N_HEADS = 8
N_KV_HEADS = 2
GROUP = N_HEADS // N_KV_HEADS
D_HEAD = 128
ATTN_W = N_HEADS * D_HEAD
KV_W = N_KV_HEADS * D_HEAD
WINDOW = 128
BLOCK = 128
FOURIER_W = D_MODEL // 4
N_FOURIER_GROUPS = 4
FOURIER_GROUP_W = FOURIER_W // N_FOURIER_GROUPS
POOL_WINDOWS = (2, 4, 8, 16)
POOL_W = D_MODEL // 4
POOL_GROUP_W = POOL_W // len(POOL_WINDOWS)
POOL_OUT_GROUP_W = D_MODEL // len(POOL_WINDOWS)
N_BRANCHES = 3
GATE_W = N_BRANCHES * D_MODEL
IN_W = ATTN_W + 2 * KV_W + FOURIER_W + POOL_W + GATE_W
D_FF = 5632
N_MOD = 9
ROPE_BASE = 10000.0
EPS = 1e-6
NEG_INF = -1e30

BF16 = jnp.bfloat16
F32 = jnp.float32

VMEM_LIMIT_BYTES = 60 * 1024 * 1024
ROW_TILE = 1024
MERGE_ROW_TILE = 256
ADA_TILE_N = 1024
POOL_HALO = 16
COL_TILE = 512
FFN_TILE_F = 512
LOG2_E = 1.4426950408889634
Q_SCALE = LOG2_E * D_HEAD ** -0.5
DFT_ROWS = GRID_W
DFT_S2_GROUP = 16


def _params(*semantics):
    return pltpu.CompilerParams(dimension_semantics=semantics, vmem_limit_bytes=VMEM_LIMIT_BYTES)


def _modulate(x, gain, shift, scale):
    y = x * lax.rsqrt(jnp.mean(x * x, axis=-1, keepdims=True) + EPS)
    return (y * gain) * (1 + scale) + shift


def _ada_kernel(cond_ref, w_ref, b_ref, o_ref):
    h = jax.nn.silu(cond_ref[...]).astype(BF16)
    o_ref[0] = jnp.dot(h, w_ref[0].astype(BF16), preferred_element_type=F32) + b_ref[0]


def _ada_call(cond, w_ada, b_ada):
    depth, d, n = w_ada.shape
    tn = ADA_TILE_N
    return pl.pallas_call(
        _ada_kernel,
        grid=(depth, n // tn),
        in_specs=[
            pl.BlockSpec((8, d), lambda l, j: (0, 0)),
            pl.BlockSpec((1, d, tn), lambda l, j: (l, 0, j)),
            pl.BlockSpec((1, 1, tn), lambda l, j: (l, 0, j)),
        ],
        out_specs=pl.BlockSpec((1, 8, tn), lambda l, j: (l, 0, j)),
        out_shape=jax.ShapeDtypeStruct((depth, 8, n), F32),
        compiler_params=_params("arbitrary", "arbitrary"),
        name="adaln",
    )(cond, w_ada, b_ada.reshape(depth, 1, n))


def _ffn_kernel(*refs, convert_next):
    if convert_next:
        (x_hbm, nw_ref, sh_ref, sc_ref, gt_ref, wg_ref, wu_ref, wd_ref, ng_ref, nu_ref, nd_ref,
         o_hbm, cg_ref, cu_ref, cd_ref, xs_ref, acc_ref, h_ref, sem) = refs
    else:
        (x_hbm, nw_ref, sh_ref, sc_ref, gt_ref, wg_ref, wu_ref, wd_ref,
         o_hbm, xs_ref, acc_ref, h_ref, sem) = refs
    i, j = pl.program_id(0), pl.program_id(1)
    n_tiles, n_ff = pl.num_programs(0), pl.num_programs(1)
    tm = xs_ref.shape[0]

    def x_copy(tile):
        return pltpu.make_async_copy(x_hbm.at[pl.ds(tile * tm, tm)], xs_ref, sem.at[0])

    def o_copy(tile):
        return pltpu.make_async_copy(acc_ref, o_hbm.at[pl.ds(tile * tm, tm)], sem.at[1])

    def gate_up():
        h = h_ref[...]
        g = jnp.dot(h, wg_ref[...].astype(BF16), preferred_element_type=F32)
        u = jnp.dot(h, wu_ref[...].astype(BF16), preferred_element_type=F32)
        return (jax.nn.silu(g) * u).astype(BF16)

    def down(a):
        return (0.5 * gt_ref[...]) * jnp.dot(a, wd_ref[...].astype(BF16), preferred_element_type=F32)

    def convert():
        if convert_next:
            cg_ref[...] = ng_ref[...].astype(BF16)
            cu_ref[...] = nu_ref[...].astype(BF16)
            cd_ref[...] = nd_ref[...].astype(BF16)

    @pl.when(j == 0)
    def _():
        @pl.when(i == 0)
        def _():
            x_copy(0).start()

        x_copy(i).wait()
        h_ref[...] = _modulate(xs_ref[...], nw_ref[...], sh_ref[...], sc_ref[...]).astype(BF16)
        a = gate_up()
        convert()

        @pl.when(i > 0)
        def _():
            o_copy(i - 1).wait()

        acc_ref[...] = xs_ref[...] + down(a)

        @pl.when(i + 1 < n_tiles)
        def _():
            x_copy(i + 1).start()

    @pl.when(j > 0)
    def _():
        acc_ref[...] += down(gate_up())
        convert()

    @pl.when(j == n_ff - 1)
    def _():
        o_copy(i).start()

        @pl.when(i == n_tiles - 1)
        def _():
            o_copy(i).wait()


def _ffn_call(x, nw, shift, scale, gate, w, tm, next_f32=None):
    s, d = x.shape
    n_tiles = s // tm
    row = pl.BlockSpec((1, d), lambda i, j: (0, 0))
    if not isinstance(w[0], tuple):
        tf = FFN_TILE_F
        w_args = list(w)
        w_specs = [pl.BlockSpec((None, d, tf), lambda i, j: (j, 0, 0)),
                   pl.BlockSpec((None, d, tf), lambda i, j: (j, 0, 0)),
                   pl.BlockSpec((tf, d), lambda i, j: (j, 0))]
    else:
        tf = FFN_TILE_F
        w_args, l0, w0 = w
        w_args = list(w_args)
        w_specs = [pl.BlockSpec((None, None, d, tf), lambda i, j: (l0, w0, 0, j)),
                   pl.BlockSpec((None, None, d, tf), lambda i, j: (l0, w0, 0, j)),
                   pl.BlockSpec((None, None, tf, d), lambda i, j: (l0, w0, j, 0))]
    in_specs = [pl.BlockSpec(memory_space=pl.ANY), row, row, row, row, *w_specs]
    out_specs = [pl.BlockSpec(memory_space=pl.ANY)]
    out_shape = [jax.ShapeDtypeStruct((s, d), F32)]
    args = [x, nw, shift, scale, gate, *w_args]
    if next_f32 is not None:
        (ng, nu, nd), l2, w2 = next_f32
        n_ff = D_FF // tf
        rb = d // n_tiles
        rd = D_FF // (n_tiles * n_ff)
        assert rb * n_tiles == d and rd * n_tiles * n_ff == D_FF
        in_specs += [
            pl.BlockSpec((None, None, rb, tf), lambda i, j: (l2, w2, i, j)),
            pl.BlockSpec((None, None, rb, tf), lambda i, j: (l2, w2, i, j)),
            pl.BlockSpec((None, None, rd, d), lambda i, j: (l2, w2, i * n_ff + j, 0)),
        ]
        out_specs += [
            pl.BlockSpec((None, rb, tf), lambda i, j: (j, i, 0)),
            pl.BlockSpec((None, rb, tf), lambda i, j: (j, i, 0)),
            pl.BlockSpec((rd, d), lambda i, j: (i * n_ff + j, 0)),
        ]
        out_shape += [jax.ShapeDtypeStruct((n_ff, d, tf), BF16), jax.ShapeDtypeStruct((n_ff, d, tf), BF16),
                      jax.ShapeDtypeStruct((D_FF, d), BF16)]
        args += [ng, nu, nd]
    outs = pl.pallas_call(
        functools.partial(_ffn_kernel, convert_next=next_f32 is not None),
        grid=(n_tiles, D_FF // tf),
        in_specs=in_specs,
        out_specs=out_specs,
        out_shape=out_shape,
        scratch_shapes=[pltpu.VMEM((tm, d), F32), pltpu.VMEM((tm, d), F32), pltpu.VMEM((tm, d), BF16),
                        pltpu.SemaphoreType.DMA((2,))],
        compiler_params=_params("arbitrary", "arbitrary"),
        name="ffn",
    )(*args)
    return outs[0], (tuple(outs[1:]) if next_f32 is not None else None)


Q_TILES = ATTN_W // COL_TILE
KV_TILE = Q_TILES
UF_TILE = KV_TILE + 1
UP_TILE = UF_TILE + 1
G_TILE0 = UP_TILE + 1
N_COL_TILES = IN_W // COL_TILE


def _norm_rope(z, gain, cos, sin_signed):
    y = z * lax.rsqrt(jnp.mean(z * z, axis=-1, keepdims=True) + EPS) * gain
    lane = lax.broadcasted_iota(jnp.int32, y.shape, 1)
    first_half = (lane & (D_HEAD // 4)) == 0
    partner = jnp.where(first_half, pltpu.roll(y, D_HEAD - D_HEAD // 4, 1), pltpu.roll(y, D_HEAD // 4, 1))
    return y * cos + partner * sin_signed


STEP_Q, STEP_KV_UF, STEP_UP, STEP_G0 = 0, 1, 2, 3


def _inproj_kernel(x_ref, nw_ref, sh_ref, sc_ref, qg_ref, kg_ref, cos_ref, sin_ref, w_ref, w2_ref,
                   q_ref, kv_ref, uf_ref, up_ref, g_ref, h_ref, z_ref):
    j = pl.program_id(1)

    def proj(w):
        return jnp.dot(h_ref[...], w[...].astype(BF16), preferred_element_type=F32)

    def q_epilogue(z, col0):
        cos, sin = cos_ref[...], sin_ref[...]
        for hh in range(COL_TILE // D_HEAD):
            sl = slice(hh * D_HEAD, (hh + 1) * D_HEAD)
            q = _norm_rope(z[:, sl], qg_ref[...], cos, sin) * Q_SCALE
            q_ref[:, col0 + hh * D_HEAD:col0 + (hh + 1) * D_HEAD] = q.astype(BF16)

    def kv_epilogue(z):
        cos, sin = cos_ref[...], sin_ref[...]
        for hh in range(N_KV_HEADS):
            sl = slice(hh * D_HEAD, (hh + 1) * D_HEAD)
            kv_ref[:, sl] = _norm_rope(z[:, sl], kg_ref[...], cos, sin).astype(BF16)
        kv_ref[:, KV_W:] = z[:, KV_W:].astype(BF16)

    @pl.when(j == STEP_Q)
    def _():
        h_ref[...] = _modulate(x_ref[...], nw_ref[...], sh_ref[...], sc_ref[...]).astype(BF16)
        z_ref[0] = proj(w_ref)
        z_ref[1] = proj(w2_ref)

    @pl.when(j == STEP_KV_UF)
    def _():
        z_ref[2] = proj(w_ref)
        uf_ref[...] = proj(w2_ref).astype(BF16)
        for t in range(Q_TILES):
            q_epilogue(z_ref[t], t * COL_TILE)

    @pl.when(j == STEP_UP)
    def _():
        up_ref[...] = proj(w_ref).astype(BF16)
        kv_epilogue(z_ref[2])

    @pl.when(j >= STEP_G0)
    def _():
        g_ref[:, :COL_TILE] = proj(w_ref).astype(BF16)
        g_ref[:, COL_TILE:] = proj(w2_ref).astype(BF16)


def _inproj_call(x, nw, shift, scale, qg, kg, cos, sin, w_in, l, tm):
    s, d = x.shape
    ct = COL_TILE
    assert (Q_TILES, KV_TILE, UF_TILE, UP_TILE, G_TILE0) == (2, 2, 3, 4, 5) and (N_COL_TILES - G_TILE0) % 2 == 0
    row = pl.BlockSpec((1, d), lambda i, j: (0, 0))
    hrow = pl.BlockSpec((1, D_HEAD), lambda i, j: (0, 0))
    tab = pl.BlockSpec((tm, D_HEAD), lambda i, j: (i, 0))
    n_steps = STEP_G0 + (N_COL_TILES - G_TILE0) // 2
    tile_a = lambda j: 2 * j - (j >= STEP_G0).astype(jnp.int32)
    tile_b = lambda j: jnp.where(j < STEP_UP, 2 * j + 1, jnp.maximum(2 * j, G_TILE0 + 1))
    return pl.pallas_call(
        _inproj_kernel,
        grid=(s // tm, n_steps),
        in_specs=[
            pl.BlockSpec((tm, d), lambda i, j: (i, 0)),
            row, row, row, hrow, hrow, tab, tab,
            pl.BlockSpec((None, d, ct), lambda i, j: (l, 0, tile_a(j))),
            pl.BlockSpec((None, d, ct), lambda i, j: (l, 0, tile_b(j))),
        ],
        out_specs=[
            pl.BlockSpec((tm, Q_TILES * ct), lambda i, j: (i, 0)),
            pl.BlockSpec((tm, ct), lambda i, j: (i, 0)),
            pl.BlockSpec((tm, ct), lambda i, j: (i, 0)),
            pl.BlockSpec((tm, ct), lambda i, j: (i, 0)),
            pl.BlockSpec((tm, 2 * ct), lambda i, j: (i, jnp.maximum(j - STEP_G0, 0))),
        ],
        out_shape=[
            jax.ShapeDtypeStruct((s, ATTN_W), BF16),
            jax.ShapeDtypeStruct((s, 2 * KV_W), BF16),
            jax.ShapeDtypeStruct((s, FOURIER_W), BF16),
            jax.ShapeDtypeStruct((s, POOL_W), BF16),
            jax.ShapeDtypeStruct((s, GATE_W), BF16),
        ],
        scratch_shapes=[pltpu.VMEM((tm, d), BF16), pltpu.VMEM((3, tm, ct), F32)],
        compiler_params=_params("arbitrary", "arbitrary"),
        name="inproj",
    )(x, nw, shift, scale, qg, kg, cos, sin, w_in, w_in)


def _qk(q, k):
    return lax.dot_general(q, k, (((1,), (1,)), ((), ())), preferred_element_type=F32)


def _k_cols(kvh):
    return slice(kvh * D_HEAD, (kvh + 1) * D_HEAD)


def _v_cols(kvh):
    return slice(KV_W + kvh * D_HEAD, KV_W + (kvh + 1) * D_HEAD)


def _sink_column(sink_ref, kvh, rows):
    head = lax.broadcasted_iota(jnp.int32, (GROUP * rows, 1), 0) // rows
    col = jnp.full((GROUP * rows, 1), sink_ref[kvh * GROUP], F32)
    for hh in range(1, GROUP):
        col = jnp.where(head == hh, sink_ref[kvh * GROUP + hh], col)
    return col * LOG2_E


def _stack_heads(q_ref, kvh):
    heads = range(kvh * GROUP, (kvh + 1) * GROUP)
    return jnp.concatenate([q_ref[:, hh * D_HEAD:(hh + 1) * D_HEAD] for hh in heads], axis=0)


def _softmax_pv(scores, values, sink_col):
    def lane_tiles(a):
        return [a[:, t:t + D_HEAD] for t in range(0, a.shape[1], D_HEAD)]

    m_tile = functools.reduce(jnp.maximum, [t for s in scores for t in lane_tiles(s)])
    m = jnp.maximum(sink_col, jnp.max(m_tile, axis=-1, keepdims=True))
    e_tile = None
    acc = None
    for s, v in zip(scores, values):
        e = jnp.exp2(s - m)
        for t in lane_tiles(e):
            e_tile = t if e_tile is None else e_tile + t
        pv = jnp.dot(e.astype(BF16), v, preferred_element_type=F32)
        acc = pv if acc is None else acc + pv
    denom = jnp.exp2(sink_col - m) + jnp.sum(e_tile, axis=-1, keepdims=True)
    return acc / denom


def _unstack_heads(out, o_ref, kvh, rows):
    for g in range(GROUP):
        hh = kvh * GROUP + g
        o_ref[:, hh * D_HEAD:(hh + 1) * D_HEAD] = out[g * rows:(g + 1) * rows].astype(o_ref.dtype)


ATTN_Q_BLOCKS = 4


def _window_attn_kernel(sink_ref, q_ref, *refs):
    kv_refs, kvx_ref, o_ref = refs[:-2], refs[-2], refs[-1]
    i = pl.program_id(0)
    n_steps = pl.num_programs(0)
    shape = (GROUP * BLOCK, BLOCK)
    qi = lax.broadcasted_iota(jnp.int32, shape, 0) % BLOCK
    kj = lax.broadcasted_iota(jnp.int32, shape, 1)
    for b in range(ATTN_Q_BLOCKS):
        kvp_ref, kvc_ref, kvn_ref = kv_refs[b:b + 3]
        keep_prev = (kj >= qi) if b > 0 else (kj >= qi) & (i > 0)
        keep_next = (kj <= qi) if b < ATTN_Q_BLOCKS - 1 else (kj <= qi) & (i < n_steps - 1)
        rows = slice(b * BLOCK, (b + 1) * BLOCK)
        for kvh in range(N_KV_HEADS):
            q = _stack_heads(q_ref.at[rows], kvh)
            ks, vs = _k_cols(kvh), _v_cols(kvh)
            s_prev = jnp.where(keep_prev, _qk(q, kvp_ref[:, ks]), NEG_INF)
            s_cur = _qk(q, kvc_ref[:, ks])
            s_next = jnp.where(keep_next, _qk(q, kvn_ref[:, ks]), NEG_INF)
            s_ctx = _qk(q, kvx_ref[:, ks])
            out = _softmax_pv([s_prev, s_cur, s_next, s_ctx],
                              [kvp_ref[:, vs], kvc_ref[:, vs], kvn_ref[:, vs], kvx_ref[:, vs]],
                              _sink_column(sink_ref, kvh, BLOCK))
            _unstack_heads(out, o_ref.at[rows], kvh, BLOCK)


def _window_attn_call(q, kv, kv_ctx, sink):
    s = q.shape[0]
    c = kv_ctx.shape[0]
    nb = s // BLOCK
    qb = ATTN_Q_BLOCKS
    kblk = lambda off: pl.BlockSpec((BLOCK, 2 * KV_W), lambda i: (jnp.clip(i * qb + off, 0, nb - 1), 0))
    return pl.pallas_call(
        _window_attn_kernel,
        grid=(nb // qb,),
        in_specs=[
            pl.BlockSpec(memory_space=pltpu.SMEM),
            pl.BlockSpec((qb * BLOCK, ATTN_W), lambda i: (i, 0)),
            *[kblk(off) for off in range(-1, qb + 1)],
            pl.BlockSpec((c, 2 * KV_W), lambda i: (0, 0)),
        ],
        out_specs=pl.BlockSpec((qb * BLOCK, ATTN_W), lambda i: (i, 0)),
        out_shape=jax.ShapeDtypeStruct((s, ATTN_W), BF16),
        compiler_params=_params("arbitrary"),
        name="window_attn",
    )(sink, q, *([kv] * (qb + 2)), kv_ctx)


def _ctx_attn_kernel(sink_ref, q_ref, kv_ref, o_ref):
    rows = q_ref.shape[0]
    for kvh in range(N_KV_HEADS):
        q = _stack_heads(q_ref, kvh)
        out = _softmax_pv([_qk(q, kv_ref[:, _k_cols(kvh)])], [kv_ref[:, _v_cols(kvh)]],
                          _sink_column(sink_ref, kvh, rows))
        _unstack_heads(out, o_ref, kvh, rows)


def _ctx_attn_call(q, kv, sink):
    c = q.shape[0]
    return pl.pallas_call(
        _ctx_attn_kernel,
        grid=(1,),
        in_specs=[
            pl.BlockSpec(memory_space=pltpu.SMEM),
            pl.BlockSpec((c, ATTN_W), lambda i: (0, 0)),
            pl.BlockSpec((c, 2 * KV_W), lambda i: (0, 0)),
        ],
        out_specs=pl.BlockSpec((c, ATTN_W), lambda i: (0, 0)),
        out_shape=jax.ShapeDtypeStruct((c, ATTN_W), BF16),
        compiler_params=_params("arbitrary"),
        name="ctx_attn",
    )(sink, q, kv)


def _dft_tables(s_len):
    n1 = DFT_ROWS if s_len > 1024 else 1
    n2 = s_len // n1
    k1 = np.arange(n1)
    ang1 = 2.0 * np.pi * ((k1[:, None] * k1[None, :]) % n1) / n1
    w1 = np.concatenate([np.cos(ang1), -np.sin(ang1)], axis=0)
    w1 = np.kron(w1, np.eye(DFT_S2_GROUP))
    k2 = np.arange(n2)
    phase = (k2[None, None, :] * (k1[:, None, None] + n1 * k2[None, :, None])) % s_len
    ang2 = 2.0 * np.pi * phase / s_len
    gr, gi = np.cos(ang2), -np.sin(ang2)
    g = np.concatenate([np.concatenate([gr, -gi], axis=2),
                        np.concatenate([gi, gr], axis=2)], axis=1)
    if n1 == 1:
        g = g[:, :, :n2]
    c = np.arange(FOURIER_GROUP_W)
    angc = 2.0 * np.pi * ((c[:, None] * c[None, :]) % FOURIER_GROUP_W) / FOURIER_GROUP_W
    norm = 1.0 / np.sqrt(float(s_len) * FOURIER_GROUP_W)
    return (w1.astype(np.float32), g.astype(np.float32),
            (np.cos(angc) * norm).astype(np.float32), (np.sin(angc) * norm).astype(np.float32))


DFT_K1_GROUP = 8


def _dft_cols(g, x, cc_ref, sc_ref, o_ref, col0):
    y = jnp.dot(g, x, preferred_element_type=F32)
    n = g.shape[0] // 2
    yr, yi = y[:n].astype(BF16), y[n:].astype(BF16)
    for gi in range(N_FOURIER_GROUPS):
        sl = slice(gi * FOURIER_GROUP_W, (gi + 1) * FOURIER_GROUP_W)
        o = (jnp.dot(yr[:, sl], cc_ref[...], preferred_element_type=F32)
             + jnp.dot(yi[:, sl], sc_ref[...], preferred_element_type=F32))
        o_ref[:, col0 + gi * FOURIER_GROUP_W:col0 + (gi + 1) * FOURIER_GROUP_W] = o.astype(o_ref.dtype)


def _dft_seq_kernel(w_ref, u_ref, g_ref, cc_ref, sc_ref, o_ref, x1_ref):
    j = pl.program_id(0)
    n1, grp, cw = u_ref.shape
    n_a = x1_ref.shape[1] // grp
    kb = g_ref.shape[0]

    @pl.when(j < n_a)
    def _():
        y = jnp.dot(w_ref[...], u_ref[...].reshape(n1 * grp, cw), preferred_element_type=F32)
        s2 = pl.multiple_of(j * grp, grp)
        x1_ref[:, pl.ds(s2, grp), :] = y.astype(BF16).reshape(2 * n1, grp, cw)

    @pl.when(j >= n_a)
    def _():
        k0 = (j - n_a) * kb
        for b in range(kb):
            x = jnp.concatenate([x1_ref[k0 + b], x1_ref[n1 + k0 + b]], axis=0)
            _dft_cols(g_ref[b], x, cc_ref, sc_ref, o_ref, b * cw)


def _dft_seq_call(w1k, g, uf, cc, sc):
    s, cw = uf.shape
    n1 = DFT_ROWS
    n2 = s // n1
    grp, kb, gw = DFT_S2_GROUP, DFT_K1_GROUP, FOURIER_GROUP_W
    n_a, n_b = n2 // grp, n1 // kb
    stage2 = lambda j: jnp.maximum(j - n_a, 0)
    out = pl.pallas_call(
        _dft_seq_kernel,
        grid=(n_a + n_b,),
        in_specs=[pl.BlockSpec((2 * n1 * grp, n1 * grp), lambda j: (0, 0)),
                  pl.BlockSpec((n1, grp, cw), lambda j: (0, jnp.minimum(j, n_a - 1), 0)),
                  pl.BlockSpec((kb, 2 * n2, 2 * n2), lambda j: (stage2(j), 0, 0)),
                  pl.BlockSpec((gw, gw), lambda j: (0, 0)),
                  pl.BlockSpec((gw, gw), lambda j: (0, 0))],
        out_specs=pl.BlockSpec((n2, kb * cw), lambda j: (0, stage2(j))),
        out_shape=jax.ShapeDtypeStruct((n2, n1 * cw), BF16),
        scratch_shapes=[pltpu.VMEM((2 * n1, n2, cw), BF16)],
        compiler_params=_params("arbitrary"),
        name="dft_seq",
    )(w1k, uf.reshape(n1, n2, cw), g, cc, sc)
    return out.reshape(s, cw)


def _dft_ctx_kernel(g_ref, u_ref, cc_ref, sc_ref, o_ref):
    _dft_cols(g_ref[0], u_ref[...], cc_ref, sc_ref, o_ref, 0)


def _dft_small_call(g, uf, cc, sc):
    c, cw = uf.shape
    gw = FOURIER_GROUP_W
    return pl.pallas_call(
        _dft_ctx_kernel,
        grid=(1,),
        in_specs=[pl.BlockSpec((1, 2 * c, c), lambda k: (0, 0, 0)),
                  pl.BlockSpec((c, cw), lambda k: (0, 0)),
                  pl.BlockSpec((gw, gw), lambda k: (0, 0)),
                  pl.BlockSpec((gw, gw), lambda k: (0, 0))],
        out_specs=pl.BlockSpec((c, cw), lambda k: (0, 0)),
        out_shape=jax.ShapeDtypeStruct((c, cw), BF16),
        compiler_params=_params("arbitrary"),
        name="dft_ctx",
    )(g, uf, cc, sc)


def _merge_kernel(a_ref, f_ref, up_ref, upp_ref, upn_ref, ga_ref, gf_ref, gp_ref, x_ref,
                  gate_ref, ps_ref, wo_ref, wf_ref, wp_ref, wout_ref, o_ref, ext_ref, *, s_len):
    i = pl.program_id(0)
    tm = up_ref.shape[0]
    h = POOL_HALO
    ext_ref[0:h] = jnp.where(i > 0, upp_ref[...].astype(F32), 0.0)
    ext_ref[h:h + tm] = up_ref[...].astype(F32)
    ext_ref[h + tm:] = jnp.where(i < pl.num_programs(0) - 1, upn_ref[...].astype(F32), 0.0)

    t = i * tm + lax.broadcasted_iota(jnp.int32, (tm, 1), 0)
    pooled = []
    for gi, w in enumerate(POOL_WINDOWS):
        sl = slice(gi * POOL_GROUP_W, (gi + 1) * POOL_GROUP_W)
        total = ext_ref[h - w // 2:h - w // 2 + tm, sl]
        for off in range(-(w // 2) + 1, w - w // 2):
            total = total + ext_ref[h + off:h + off + tm, sl]
        count = jnp.minimum(t + (w - w // 2), s_len) - jnp.maximum(t - w // 2, 0)
        u = ext_ref[h:h + tm, sl]
        pooled.append((total / count.astype(F32) - u).astype(BF16))
    yp = jnp.concatenate(
        [jnp.dot(p, wp_ref[gi], preferred_element_type=F32) for gi, p in enumerate(pooled)], axis=-1)

    ya = jnp.dot(a_ref[...], wo_ref[...], preferred_element_type=F32)
    yf = jnp.dot(f_ref[...], wf_ref[...], preferred_element_type=F32)
    y = (jax.nn.sigmoid(ga_ref[...].astype(F32)) * ya
         + jax.nn.sigmoid(gf_ref[...].astype(F32)) * yf
         + jax.nn.sigmoid(gp_ref[...].astype(F32)) * (yp * ps_ref[...]))
    mix = jnp.dot(y.astype(BF16), wout_ref[...], preferred_element_type=F32)
    o_ref[...] = x_ref[...] + gate_ref[...] * mix


def _merge_call(a, f, up, g, x, gate, pool_scale, w_attn_o, w_fourier, w_pool, w_out, l, tm):
    s, d = x.shape
    h = POOL_HALO
    n_halo = s // h
    row = pl.BlockSpec((1, d), lambda i: (0, 0))
    once = dict(pipeline_mode=pl.Buffered(1))
    return pl.pallas_call(
        functools.partial(_merge_kernel, s_len=s),
        grid=(s // tm,),
        in_specs=[
            pl.BlockSpec((tm, ATTN_W), lambda i: (i, 0)),
            pl.BlockSpec((tm, FOURIER_W), lambda i: (i, 0)),
            pl.BlockSpec((tm, POOL_W), lambda i: (i, 0)),
            pl.BlockSpec((h, POOL_W), lambda i: (jnp.maximum(i * (tm // h) - 1, 0), 0)),
            pl.BlockSpec((h, POOL_W), lambda i: (jnp.minimum((i + 1) * (tm // h), n_halo - 1), 0)),
            pl.BlockSpec((tm, d), lambda i: (i, 0)),
            pl.BlockSpec((tm, d), lambda i: (i, 1)),
            pl.BlockSpec((tm, d), lambda i: (i, 2)),
            pl.BlockSpec((tm, d), lambda i: (i, 0)),
            row, row,
            pl.BlockSpec((None, ATTN_W, d), lambda i: (l, 0, 0), **once),
            pl.BlockSpec((None, FOURIER_W, d), lambda i: (l, 0, 0), **once),
            pl.BlockSpec((None, len(POOL_WINDOWS), POOL_GROUP_W, POOL_OUT_GROUP_W),
                         lambda i: (l, 0, 0, 0), **once),
            pl.BlockSpec((None, d, d), lambda i: (l, 0, 0), **once),
        ],
        out_specs=pl.BlockSpec((tm, d), lambda i: (i, 0)),
        out_shape=jax.ShapeDtypeStruct((s, d), F32),
        scratch_shapes=[pltpu.VMEM((tm + 2 * h, POOL_W), F32)],
        compiler_params=_params("arbitrary"),
        name="merge",
    )(a, f, up, up, up, g, g, g, x, gate, pool_scale, w_attn_o, w_fourier, w_pool, w_out)


def _rope_tables(s_len):
    rows_n = s_len // GRID_W
    n_freq = D_HEAD // 4
    freqs = ROPE_BASE ** (-jnp.arange(n_freq, dtype=F32) / n_freq)
    ar = jnp.arange(rows_n).astype(F32)[:, None] * freqs
    ac = jnp.arange(GRID_W).astype(F32)[:, None] * freqs
    by_row = lambda t: jnp.repeat(t, GRID_W, axis=0)
    by_col = lambda t: jnp.tile(t, (rows_n, 1))
    cr, sr, cc, sc = by_row(jnp.cos(ar)), by_row(jnp.sin(ar)), by_col(jnp.cos(ac)), by_col(jnp.sin(ac))
    cos = jnp.concatenate([cr, cr, cc, cc], axis=-1)
    sin = jnp.concatenate([-sr, sr, -sc, sc], axis=-1)
    return cos, sin


def kernel(x, c, ctx, c_ctx, w_ada, b_ada, norm_w, ffn_w_gate, ffn_w_up, ffn_w_down,
           w_in, q_gain, k_gain, sink, w_attn_o, w_fourier, w_pool, pool_scale, w_out):
    b, s_len, d = x.shape
    c_len = ctx.shape[1]
    assert b == 1 and d == D_MODEL and s_len % (DFT_ROWS * BLOCK) == 0
    x, ctx = x[0], ctx[0]

    cond = jnp.zeros((8, d), F32).at[0].set(c[0]).at[1].set(c_ctx)
    mod = _ada_call(cond, w_ada, b_ada).reshape(DEPTH, 8, N_MOD, 1, d)

    ffn_f32 = (ffn_w_gate, ffn_w_up, ffn_w_down)
    ffn_w = (ffn_f32, 0, 0)
    w_ao, w_fo = w_attn_o.astype(BF16), w_fourier.astype(BF16)
    w_po, w_ou = w_pool.astype(BF16), w_out.astype(BF16)

    cos_x, sin_x = _rope_tables(s_len)
    cos_c, sin_c = jnp.ones((c_len, D_HEAD), F32), jnp.zeros((c_len, D_HEAD), F32)
    w1, g_x, cc, sc = (jnp.asarray(t).astype(BF16) for t in _dft_tables(s_len))
    _, g_c, cc_c, sc_c = (jnp.asarray(t).astype(BF16) for t in _dft_tables(c_len))

    tm_x, tm_m = ROW_TILE, MERGE_ROW_TILE
    for l in range(DEPTH):
        last = l == DEPTH - 1
        mx, mc = mod[l, 0], mod[l, 1]
        nw = norm_w[l][:, None, :]
        qg, kg = q_gain[l][None, :], k_gain[l][None, :]
        ps = pool_scale[l][None, :]

        x, ffn_w_post = _ffn_call(x, nw[0], mx[0], mx[1], mx[2], ffn_w, tm_x, (ffn_f32, l, 1))
        ctx, _ = _ffn_call(ctx, nw[0], mc[0], mc[1], mc[2], ffn_w, c_len)

        qx, kvx, ufx, upx, gx = _inproj_call(x, nw[1], mx[3], mx[4], qg, kg, cos_x, sin_x, w_in, l, tm_x)
        qc, kvc, ufc, upc, gc = _inproj_call(ctx, nw[1], mc[3], mc[4], qg, kg, cos_c, sin_c, w_in, l, c_len)

        ax = _window_attn_call(qx, kvx, kvc, sink[l])
        fx = _dft_seq_call(w1, g_x, ufx, cc, sc)
        x = _merge_call(ax, fx, upx, gx, x, mx[5], ps, w_ao, w_fo, w_po, w_ou, l, tm_m)
        if not last:
            ac = _ctx_attn_call(qc, kvc, sink[l])
            fc = _dft_small_call(g_c, ufc, cc_c, sc_c)
            ctx = _merge_call(ac, fc, upc, gc, ctx, mc[5], ps, w_ao, w_fo, w_po, w_ou, l, c_len)

        x, ffn_w_next = _ffn_call(x, nw[2], mx[6], mx[7], mx[8], ffn_w_post, tm_x,
                                  None if last else (ffn_f32, l + 1, 0))
        if not last:
            ctx, _ = _ffn_call(ctx, nw[2], mc[6], mc[7], mc[8], ffn_w_post, c_len)
        ffn_w = ffn_w_next
    return x[None]
```

```python
import functools

import numpy as np
import jax
import jax.numpy as jnp
from jax import lax
from jax.experimental import pallas as pl
from jax.experimental.pallas import tpu as pltpu

D_MODEL = 2048
DEPTH = 4
GRID_W = 64
N_HEADS = 8
N_KV_HEADS = 2
GROUP = N_HEADS // N_KV_HEADS
D_HEAD = 128
ATTN_W = N_HEADS * D_HEAD
KV_W = N_KV_HEADS * D_HEAD
WINDOW = 128
BLOCK = 128
FOURIER_W = D_MODEL // 4
N_FOURIER_GROUPS = 4
FOURIER_GROUP_W = FOURIER_W // N_FOURIER_GROUPS
POOL_WINDOWS = (2, 4, 8, 16)
POOL_W = D_MODEL // 4
POOL_GROUP_W = POOL_W // len(POOL_WINDOWS)
POOL_OUT_GROUP_W = D_MODEL // len(POOL_WINDOWS)
N_BRANCHES = 3
GATE_W = N_BRANCHES * D_MODEL
IN_W = ATTN_W + 2 * KV_W + FOURIER_W + POOL_W + GATE_W
D_FF = 5632
N_MOD = 9
ROPE_BASE = 10000.0
EPS = 1e-6
NEG_INF = -1e30

BF16 = jnp.bfloat16
F32 = jnp.float32

VMEM_LIMIT_BYTES = 60 * 1024 * 1024
ROW_TILE = 1024
MERGE_ROW_TILE = 256
ADA_TILE_N = 1024
POOL_HALO = 16
COL_TILE = 512
FFN_TILE_F = 512
LOG2_E = 1.4426950408889634
Q_SCALE = LOG2_E * D_HEAD ** -0.5
DFT_ROWS = GRID_W
DFT_S2_GROUP = 16


def _params(*semantics):
    return pltpu.CompilerParams(dimension_semantics=semantics, vmem_limit_bytes=VMEM_LIMIT_BYTES)


def _modulate(x, gain, shift, scale):
    y = x * lax.rsqrt(jnp.mean(x * x, axis=-1, keepdims=True) + EPS)
    return (y * gain) * (1 + scale) + shift


def _ada_kernel(cond_ref, w_ref, b_ref, o_ref):
    h = jax.nn.silu(cond_ref[...]).astype(BF16)
    o_ref[0] = jnp.dot(h, w_ref[0].astype(BF16), preferred_element_type=F32) + b_ref[0]


def _ada_call(cond, w_ada, b_ada):
    depth, d, n = w_ada.shape
    tn = ADA_TILE_N
    return pl.pallas_call(
        _ada_kernel,
        grid=(depth, n // tn),
        in_specs=[
            pl.BlockSpec((8, d), lambda l, j: (0, 0)),
            pl.BlockSpec((1, d, tn), lambda l, j: (l, 0, j)),
            pl.BlockSpec((1, 1, tn), lambda l, j: (l, 0, j)),
        ],
        out_specs=pl.BlockSpec((1, 8, tn), lambda l, j: (l, 0, j)),
        out_shape=jax.ShapeDtypeStruct((depth, 8, n), F32),
        compiler_params=_params("arbitrary", "arbitrary"),
        name="adaln",
    )(cond, w_ada, b_ada.reshape(depth, 1, n))


def _ffn_kernel(*refs, fused_gate_up, convert_next):
    refs = list(refs)
    take = lambda n: [refs.pop(0) for _ in range(n)]
    x_hbm, nw_ref, sh_ref, sc_ref, gt_ref = take(5)
    if fused_gate_up:
        (wgu_ref,) = take(1)
    else:
        wg_ref, wu_ref = take(2)
    (wd_ref,) = take(1)
    ng_ref, nu_ref, nd_ref = take(3) if convert_next else (None,) * 3
    (o_hbm,) = take(1)
    cgu_ref, cd_ref = take(2) if convert_next else (None,) * 2
    xs_ref, acc_ref, h_ref, sem = take(4)
    i, j = pl.program_id(0), pl.program_id(1)
    n_tiles, n_ff = pl.num_programs(0), pl.num_programs(1)
    tm = xs_ref.shape[0]

    def x_copy(tile):
        return pltpu.make_async_copy(x_hbm.at[pl.ds(tile * tm, tm)], xs_ref, sem.at[0])

    def o_copy(tile):
        return pltpu.make_async_copy(acc_ref, o_hbm.at[pl.ds(tile * tm, tm)], sem.at[1])

    def gate_up():
        h = h_ref[...]
        if fused_gate_up:
            gu = jnp.dot(h, wgu_ref[...], preferred_element_type=F32)
            tf = gu.shape[1] // 2
            g, u = gu[:, :tf], gu[:, tf:]
        else:
            g = jnp.dot(h, wg_ref[...].astype(BF16), preferred_element_type=F32)
            u = jnp.dot(h, wu_ref[...].astype(BF16), preferred_element_type=F32)
        return (jax.nn.silu(g) * u).astype(BF16)

    def down(a):
        return (0.5 * gt_ref[...]) * jnp.dot(a, wd_ref[...].astype(BF16), preferred_element_type=F32)

    def convert():
        if convert_next:
            tf = ng_ref.shape[1]
            cgu_ref[:, :tf] = ng_ref[...].astype(BF16)
            cgu_ref[:, tf:] = nu_ref[...].astype(BF16)
            cd_ref[...] = nd_ref[...].astype(BF16)

    @pl.when(j == 0)
    def _():
        @pl.when(i == 0)
        def _():
            x_copy(0).start()

        x_copy(i).wait()
        h_ref[...] = _modulate(xs_ref[...], nw_ref[...], sh_ref[...], sc_ref[...]).astype(BF16)
        a = gate_up()
        convert()

        @pl.when(i > 0)
        def _():
            o_copy(i - 1).wait()

        acc_ref[...] = xs_ref[...] + down(a)

        @pl.when(i + 1 < n_tiles)
        def _():
            x_copy(i + 1).start()

    @pl.when(j > 0)
    def _():
        acc_ref[...] += down(gate_up())
        convert()

    @pl.when(j == n_ff - 1)
    def _():
        o_copy(i).start()

        @pl.when(i == n_tiles - 1)
        def _():
            o_copy(i).wait()


def _ffn_call(x, nw, shift, scale, gate, w, tm, next_f32=None):
    s, d = x.shape
    n_tiles = s // tm
    tf = FFN_TILE_F
    row = pl.BlockSpec((1, d), lambda i, j: (0, 0))
    fused = not isinstance(w[0], tuple)
    if fused:
        w_args = list(w)
        w_specs = [pl.BlockSpec((None, d, 2 * tf), lambda i, j: (j, 0, 0)),
                   pl.BlockSpec((tf, d), lambda i, j: (j, 0))]
    else:
        w_args, l0, w0 = w
        w_args = list(w_args)
        w_specs = [pl.BlockSpec((None, None, d, tf), lambda i, j: (l0, w0, 0, j)),
                   pl.BlockSpec((None, None, d, tf), lambda i, j: (l0, w0, 0, j)),
                   pl.BlockSpec((None, None, tf, d), lambda i, j: (l0, w0, j, 0))]
    in_specs = [pl.BlockSpec(memory_space=pl.ANY), row, row, row, row, *w_specs]
    out_specs = [pl.BlockSpec(memory_space=pl.ANY)]
    out_shape = [jax.ShapeDtypeStruct((s, d), F32)]
    args = [x, nw, shift, scale, gate, *w_args]
    if next_f32 is not None:
        (ng, nu, nd), l2, w2 = next_f32
        n_ff = D_FF // tf
        rb = d // n_tiles
        rd = D_FF // (n_tiles * n_ff)
        assert rb * n_tiles == d and rd * n_tiles * n_ff == D_FF
        in_specs += [
            pl.BlockSpec((None, None, rb, tf), lambda i, j: (l2, w2, i, j)),
            pl.BlockSpec((None, None, rb, tf), lambda i, j: (l2, w2, i, j)),
            pl.BlockSpec((None, None, rd, d), lambda i, j: (l2, w2, i * n_ff + j, 0)),
        ]
        out_specs += [
            pl.BlockSpec((None, rb, 2 * tf), lambda i, j: (j, i, 0)),
            pl.BlockSpec((rd, d), lambda i, j: (i * n_ff + j, 0)),
        ]
        out_shape += [jax.ShapeDtypeStruct((n_ff, d, 2 * tf), BF16), jax.ShapeDtypeStruct((D_FF, d), BF16)]
        args += [ng, nu, nd]
    outs = pl.pallas_call(
        functools.partial(_ffn_kernel, fused_gate_up=fused, convert_next=next_f32 is not None),
        grid=(n_tiles, D_FF // tf),
        in_specs=in_specs,
        out_specs=out_specs,
        out_shape=out_shape,
        scratch_shapes=[pltpu.VMEM((tm, d), F32), pltpu.VMEM((tm, d), F32), pltpu.VMEM((tm, d), BF16),
                        pltpu.SemaphoreType.DMA((2,))],
        compiler_params=_params("arbitrary", "arbitrary"),
        name="ffn",
    )(*args)
    return outs[0], (tuple(outs[1:]) if next_f32 is not None else None)


Q_TILES = ATTN_W // COL_TILE
KV_TILE = Q_TILES
UF_TILE = KV_TILE + 1
UP_TILE = UF_TILE + 1
G_TILE0 = UP_TILE + 1
N_COL_TILES = IN_W // COL_TILE


def _norm_rope(z, gain, cos, sin_signed):
    y = z * lax.rsqrt(jnp.mean(z * z, axis=-1, keepdims=True) + EPS) * gain
    lane = lax.broadcasted_iota(jnp.int32, y.shape, 1)
    first_half = (lane & (D_HEAD // 4)) == 0
    partner = jnp.where(first_half, pltpu.roll(y, D_HEAD - D_HEAD // 4, 1), pltpu.roll(y, D_HEAD // 4, 1))
    return y * cos + partner * sin_signed


STEP_Q, STEP_KV_UF, STEP_UP, STEP_G0 = 0, 1, 2, 3


def _inproj_kernel(x_ref, nw_ref, sh_ref, sc_ref, qg_ref, kg_ref, cos_ref, sin_ref, w_ref, w2_ref,
                   q_ref, kv_ref, uf_ref, up_ref, g_ref, h_ref, z_ref):
    j = pl.program_id(1)

    def proj(w):
        return jnp.dot(h_ref[...], w[...].astype(BF16), preferred_element_type=F32)

    def q_epilogue(z, col0):
        cos, sin = cos_ref[...], sin_ref[...]
        for hh in range(COL_TILE // D_HEAD):
            sl = slice(hh * D_HEAD, (hh + 1) * D_HEAD)
            q = _norm_rope(z[:, sl], qg_ref[...], cos, sin) * Q_SCALE
            q_ref[:, col0 + hh * D_HEAD:col0 + (hh + 1) * D_HEAD] = q.astype(BF16)

    def kv_epilogue(z):
        cos, sin = cos_ref[...], sin_ref[...]
        for hh in range(N_KV_HEADS):
            sl = slice(hh * D_HEAD, (hh + 1) * D_HEAD)
            kv_ref[:, sl] = _norm_rope(z[:, sl], kg_ref[...], cos, sin).astype(BF16)
        kv_ref[:, KV_W:] = z[:, KV_W:].astype(BF16)

    @pl.when(j == STEP_Q)
    def _():
        h_ref[...] = _modulate(x_ref[...], nw_ref[...], sh_ref[...], sc_ref[...]).astype(BF16)
        z_ref[0] = proj(w_ref)
        z_ref[1] = proj(w2_ref)

    @pl.when(j == STEP_KV_UF)
    def _():
        z_ref[2] = proj(w_ref)
        uf_ref[...] = proj(w2_ref).astype(BF16)
        for t in range(Q_TILES):
            q_epilogue(z_ref[t], t * COL_TILE)

    @pl.when(j == STEP_UP)
    def _():
        up_ref[...] = proj(w_ref).astype(BF16)
        kv_epilogue(z_ref[2])

    @pl.when(j >= STEP_G0)
    def _():
        g_ref[:, :COL_TILE] = proj(w_ref).astype(BF16)
        g_ref[:, COL_TILE:] = proj(w2_ref).astype(BF16)


def _inproj_call(x, nw, shift, scale, qg, kg, cos, sin, w_in, l, tm):
    s, d = x.shape
    ct = COL_TILE
    assert (Q_TILES, KV_TILE, UF_TILE, UP_TILE, G_TILE0) == (2, 2, 3, 4, 5) and (N_COL_TILES - G_TILE0) % 2 == 0
    row = pl.BlockSpec((1, d), lambda i, j: (0, 0))
    hrow = pl.BlockSpec((1, D_HEAD), lambda i, j: (0, 0))
    tab = pl.BlockSpec((tm, D_HEAD), lambda i, j: (i, 0))
    n_steps = STEP_G0 + (N_COL_TILES - G_TILE0) // 2
    tile_a = lambda j: 2 * j - (j >= STEP_G0).astype(jnp.int32)
    tile_b = lambda j: jnp.where(j < STEP_UP, 2 * j + 1, jnp.maximum(2 * j, G_TILE0 + 1))
    return pl.pallas_call(
        _inproj_kernel,
        grid=(s // tm, n_steps),
        in_specs=[
            pl.BlockSpec((tm, d), lambda i, j: (i, 0)),
            row, row, row, hrow, hrow, tab, tab,
            pl.BlockSpec((None, d, ct), lambda i, j: (l, 0, tile_a(j))),
            pl.BlockSpec((None, d, ct), lambda i, j: (l, 0, tile_b(j))),
        ],
        out_specs=[
            pl.BlockSpec((tm, Q_TILES * ct), lambda i, j: (i, 0)),
            pl.BlockSpec((tm, ct), lambda i, j: (i, 0)),
            pl.BlockSpec((tm, ct), lambda i, j: (i, 0)),
            pl.BlockSpec((tm, ct), lambda i, j: (i, 0)),
            pl.BlockSpec((tm, 2 * ct), lambda i, j: (i, jnp.maximum(j - STEP_G0, 0))),
        ],
        out_shape=[
            jax.ShapeDtypeStruct((s, ATTN_W), BF16),
            jax.ShapeDtypeStruct((s, 2 * KV_W), BF16),
            jax.ShapeDtypeStruct((s, FOURIER_W), BF16),
            jax.ShapeDtypeStruct((s, POOL_W), BF16),
            jax.ShapeDtypeStruct((s, GATE_W), BF16),
        ],
        scratch_shapes=[pltpu.VMEM((tm, d), BF16), pltpu.VMEM((3, tm, ct), F32)],
        compiler_params=_params("arbitrary", "arbitrary"),
        name="inproj",
    )(x, nw, shift, scale, qg, kg, cos, sin, w_in, w_in)


def _qk(q, k):
    return lax.dot_general(q, k, (((1,), (1,)), ((), ())), preferred_element_type=F32)


def _k_cols(kvh):
    return slice(kvh * D_HEAD, (kvh + 1) * D_HEAD)


def _v_cols(kvh):
    return slice(KV_W + kvh * D_HEAD, KV_W + (kvh + 1) * D_HEAD)


def _sink_column(sink_ref, kvh, rows):
    head = lax.broadcasted_iota(jnp.int32, (GROUP * rows, 1), 0) // rows
    col = jnp.full((GROUP * rows, 1), sink_ref[kvh * GROUP], F32)
    for hh in range(1, GROUP):
        col = jnp.where(head == hh, sink_ref[kvh * GROUP + hh], col)
    return col * LOG2_E


def _stack_heads(q_ref, kvh):
    heads = range(kvh * GROUP, (kvh + 1) * GROUP)
    return jnp.concatenate([q_ref[:, hh * D_HEAD:(hh + 1) * D_HEAD] for hh in heads], axis=0)


def _softmax_pv(scores, values, sink_col):
    def lane_tiles(a):
        return [a[:, t:t + D_HEAD] for t in range(0, a.shape[1], D_HEAD)]

    m_tile = functools.reduce(jnp.maximum, [t for s in scores for t in lane_tiles(s)])
    m = jnp.maximum(sink_col, jnp.max(m_tile, axis=-1, keepdims=True))
    e_tile = None
    acc = None
    for s, v in zip(scores, values):
        e = jnp.exp2(s - m)
        for t in lane_tiles(e):
            e_tile = t if e_tile is None else e_tile + t
        pv = jnp.dot(e.astype(BF16), v, preferred_element_type=F32)
        acc = pv if acc is None else acc + pv
    denom = jnp.exp2(sink_col - m) + jnp.sum(e_tile, axis=-1, keepdims=True)
    return acc / denom


def _unstack_heads(out, o_ref, kvh, rows):
    for g in range(GROUP):
        hh = kvh * GROUP + g
        o_ref[:, hh * D_HEAD:(hh + 1) * D_HEAD] = out[g * rows:(g + 1) * rows].astype(o_ref.dtype)


ATTN_Q_BLOCKS = 4


def _window_attn_kernel(sink_ref, q_ref, *refs):
    kv_refs, kvx_ref, o_ref = refs[:-2], refs[-2], refs[-1]
    i = pl.program_id(0)
    n_steps = pl.num_programs(0)
    shape = (GROUP * BLOCK, BLOCK)
    qi = lax.broadcasted_iota(jnp.int32, shape, 0) % BLOCK
    kj = lax.broadcasted_iota(jnp.int32, shape, 1)
    for b in range(ATTN_Q_BLOCKS):
        kvp_ref, kvc_ref, kvn_ref = kv_refs[b:b + 3]
        keep_prev = (kj >= qi) if b > 0 else (kj >= qi) & (i > 0)
        keep_next = (kj <= qi) if b < ATTN_Q_BLOCKS - 1 else (kj <= qi) & (i < n_steps - 1)
        rows = slice(b * BLOCK, (b + 1) * BLOCK)
        for kvh in range(N_KV_HEADS):
            q = _stack_heads(q_ref.at[rows], kvh)
            ks, vs = _k_cols(kvh), _v_cols(kvh)
            s_prev = jnp.where(keep_prev, _qk(q, kvp_ref[:, ks]), NEG_INF)
            s_cur = _qk(q, kvc_ref[:, ks])
            s_next = jnp.where(keep_next, _qk(q, kvn_ref[:, ks]), NEG_INF)
            s_ctx = _qk(q, kvx_ref[:, ks])
            out = _softmax_pv([s_prev, s_cur, s_next, s_ctx],
                              [kvp_ref[:, vs], kvc_ref[:, vs], kvn_ref[:, vs], kvx_ref[:, vs]],
                              _sink_column(sink_ref, kvh, BLOCK))
            _unstack_heads(out, o_ref.at[rows], kvh, BLOCK)


def _window_attn_call(q, kv, kv_ctx, sink):
    s = q.shape[0]
    c = kv_ctx.shape[0]
    nb = s // BLOCK
    qb = ATTN_Q_BLOCKS
    kblk = lambda off: pl.BlockSpec((BLOCK, 2 * KV_W), lambda i: (jnp.clip(i * qb + off, 0, nb - 1), 0))
    return pl.pallas_call(
        _window_attn_kernel,
        grid=(nb // qb,),
        in_specs=[
            pl.BlockSpec(memory_space=pltpu.SMEM),
            pl.BlockSpec((qb * BLOCK, ATTN_W), lambda i: (i, 0)),
            *[kblk(off) for off in range(-1, qb + 1)],
            pl.BlockSpec((c, 2 * KV_W), lambda i: (0, 0)),
        ],
        out_specs=pl.BlockSpec((qb * BLOCK, ATTN_W), lambda i: (i, 0)),
        out_shape=jax.ShapeDtypeStruct((s, ATTN_W), BF16),
        compiler_params=_params("arbitrary"),
        name="window_attn",
    )(sink, q, *([kv] * (qb + 2)), kv_ctx)


def _ctx_attn_kernel(sink_ref, q_ref, kv_ref, o_ref):
    rows = q_ref.shape[0]
    for kvh in range(N_KV_HEADS):
        q = _stack_heads(q_ref, kvh)
        out = _softmax_pv([_qk(q, kv_ref[:, _k_cols(kvh)])], [kv_ref[:, _v_cols(kvh)]],
                          _sink_column(sink_ref, kvh, rows))
        _unstack_heads(out, o_ref, kvh, rows)


def _ctx_attn_call(q, kv, sink):
    c = q.shape[0]
    return pl.pallas_call(
        _ctx_attn_kernel,
        grid=(1,),
        in_specs=[
            pl.BlockSpec(memory_space=pltpu.SMEM),
            pl.BlockSpec((c, ATTN_W), lambda i: (0, 0)),
            pl.BlockSpec((c, 2 * KV_W), lambda i: (0, 0)),
        ],
        out_specs=pl.BlockSpec((c, ATTN_W), lambda i: (0, 0)),
        out_shape=jax.ShapeDtypeStruct((c, ATTN_W), BF16),
        compiler_params=_params("arbitrary"),
        name="ctx_attn",
    )(sink, q, kv)


def _dft_tables(s_len):
    n1 = DFT_ROWS if s_len > 1024 else 1
    n2 = s_len // n1
    k1 = np.arange(n1)
    ang1 = 2.0 * np.pi * ((k1[:, None] * k1[None, :]) % n1) / n1
    w1 = np.concatenate([np.cos(ang1), -np.sin(ang1)], axis=0)
    w1 = np.kron(w1, np.eye(DFT_S2_GROUP))
    k2 = np.arange(n2)
    phase = (k2[None, None, :] * (k1[:, None, None] + n1 * k2[None, :, None])) % s_len
    ang2 = 2.0 * np.pi * phase / s_len
    gr, gi = np.cos(ang2), -np.sin(ang2)
    g = np.concatenate([np.concatenate([gr, -gi], axis=2),
                        np.concatenate([gi, gr], axis=2)], axis=1)
    if n1 == 1:
        g = g[:, :, :n2]
    c = np.arange(FOURIER_GROUP_W)
    angc = 2.0 * np.pi * ((c[:, None] * c[None, :]) % FOURIER_GROUP_W) / FOURIER_GROUP_W
    norm = 1.0 / np.sqrt(float(s_len) * FOURIER_GROUP_W)
    return (w1.astype(np.float32), g.astype(np.float32),
            (np.cos(angc) * norm).astype(np.float32), (np.sin(angc) * norm).astype(np.float32))


DFT_K1_GROUP = 8


def _dft_cols(g, x, cc_ref, sc_ref, o_ref, col0):
    y = jnp.dot(g, x, preferred_element_type=F32)
    n = g.shape[0] // 2
    yr, yi = y[:n].astype(BF16), y[n:].astype(BF16)
    for gi in range(N_FOURIER_GROUPS):
        sl = slice(gi * FOURIER_GROUP_W, (gi + 1) * FOURIER_GROUP_W)
        o = (jnp.dot(yr[:, sl], cc_ref[...], preferred_element_type=F32)
             + jnp.dot(yi[:, sl], sc_ref[...], preferred_element_type=F32))
        o_ref[:, col0 + gi * FOURIER_GROUP_W:col0 + (gi + 1) * FOURIER_GROUP_W] = o.astype(o_ref.dtype)


def _dft_seq_kernel(w_ref, u_ref, g_ref, cc_ref, sc_ref, o_ref, x1_ref):
    j = pl.program_id(0)
    n1, grp, cw = u_ref.shape
    n_a = x1_ref.shape[1] // grp
    kb = g_ref.shape[0]

    @pl.when(j < n_a)
    def _():
        y = jnp.dot(w_ref[...], u_ref[...].reshape(n1 * grp, cw), preferred_element_type=F32)
        s2 = pl.multiple_of(j * grp, grp)
        x1_ref[:, pl.ds(s2, grp), :] = y.astype(BF16).reshape(2 * n1, grp, cw)

    @pl.when(j >= n_a)
    def _():
        k0 = (j - n_a) * kb
        for b in range(kb):
            x = jnp.concatenate([x1_ref[k0 + b], x1_ref[n1 + k0 + b]], axis=0)
            _dft_cols(g_ref[b], x, cc_ref, sc_ref, o_ref, b * cw)


def _dft_seq_call(w1k, g, uf, cc, sc):
    s, cw = uf.shape
    n1 = DFT_ROWS
    n2 = s // n1
    grp, kb, gw = DFT_S2_GROUP, DFT_K1_GROUP, FOURIER_GROUP_W
    n_a, n_b = n2 // grp, n1 // kb
    stage2 = lambda j: jnp.maximum(j - n_a, 0)
    out = pl.pallas_call(
        _dft_seq_kernel,
        grid=(n_a + n_b,),
        in_specs=[pl.BlockSpec((2 * n1 * grp, n1 * grp), lambda j: (0, 0)),
                  pl.BlockSpec((n1, grp, cw), lambda j: (0, jnp.minimum(j, n_a - 1), 0)),
                  pl.BlockSpec((kb, 2 * n2, 2 * n2), lambda j: (stage2(j), 0, 0)),
                  pl.BlockSpec((gw, gw), lambda j: (0, 0)),
                  pl.BlockSpec((gw, gw), lambda j: (0, 0))],
        out_specs=pl.BlockSpec((n2, kb * cw), lambda j: (0, stage2(j))),
        out_shape=jax.ShapeDtypeStruct((n2, n1 * cw), BF16),
        scratch_shapes=[pltpu.VMEM((2 * n1, n2, cw), BF16)],
        compiler_params=_params("arbitrary"),
        name="dft_seq",
    )(w1k, uf.reshape(n1, n2, cw), g, cc, sc)
    return out.reshape(s, cw)


def _dft_ctx_kernel(g_ref, u_ref, cc_ref, sc_ref, o_ref):
    _dft_cols(g_ref[0], u_ref[...], cc_ref, sc_ref, o_ref, 0)


def _dft_small_call(g, uf, cc, sc):
    c, cw = uf.shape
    gw = FOURIER_GROUP_W
    return pl.pallas_call(
        _dft_ctx_kernel,
        grid=(1,),
        in_specs=[pl.BlockSpec((1, 2 * c, c), lambda k: (0, 0, 0)),
                  pl.BlockSpec((c, cw), lambda k: (0, 0)),
                  pl.BlockSpec((gw, gw), lambda k: (0, 0)),
                  pl.BlockSpec((gw, gw), lambda k: (0, 0))],
        out_specs=pl.BlockSpec((c, cw), lambda k: (0, 0)),
        out_shape=jax.ShapeDtypeStruct((c, cw), BF16),
        compiler_params=_params("arbitrary"),
        name="dft_ctx",
    )(g, uf, cc, sc)


def _merge_kernel(a_ref, f_ref, up_ref, upp_ref, upn_ref, ga_ref, gf_ref, gp_ref, x_ref,
                  gate_ref, ps_ref, wo_ref, wf_ref, wp_ref, wout_ref, o_ref, ext_ref, *, s_len):
    i = pl.program_id(0)
    tm = up_ref.shape[0]
    h = POOL_HALO
    ext_ref[0:h] = jnp.where(i > 0, upp_ref[...].astype(F32), 0.0)
    ext_ref[h:h + tm] = up_ref[...].astype(F32)
    ext_ref[h + tm:] = jnp.where(i < pl.num_programs(0) - 1, upn_ref[...].astype(F32), 0.0)

    t = i * tm + lax.broadcasted_iota(jnp.int32, (tm, 1), 0)
    pooled = []
    for gi, w in enumerate(POOL_WINDOWS):
        sl = slice(gi * POOL_GROUP_W, (gi + 1) * POOL_GROUP_W)
        total = ext_ref[h - w // 2:h - w // 2 + tm, sl]
        for off in range(-(w // 2) + 1, w - w // 2):
            total = total + ext_ref[h + off:h + off + tm, sl]
        count = jnp.minimum(t + (w - w // 2), s_len) - jnp.maximum(t - w // 2, 0)
        u = ext_ref[h:h + tm, sl]
        pooled.append((total / count.astype(F32) - u).astype(BF16))
    yp = jnp.concatenate(
        [jnp.dot(p, wp_ref[gi], preferred_element_type=F32) for gi, p in enumerate(pooled)], axis=-1)

    ya = jnp.dot(a_ref[...], wo_ref[...], preferred_element_type=F32)
    yf = jnp.dot(f_ref[...], wf_ref[...], preferred_element_type=F32)
    y = (jax.nn.sigmoid(ga_ref[...].astype(F32)) * ya
         + jax.nn.sigmoid(gf_ref[...].astype(F32)) * yf
         + jax.nn.sigmoid(gp_ref[...].astype(F32)) * (yp * ps_ref[...]))
    mix = jnp.dot(y.astype(BF16), wout_ref[...], preferred_element_type=F32)
    o_ref[...] = x_ref[...] + gate_ref[...] * mix


def _merge_call(a, f, up, g, x, gate, pool_scale, w_attn_o, w_fourier, w_pool, w_out, l, tm):
    s, d = x.shape
    h = POOL_HALO
    n_halo = s // h
    row = pl.BlockSpec((1, d), lambda i: (0, 0))
    once = dict(pipeline_mode=pl.Buffered(1))
    return pl.pallas_call(
        functools.partial(_merge_kernel, s_len=s),
        grid=(s // tm,),
        in_specs=[
            pl.BlockSpec((tm, ATTN_W), lambda i: (i, 0)),
            pl.BlockSpec((tm, FOURIER_W), lambda i: (i, 0)),
            pl.BlockSpec((tm, POOL_W), lambda i: (i, 0)),
            pl.BlockSpec((h, POOL_W), lambda i: (jnp.maximum(i * (tm // h) - 1, 0), 0)),
            pl.BlockSpec((h, POOL_W), lambda i: (jnp.minimum((i + 1) * (tm // h), n_halo - 1), 0)),
            pl.BlockSpec((tm, d), lambda i: (i, 0)),
            pl.BlockSpec((tm, d), lambda i: (i, 1)),
            pl.BlockSpec((tm, d), lambda i: (i, 2)),
            pl.BlockSpec((tm, d), lambda i: (i, 0)),
            row, row,
            pl.BlockSpec((None, ATTN_W, d), lambda i: (l, 0, 0), **once),
            pl.BlockSpec((None, FOURIER_W, d), lambda i: (l, 0, 0), **once),
            pl.BlockSpec((None, len(POOL_WINDOWS), POOL_GROUP_W, POOL_OUT_GROUP_W),
                         lambda i: (l, 0, 0, 0), **once),
            pl.BlockSpec((None, d, d), lambda i: (l, 0, 0), **once),
        ],
        out_specs=pl.BlockSpec((tm, d), lambda i: (i, 0)),
        out_shape=jax.ShapeDtypeStruct((s, d), F32),
        scratch_shapes=[pltpu.VMEM((tm + 2 * h, POOL_W), F32)],
        compiler_params=_params("arbitrary"),
        name="merge",
    )(a, f, up, up, up, g, g, g, x, gate, pool_scale, w_attn_o, w_fourier, w_pool, w_out)


def _rope_tables(s_len):
    rows_n = s_len // GRID_W
    n_freq = D_HEAD // 4
    freqs = ROPE_BASE ** (-jnp.arange(n_freq, dtype=F32) / n_freq)
    ar = jnp.arange(rows_n).astype(F32)[:, None] * freqs
    ac = jnp.arange(GRID_W).astype(F32)[:, None] * freqs
    by_row = lambda t: jnp.repeat(t, GRID_W, axis=0)
    by_col = lambda t: jnp.tile(t, (rows_n, 1))
    cr, sr, cc, sc = by_row(jnp.cos(ar)), by_row(jnp.sin(ar)), by_col(jnp.cos(ac)), by_col(jnp.sin(ac))
    cos = jnp.concatenate([cr, cr, cc, cc], axis=-1)
    sin = jnp.concatenate([-sr, sr, -sc, sc], axis=-1)
    return cos, sin


def kernel(x, c, ctx, c_ctx, w_ada, b_ada, norm_w, ffn_w_gate, ffn_w_up, ffn_w_down,
           w_in, q_gain, k_gain, sink, w_attn_o, w_fourier, w_pool, pool_scale, w_out):
    b, s_len, d = x.shape
    c_len = ctx.shape[1]
    assert b == 1 and d == D_MODEL and s_len % (DFT_ROWS * BLOCK) == 0
    x, ctx = x[0], ctx[0]

    cond = jnp.zeros((8, d), F32).at[0].set(c[0]).at[1].set(c_ctx)
    mod = _ada_call(cond, w_ada, b_ada).reshape(DEPTH, 8, N_MOD, 1, d)

    ffn_f32 = (ffn_w_gate, ffn_w_up, ffn_w_down)
    ffn_w = (ffn_f32, 0, 0)
    w_ao, w_fo = w_attn_o.astype(BF16), w_fourier.astype(BF16)
    w_po, w_ou = w_pool.astype(BF16), w_out.astype(BF16)

    cos_x, sin_x = _rope_tables(s_len)
    cos_c, sin_c = jnp.ones((c_len, D_HEAD), F32), jnp.zeros((c_len, D_HEAD), F32)
    w1, g_x, cc, sc = (jnp.asarray(t).astype(BF16) for t in _dft_tables(s_len))
    _, g_c, cc_c, sc_c = (jnp.asarray(t).astype(BF16) for t in _dft_tables(c_len))

    tm_x, tm_m = ROW_TILE, MERGE_ROW_TILE
    for l in range(DEPTH):
        last = l == DEPTH - 1
        mx, mc = mod[l, 0], mod[l, 1]
        nw = norm_w[l][:, None, :]
        qg, kg = q_gain[l][None, :], k_gain[l][None, :]
        ps = pool_scale[l][None, :]

        x, ffn_w_post = _ffn_call(x, nw[0], mx[0], mx[1], mx[2], ffn_w, tm_x, (ffn_f32, l, 1))
        ctx, _ = _ffn_call(ctx, nw[0], mc[0], mc[1], mc[2], ffn_w, c_len)

        qx, kvx, ufx, upx, gx = _inproj_call(x, nw[1], mx[3], mx[4], qg, kg, cos_x, sin_x, w_in, l, tm_x)
        qc, kvc, ufc, upc, gc = _inproj_call(ctx, nw[1], mc[3], mc[4], qg, kg, cos_c, sin_c, w_in, l, c_len)

        ax = _window_attn_call(qx, kvx, kvc, sink[l])
        fx = _dft_seq_call(w1, g_x, ufx, cc, sc)
        x = _merge_call(ax, fx, upx, gx, x, mx[5], ps, w_ao, w_fo, w_po, w_ou, l, tm_m)
        if not last:
            ac = _ctx_attn_call(qc, kvc, sink[l])
            fc = _dft_small_call(g_c, ufc, cc_c, sc_c)
            ctx = _merge_call(ac, fc, upc, gc, ctx, mc[5], ps, w_ao, w_fo, w_po, w_ou, l, c_len)

        x, ffn_w_next = _ffn_call(x, nw[2], mx[6], mx[7], mx[8], ffn_w_post, tm_x,
                                  None if last else (ffn_f32, l + 1, 0))
        if not last:
            ctx, _ = _ffn_call(ctx, nw[2], mc[6], mc[7], mc[8], ffn_w_post, c_len)
        ffn_w = ffn_w_next
    return x[None]
```

```python
import functools

import numpy as np
import jax
import jax.numpy as jnp
from jax import lax
from jax.experimental import pallas as pl
from jax.experimental.pallas import tpu as pltpu

D_MODEL = 2048
DEPTH = 4
GRID_W = 64
N_HEADS = 8
N_KV_HEADS = 2
GROUP = N_HEADS // N_KV_HEADS
D_HEAD = 128
ATTN_W = N_HEADS * D_HEAD
KV_W = N_KV_HEADS * D_HEAD
WINDOW = 128
BLOCK = 128
FOURIER_W = D_MODEL // 4
N_FOURIER_GROUPS = 4
FOURIER_GROUP_W = FOURIER_W // N_FOURIER_GROUPS
POOL_WINDOWS = (2, 4, 8, 16)
POOL_W = D_MODEL // 4
POOL_GROUP_W = POOL_W // len(POOL_WINDOWS)
POOL_OUT_GROUP_W = D_MODEL // len(POOL_WINDOWS)
N_BRANCHES = 3
GATE_W = N_BRANCHES * D_MODEL
IN_W = ATTN_W + 2 * KV_W + FOURIER_W + POOL_W + GATE_W
D_FF = 5632
N_MOD = 9
ROPE_BASE = 10000.0
EPS = 1e-6
NEG_INF = -1e30

BF16 = jnp.bfloat16
F32 = jnp.float32

VMEM_LIMIT_BYTES = 60 * 1024 * 1024
ROW_TILE = 1024
MERGE_ROW_TILE = 256
ADA_TILE_N = 1024
POOL_HALO = 16
COL_TILE = 512
FFN_TILE_F = 512
LOG2_E = 1.4426950408889634
Q_SCALE = LOG2_E * D_HEAD ** -0.5
DFT_ROWS = GRID_W
DFT_S2_GROUP = 16


def _params(*semantics):
    return pltpu.CompilerParams(dimension_semantics=semantics, vmem_limit_bytes=VMEM_LIMIT_BYTES)


def _modulate(x, gain, shift, scale):
    y = x * lax.rsqrt(jnp.mean(x * x, axis=-1, keepdims=True) + EPS)
    return (y * gain) * (1 + scale) + shift


def _ada_kernel(cond_ref, w_ref, b_ref, o_ref):
    h = jax.nn.silu(cond_ref[...]).astype(BF16)
    o_ref[0] = jnp.dot(h, w_ref[0].astype(BF16), preferred_element_type=F32) + b_ref[0]


def _ada_call(cond, w_ada, b_ada):
    depth, d, n = w_ada.shape
    tn = ADA_TILE_N
    return pl.pallas_call(
        _ada_kernel,
        grid=(depth, n // tn),
        in_specs=[
            pl.BlockSpec((8, d), lambda l, j: (0, 0)),
            pl.BlockSpec((1, d, tn), lambda l, j: (l, 0, j)),
            pl.BlockSpec((1, 1, tn), lambda l, j: (l, 0, j)),
        ],
        out_specs=pl.BlockSpec((1, 8, tn), lambda l, j: (l, 0, j)),
        out_shape=jax.ShapeDtypeStruct((depth, 8, n), F32),
        compiler_params=_params("arbitrary", "arbitrary"),
        name="adaln",
    )(cond, w_ada, b_ada.reshape(depth, 1, n))


def _ffn_kernel(*refs, convert_next, convert_win):
    refs = list(refs)
    take = lambda n: [refs.pop(0) for _ in range(n)]
    x_hbm, nw_ref, sh_ref, sc_ref, gt_ref, wg_ref, wu_ref, wd_ref = take(8)
    ng_ref, nu_ref, nd_ref = take(3) if convert_next else (None,) * 3
    wia_ref, wib_ref = take(2) if convert_win else (None,) * 2
    (o_hbm,) = take(1)
    cg_ref, cu_ref, cd_ref = take(3) if convert_next else (None,) * 3
    cia_ref, cib_ref = take(2) if convert_win else (None,) * 2
    xs_ref, acc_ref, h_ref, sem = take(4)
    i, j = pl.program_id(0), pl.program_id(1)
    n_tiles, n_ff = pl.num_programs(0), pl.num_programs(1)
    tm = xs_ref.shape[0]

    def x_copy(tile):
        return pltpu.make_async_copy(x_hbm.at[pl.ds(tile * tm, tm)], xs_ref, sem.at[0])

    def o_copy(tile):
        return pltpu.make_async_copy(acc_ref, o_hbm.at[pl.ds(tile * tm, tm)], sem.at[1])

    def gate_up():
        h = h_ref[...]
        g = jnp.dot(h, wg_ref[...].astype(BF16), preferred_element_type=F32)
        u = jnp.dot(h, wu_ref[...].astype(BF16), preferred_element_type=F32)
        return (jax.nn.silu(g) * u).astype(BF16)

    def down(a):
        return (0.5 * gt_ref[...]) * jnp.dot(a, wd_ref[...].astype(BF16), preferred_element_type=F32)

    def convert():
        if convert_next:
            cg_ref[...] = ng_ref[...].astype(BF16)
            cu_ref[...] = nu_ref[...].astype(BF16)
            cd_ref[...] = nd_ref[...].astype(BF16)
        if convert_win:
            cia_ref[...] = wia_ref[...].astype(BF16)
            cib_ref[...] = wib_ref[...].astype(BF16)

    @pl.when(j == 0)
    def _():
        @pl.when(i == 0)
        def _():
            x_copy(0).start()

        x_copy(i).wait()
        h_ref[...] = _modulate(xs_ref[...], nw_ref[...], sh_ref[...], sc_ref[...]).astype(BF16)
        a = gate_up()
        convert()

        @pl.when(i > 0)
        def _():
            o_copy(i - 1).wait()

        acc_ref[...] = xs_ref[...] + down(a)

        @pl.when(i + 1 < n_tiles)
        def _():
            x_copy(i + 1).start()

    @pl.when(j > 0)
    def _():
        acc_ref[...] += down(gate_up())
        convert()

    @pl.when(j == n_ff - 1)
    def _():
        o_copy(i).start()

        @pl.when(i == n_tiles - 1)
        def _():
            o_copy(i).wait()


def _ffn_call(x, nw, shift, scale, gate, w, tm, next_f32=None, win_f32=None):
    s, d = x.shape
    n_tiles = s // tm
    row = pl.BlockSpec((1, d), lambda i, j: (0, 0))
    if not isinstance(w[0], tuple):
        tf = FFN_TILE_F
        w_args = list(w)
        w_specs = [pl.BlockSpec((None, d, tf), lambda i, j: (j, 0, 0)),
                   pl.BlockSpec((None, d, tf), lambda i, j: (j, 0, 0)),
                   pl.BlockSpec((tf, d), lambda i, j: (j, 0))]
    else:
        tf = FFN_TILE_F
        w_args, l0, w0 = w
        w_args = list(w_args)
        w_specs = [pl.BlockSpec((None, None, d, tf), lambda i, j: (l0, w0, 0, j)),
                   pl.BlockSpec((None, None, d, tf), lambda i, j: (l0, w0, 0, j)),
                   pl.BlockSpec((None, None, tf, d), lambda i, j: (l0, w0, j, 0))]
    in_specs = [pl.BlockSpec(memory_space=pl.ANY), row, row, row, row, *w_specs]
    out_specs = [pl.BlockSpec(memory_space=pl.ANY)]
    out_shape = [jax.ShapeDtypeStruct((s, d), F32)]
    args = [x, nw, shift, scale, gate, *w_args]
    if next_f32 is not None:
        (ng, nu, nd), l2, w2 = next_f32
        n_ff = D_FF // tf
        rb = d // n_tiles
        rd = D_FF // (n_tiles * n_ff)
        assert rb * n_tiles == d and rd * n_tiles * n_ff == D_FF
        in_specs += [
            pl.BlockSpec((None, None, rb, tf), lambda i, j: (l2, w2, i, j)),
            pl.BlockSpec((None, None, rb, tf), lambda i, j: (l2, w2, i, j)),
            pl.BlockSpec((None, None, rd, d), lambda i, j: (l2, w2, i * n_ff + j, 0)),
        ]
        out_specs += [
            pl.BlockSpec((None, rb, tf), lambda i, j: (j, i, 0)),
            pl.BlockSpec((None, rb, tf), lambda i, j: (j, i, 0)),
            pl.BlockSpec((rd, d), lambda i, j: (i * n_ff + j, 0)),
        ]
        out_shape += [jax.ShapeDtypeStruct((n_ff, d, tf), BF16), jax.ShapeDtypeStruct((n_ff, d, tf), BF16),
                      jax.ShapeDtypeStruct((D_FF, d), BF16)]
        args += [ng, nu, nd]
    if win_f32 is not None:
        w_in, l1 = win_f32
        ct, rb = COL_TILE, d // n_tiles
        assert D_FF // tf >= INPROJ_STEPS
        blk_a = lambda j: jnp.minimum(j, INPROJ_STEPS - 1)
        blk_b = lambda j: jnp.minimum(j, INPROJ_STEPS - 2)
        in_specs += [
            pl.BlockSpec((None, rb, ct), lambda i, j: (l1, i, _inproj_tile_a(blk_a(j)))),
            pl.BlockSpec((None, rb, ct), lambda i, j: (l1, i, _inproj_tile_b_of_block(blk_b(j)))),
        ]
        out_specs += [pl.BlockSpec((rb, ct), lambda i, j: (i, blk_a(j))),
                      pl.BlockSpec((rb, ct), lambda i, j: (i, blk_b(j)))]
        out_shape += [jax.ShapeDtypeStruct((d, INPROJ_STEPS * ct), BF16),
                      jax.ShapeDtypeStruct((d, (INPROJ_STEPS - 1) * ct), BF16)]
        args += [w_in, w_in]
    n_conv = 3 if next_f32 is not None else 0
    outs = pl.pallas_call(
        functools.partial(_ffn_kernel, convert_next=next_f32 is not None, convert_win=win_f32 is not None),
        grid=(n_tiles, D_FF // tf),
        in_specs=in_specs,
        out_specs=out_specs,
        out_shape=out_shape,
        scratch_shapes=[pltpu.VMEM((tm, d), F32), pltpu.VMEM((tm, d), F32), pltpu.VMEM((tm, d), BF16),
                        pltpu.SemaphoreType.DMA((2,))],
        compiler_params=_params("arbitrary", "arbitrary"),
        name="ffn",
    )(*args)
    w_next = tuple(outs[1:1 + n_conv]) if next_f32 is not None else None
    win_next = tuple(outs[1 + n_conv:]) if win_f32 is not None else None
    return outs[0], w_next, win_next


Q_TILES = ATTN_W // COL_TILE
KV_TILE = Q_TILES
UF_TILE = KV_TILE + 1
UP_TILE = UF_TILE + 1
G_TILE0 = UP_TILE + 1
N_COL_TILES = IN_W // COL_TILE


def _norm_rope(z, gain, cos, sin_signed):
    y = z * lax.rsqrt(jnp.mean(z * z, axis=-1, keepdims=True) + EPS) * gain
    lane = lax.broadcasted_iota(jnp.int32, y.shape, 1)
    first_half = (lane & (D_HEAD // 4)) == 0
    partner = jnp.where(first_half, pltpu.roll(y, D_HEAD - D_HEAD // 4, 1), pltpu.roll(y, D_HEAD // 4, 1))
    return y * cos + partner * sin_signed


STEP_Q, STEP_KV_UF, STEP_UP, STEP_G0 = 0, 1, 2, 3


def _inproj_kernel(x_ref, nw_ref, sh_ref, sc_ref, qg_ref, kg_ref, cos_ref, sin_ref, w_ref, w2_ref,
                   q_ref, kv_ref, uf_ref, up_ref, g_ref, h_ref, z_ref):
    j = pl.program_id(1)

    def proj(w):
        return jnp.dot(h_ref[...], w[...].astype(BF16), preferred_element_type=F32)

    def q_epilogue(z, col0):
        cos, sin = cos_ref[...], sin_ref[...]
        for hh in range(COL_TILE // D_HEAD):
            sl = slice(hh * D_HEAD, (hh + 1) * D_HEAD)
            q = _norm_rope(z[:, sl], qg_ref[...], cos, sin) * Q_SCALE
            q_ref[:, col0 + hh * D_HEAD:col0 + (hh + 1) * D_HEAD] = q.astype(BF16)

    def kv_epilogue(z):
        cos, sin = cos_ref[...], sin_ref[...]
        for hh in range(N_KV_HEADS):
            sl = slice(hh * D_HEAD, (hh + 1) * D_HEAD)
            kv_ref[:, sl] = _norm_rope(z[:, sl], kg_ref[...], cos, sin).astype(BF16)
        kv_ref[:, KV_W:] = z[:, KV_W:].astype(BF16)

    @pl.when(j == STEP_Q)
    def _():
        h_ref[...] = _modulate(x_ref[...], nw_ref[...], sh_ref[...], sc_ref[...]).astype(BF16)
        z_ref[0] = proj(w_ref)
        z_ref[1] = proj(w2_ref)

    @pl.when(j == STEP_KV_UF)
    def _():
        z_ref[2] = proj(w_ref)
        uf_ref[...] = proj(w2_ref).astype(BF16)
        for t in range(Q_TILES):
            q_epilogue(z_ref[t], t * COL_TILE)

    @pl.when(j == STEP_UP)
    def _():
        up_ref[...] = proj(w_ref).astype(BF16)
        kv_epilogue(z_ref[2])

    @pl.when(j >= STEP_G0)
    def _():
        g_ref[:, :COL_TILE] = proj(w_ref).astype(BF16)
        g_ref[:, COL_TILE:] = proj(w2_ref).astype(BF16)


INPROJ_STEPS = STEP_G0 + (N_COL_TILES - G_TILE0) // 2


def _inproj_tile_a(s):
    return 2 * s - (s >= STEP_G0).astype(jnp.int32)


def _inproj_tile_b_of_block(k):
    return 2 * k + 1 + (k >= STEP_UP).astype(jnp.int32)


def _inproj_block_b(s):
    return s - (s > STEP_UP).astype(jnp.int32)


def _inproj_call(x, nw, shift, scale, qg, kg, cos, sin, w_ab, tm):
    s, d = x.shape
    ct = COL_TILE
    assert (Q_TILES, KV_TILE, UF_TILE, UP_TILE, G_TILE0) == (2, 2, 3, 4, 5) and (N_COL_TILES - G_TILE0) % 2 == 0
    row = pl.BlockSpec((1, d), lambda i, j: (0, 0))
    hrow = pl.BlockSpec((1, D_HEAD), lambda i, j: (0, 0))
    tab = pl.BlockSpec((tm, D_HEAD), lambda i, j: (i, 0))
    n_steps = INPROJ_STEPS
    return pl.pallas_call(
        _inproj_kernel,
        grid=(s // tm, n_steps),
        in_specs=[
            pl.BlockSpec((tm, d), lambda i, j: (i, 0)),
            row, row, row, hrow, hrow, tab, tab,
            pl.BlockSpec((d, ct), lambda i, j: (0, j)),
            pl.BlockSpec((d, ct), lambda i, j: (0, _inproj_block_b(j))),
        ],
        out_specs=[
            pl.BlockSpec((tm, Q_TILES * ct), lambda i, j: (i, 0)),
            pl.BlockSpec((tm, ct), lambda i, j: (i, 0)),
            pl.BlockSpec((tm, ct), lambda i, j: (i, 0)),
            pl.BlockSpec((tm, ct), lambda i, j: (i, 0)),
            pl.BlockSpec((tm, 2 * ct), lambda i, j: (i, jnp.maximum(j - STEP_G0, 0))),
        ],
        out_shape=[
            jax.ShapeDtypeStruct((s, ATTN_W), BF16),
            jax.ShapeDtypeStruct((s, 2 * KV_W), BF16),
            jax.ShapeDtypeStruct((s, FOURIER_W), BF16),
            jax.ShapeDtypeStruct((s, POOL_W), BF16),
            jax.ShapeDtypeStruct((s, GATE_W), BF16),
        ],
        scratch_shapes=[pltpu.VMEM((tm, d), BF16), pltpu.VMEM((3, tm, ct), F32)],
        compiler_params=_params("arbitrary", "arbitrary"),
        name="inproj",
    )(x, nw, shift, scale, qg, kg, cos, sin, *w_ab)


def _qk(q, k):
    return lax.dot_general(q, k, (((1,), (1,)), ((), ())), preferred_element_type=F32)


def _k_cols(kvh):
    return slice(kvh * D_HEAD, (kvh + 1) * D_HEAD)


def _v_cols(kvh):
    return slice(KV_W + kvh * D_HEAD, KV_W + (kvh + 1) * D_HEAD)


def _sink_column(sink_ref, kvh, rows):
    head = lax.broadcasted_iota(jnp.int32, (GROUP * rows, 1), 0) // rows
    col = jnp.full((GROUP * rows, 1), sink_ref[kvh * GROUP], F32)
    for hh in range(1, GROUP):
        col = jnp.where(head == hh, sink_ref[kvh * GROUP + hh], col)
    return col * LOG2_E


def _stack_heads(q_ref, kvh):
    heads = range(kvh * GROUP, (kvh + 1) * GROUP)
    return jnp.concatenate([q_ref[:, hh * D_HEAD:(hh + 1) * D_HEAD] for hh in heads], axis=0)


def _softmax_pv(scores, values, sink_col):
    def lane_tiles(a):
        return [a[:, t:t + D_HEAD] for t in range(0, a.shape[1], D_HEAD)]

    m_tile = functools.reduce(jnp.maximum, [t for s in scores for t in lane_tiles(s)])
    m = jnp.maximum(sink_col, jnp.max(m_tile, axis=-1, keepdims=True))
    e_tile = None
    acc = None
    for s, v in zip(scores, values):
        e = jnp.exp2(s - m)
        for t in lane_tiles(e):
            e_tile = t if e_tile is None else e_tile + t
        pv = jnp.dot(e.astype(BF16), v, preferred_element_type=F32)
        acc = pv if acc is None else acc + pv
    denom = jnp.exp2(sink_col - m) + jnp.sum(e_tile, axis=-1, keepdims=True)
    return acc / denom


def _unstack_heads(out, o_ref, kvh, rows):
    for g in range(GROUP):
        hh = kvh * GROUP + g
        o_ref[:, hh * D_HEAD:(hh + 1) * D_HEAD] = out[g * rows:(g + 1) * rows].astype(o_ref.dtype)


ATTN_Q_BLOCKS = 4


def _window_attn_kernel(sink_ref, q_ref, *refs):
    kv_refs, kvx_ref, o_ref = refs[:-2], refs[-2], refs[-1]
    i = pl.program_id(0)
    n_steps = pl.num_programs(0)
    shape = (GROUP * BLOCK, BLOCK)
    qi = lax.broadcasted_iota(jnp.int32, shape, 0) % BLOCK
    kj = lax.broadcasted_iota(jnp.int32, shape, 1)
    for b in range(ATTN_Q_BLOCKS):
        kvp_ref, kvc_ref, kvn_ref = kv_refs[b:b + 3]
        keep_prev = (kj >= qi) if b > 0 else (kj >= qi) & (i > 0)
        keep_next = (kj <= qi) if b < ATTN_Q_BLOCKS - 1 else (kj <= qi) & (i < n_steps - 1)
        rows = slice(b * BLOCK, (b + 1) * BLOCK)
        for kvh in range(N_KV_HEADS):
            q = _stack_heads(q_ref.at[rows], kvh)
            ks, vs = _k_cols(kvh), _v_cols(kvh)
            s_prev = jnp.where(keep_prev, _qk(q, kvp_ref[:, ks]), NEG_INF)
            s_cur = _qk(q, kvc_ref[:, ks])
            s_next = jnp.where(keep_next, _qk(q, kvn_ref[:, ks]), NEG_INF)
            s_ctx = _qk(q, kvx_ref[:, ks])
            out = _softmax_pv([s_prev, s_cur, s_next, s_ctx],
                              [kvp_ref[:, vs], kvc_ref[:, vs], kvn_ref[:, vs], kvx_ref[:, vs]],
                              _sink_column(sink_ref, kvh, BLOCK))
            _unstack_heads(out, o_ref.at[rows], kvh, BLOCK)


def _window_attn_call(q, kv, kv_ctx, sink):
    s = q.shape[0]
    c = kv_ctx.shape[0]
    nb = s // BLOCK
    qb = ATTN_Q_BLOCKS
    kblk = lambda off: pl.BlockSpec((BLOCK, 2 * KV_W), lambda i: (jnp.clip(i * qb + off, 0, nb - 1), 0))
    return pl.pallas_call(
        _window_attn_kernel,
        grid=(nb // qb,),
        in_specs=[
            pl.BlockSpec(memory_space=pltpu.SMEM),
            pl.BlockSpec((qb * BLOCK, ATTN_W), lambda i: (i, 0)),
            *[kblk(off) for off in range(-1, qb + 1)],
            pl.BlockSpec((c, 2 * KV_W), lambda i: (0, 0)),
        ],
        out_specs=pl.BlockSpec((qb * BLOCK, ATTN_W), lambda i: (i, 0)),
        out_shape=jax.ShapeDtypeStruct((s, ATTN_W), BF16),
        compiler_params=_params("arbitrary"),
        name="window_attn",
    )(sink, q, *([kv] * (qb + 2)), kv_ctx)


def _ctx_attn_kernel(sink_ref, q_ref, kv_ref, o_ref):
    rows = q_ref.shape[0]
    for kvh in range(N_KV_HEADS):
        q = _stack_heads(q_ref, kvh)
        out = _softmax_pv([_qk(q, kv_ref[:, _k_cols(kvh)])], [kv_ref[:, _v_cols(kvh)]],
                          _sink_column(sink_ref, kvh, rows))
        _unstack_heads(out, o_ref, kvh, rows)


def _ctx_attn_call(q, kv, sink):
    c = q.shape[0]
    return pl.pallas_call(
        _ctx_attn_kernel,
        grid=(1,),
        in_specs=[
            pl.BlockSpec(memory_space=pltpu.SMEM),
            pl.BlockSpec((c, ATTN_W), lambda i: (0, 0)),
            pl.BlockSpec((c, 2 * KV_W), lambda i: (0, 0)),
        ],
        out_specs=pl.BlockSpec((c, ATTN_W), lambda i: (0, 0)),
        out_shape=jax.ShapeDtypeStruct((c, ATTN_W), BF16),
        compiler_params=_params("arbitrary"),
        name="ctx_attn",
    )(sink, q, kv)


def _dft_tables(s_len):
    n1 = DFT_ROWS if s_len > 1024 else 1
    n2 = s_len // n1
    k1 = np.arange(n1)
    ang1 = 2.0 * np.pi * ((k1[:, None] * k1[None, :]) % n1) / n1
    w1 = np.concatenate([np.cos(ang1), -np.sin(ang1)], axis=0)
    w1 = np.kron(w1, np.eye(DFT_S2_GROUP))
    k2 = np.arange(n2)
    phase = (k2[None, None, :] * (k1[:, None, None] + n1 * k2[None, :, None])) % s_len
    ang2 = 2.0 * np.pi * phase / s_len
    gr, gi = np.cos(ang2), -np.sin(ang2)
    g = np.concatenate([np.concatenate([gr, -gi], axis=2),
                        np.concatenate([gi, gr], axis=2)], axis=1)
    if n1 == 1:
        g = g[:, :, :n2]
    c = np.arange(FOURIER_GROUP_W)
    angc = 2.0 * np.pi * ((c[:, None] * c[None, :]) % FOURIER_GROUP_W) / FOURIER_GROUP_W
    norm = 1.0 / np.sqrt(float(s_len) * FOURIER_GROUP_W)
    return (w1.astype(np.float32), g.astype(np.float32),
            (np.cos(angc) * norm).astype(np.float32), (np.sin(angc) * norm).astype(np.float32))


DFT_K1_GROUP = 8


def _dft_cols(g, x, cc_ref, sc_ref, o_ref, col0):
    y = jnp.dot(g, x, preferred_element_type=F32)
    n = g.shape[0] // 2
    yr, yi = y[:n].astype(BF16), y[n:].astype(BF16)
    for gi in range(N_FOURIER_GROUPS):
        sl = slice(gi * FOURIER_GROUP_W, (gi + 1) * FOURIER_GROUP_W)
        o = (jnp.dot(yr[:, sl], cc_ref[...], preferred_element_type=F32)
             + jnp.dot(yi[:, sl], sc_ref[...], preferred_element_type=F32))
        o_ref[:, col0 + gi * FOURIER_GROUP_W:col0 + (gi + 1) * FOURIER_GROUP_W] = o.astype(o_ref.dtype)


def _dft_seq_kernel(w_ref, u_ref, g_ref, cc_ref, sc_ref, o_ref, x1_ref):
    j = pl.program_id(0)
    n1, grp, cw = u_ref.shape
    n_a = x1_ref.shape[1] // grp
    kb = g_ref.shape[0]

    @pl.when(j < n_a)
    def _():
        y = jnp.dot(w_ref[...], u_ref[...].reshape(n1 * grp, cw), preferred_element_type=F32)
        s2 = pl.multiple_of(j * grp, grp)
        x1_ref[:, pl.ds(s2, grp), :] = y.astype(BF16).reshape(2 * n1, grp, cw)

    @pl.when(j >= n_a)
    def _():
        k0 = (j - n_a) * kb
        for b in range(kb):
            x = jnp.concatenate([x1_ref[k0 + b], x1_ref[n1 + k0 + b]], axis=0)
            _dft_cols(g_ref[b], x, cc_ref, sc_ref, o_ref, b * cw)


def _dft_seq_call(w1k, g, uf, cc, sc):
    s, cw = uf.shape
    n1 = DFT_ROWS
    n2 = s // n1
    grp, kb, gw = DFT_S2_GROUP, DFT_K1_GROUP, FOURIER_GROUP_W
    n_a, n_b = n2 // grp, n1 // kb
    stage2 = lambda j: jnp.maximum(j - n_a, 0)
    out = pl.pallas_call(
        _dft_seq_kernel,
        grid=(n_a + n_b,),
        in_specs=[pl.BlockSpec((2 * n1 * grp, n1 * grp), lambda j: (0, 0)),
                  pl.BlockSpec((n1, grp, cw), lambda j: (0, jnp.minimum(j, n_a - 1), 0)),
                  pl.BlockSpec((kb, 2 * n2, 2 * n2), lambda j: (stage2(j), 0, 0)),
                  pl.BlockSpec((gw, gw), lambda j: (0, 0)),
                  pl.BlockSpec((gw, gw), lambda j: (0, 0))],
        out_specs=pl.BlockSpec((n2, kb * cw), lambda j: (0, stage2(j))),
        out_shape=jax.ShapeDtypeStruct((n2, n1 * cw), BF16),
        scratch_shapes=[pltpu.VMEM((2 * n1, n2, cw), BF16)],
        compiler_params=_params("arbitrary"),
        name="dft_seq",
    )(w1k, uf.reshape(n1, n2, cw), g, cc, sc)
    return out.reshape(s, cw)


def _dft_ctx_kernel(g_ref, u_ref, cc_ref, sc_ref, o_ref):
    _dft_cols(g_ref[0], u_ref[...], cc_ref, sc_ref, o_ref, 0)


def _dft_small_call(g, uf, cc, sc):
    c, cw = uf.shape
    gw = FOURIER_GROUP_W
    return pl.pallas_call(
        _dft_ctx_kernel,
        grid=(1,),
        in_specs=[pl.BlockSpec((1, 2 * c, c), lambda k: (0, 0, 0)),
                  pl.BlockSpec((c, cw), lambda k: (0, 0)),
                  pl.BlockSpec((gw, gw), lambda k: (0, 0)),
                  pl.BlockSpec((gw, gw), lambda k: (0, 0))],
        out_specs=pl.BlockSpec((c, cw), lambda k: (0, 0)),
        out_shape=jax.ShapeDtypeStruct((c, cw), BF16),
        compiler_params=_params("arbitrary"),
        name="dft_ctx",
    )(g, uf, cc, sc)


def _merge_kernel(a_ref, f_ref, up_ref, upp_ref, upn_ref, ga_ref, gf_ref, gp_ref, x_ref,
                  gate_ref, ps_ref, wo_ref, wf_ref, wp_ref, wout_ref, o_ref, ext_ref, *, s_len):
    i = pl.program_id(0)
    tm = up_ref.shape[0]
    h = POOL_HALO
    ext_ref[0:h] = jnp.where(i > 0, upp_ref[...].astype(F32), 0.0)
    ext_ref[h:h + tm] = up_ref[...].astype(F32)
    ext_ref[h + tm:] = jnp.where(i < pl.num_programs(0) - 1, upn_ref[...].astype(F32), 0.0)

    t = i * tm + lax.broadcasted_iota(jnp.int32, (tm, 1), 0)
    pooled = []
    for gi, w in enumerate(POOL_WINDOWS):
        sl = slice(gi * POOL_GROUP_W, (gi + 1) * POOL_GROUP_W)
        total = ext_ref[h - w // 2:h - w // 2 + tm, sl]
        for off in range(-(w // 2) + 1, w - w // 2):
            total = total + ext_ref[h + off:h + off + tm, sl]
        count = jnp.minimum(t + (w - w // 2), s_len) - jnp.maximum(t - w // 2, 0)
        u = ext_ref[h:h + tm, sl]
        pooled.append((total / count.astype(F32) - u).astype(BF16))
    yp = jnp.concatenate(
        [jnp.dot(p, wp_ref[gi], preferred_element_type=F32) for gi, p in enumerate(pooled)], axis=-1)

    ya = jnp.dot(a_ref[...], wo_ref[...], preferred_element_type=F32)
    yf = jnp.dot(f_ref[...], wf_ref[...], preferred_element_type=F32)
    y = (jax.nn.sigmoid(ga_ref[...].astype(F32)) * ya
         + jax.nn.sigmoid(gf_ref[...].astype(F32)) * yf
         + jax.nn.sigmoid(gp_ref[...].astype(F32)) * (yp * ps_ref[...]))
    mix = jnp.dot(y.astype(BF16), wout_ref[...], preferred_element_type=F32)
    o_ref[...] = x_ref[...] + gate_ref[...] * mix


def _merge_call(a, f, up, g, x, gate, pool_scale, w_attn_o, w_fourier, w_pool, w_out, l, tm):
    s, d = x.shape
    h = POOL_HALO
    n_halo = s // h
    row = pl.BlockSpec((1, d), lambda i: (0, 0))
    once = dict(pipeline_mode=pl.Buffered(1))
    return pl.pallas_call(
        functools.partial(_merge_kernel, s_len=s),
        grid=(s // tm,),
        in_specs=[
            pl.BlockSpec((tm, ATTN_W), lambda i: (i, 0)),
            pl.BlockSpec((tm, FOURIER_W), lambda i: (i, 0)),
            pl.BlockSpec((tm, POOL_W), lambda i: (i, 0)),
            pl.BlockSpec((h, POOL_W), lambda i: (jnp.maximum(i * (tm // h) - 1, 0), 0)),
            pl.BlockSpec((h, POOL_W), lambda i: (jnp.minimum((i + 1) * (tm // h), n_halo - 1), 0)),
            pl.BlockSpec((tm, d), lambda i: (i, 0)),
            pl.BlockSpec((tm, d), lambda i: (i, 1)),
            pl.BlockSpec((tm, d), lambda i: (i, 2)),
            pl.BlockSpec((tm, d), lambda i: (i, 0)),
            row, row,
            pl.BlockSpec((None, ATTN_W, d), lambda i: (l, 0, 0), **once),
            pl.BlockSpec((None, FOURIER_W, d), lambda i: (l, 0, 0), **once),
            pl.BlockSpec((None, len(POOL_WINDOWS), POOL_GROUP_W, POOL_OUT_GROUP_W),
                         lambda i: (l, 0, 0, 0), **once),
            pl.BlockSpec((None, d, d), lambda i: (l, 0, 0), **once),
        ],
        out_specs=pl.BlockSpec((tm, d), lambda i: (i, 0)),
        out_shape=jax.ShapeDtypeStruct((s, d), F32),
        scratch_shapes=[pltpu.VMEM((tm + 2 * h, POOL_W), F32)],
        compiler_params=_params("arbitrary"),
        name="merge",
    )(a, f, up, up, up, g, g, g, x, gate, pool_scale, w_attn_o, w_fourier, w_pool, w_out)


def _rope_tables(s_len):
    rows_n = s_len // GRID_W
    n_freq = D_HEAD // 4
    freqs = ROPE_BASE ** (-jnp.arange(n_freq, dtype=F32) / n_freq)
    ar = jnp.arange(rows_n).astype(F32)[:, None] * freqs
    ac = jnp.arange(GRID_W).astype(F32)[:, None] * freqs
    by_row = lambda t: jnp.repeat(t, GRID_W, axis=0)
    by_col = lambda t: jnp.tile(t, (rows_n, 1))
    cr, sr, cc, sc = by_row(jnp.cos(ar)), by_row(jnp.sin(ar)), by_col(jnp.cos(ac)), by_col(jnp.sin(ac))
    cos = jnp.concatenate([cr, cr, cc, cc], axis=-1)
    sin = jnp.concatenate([-sr, sr, -sc, sc], axis=-1)
    return cos, sin


def kernel(x, c, ctx, c_ctx, w_ada, b_ada, norm_w, ffn_w_gate, ffn_w_up, ffn_w_down,
           w_in, q_gain, k_gain, sink, w_attn_o, w_fourier, w_pool, pool_scale, w_out):
    b, s_len, d = x.shape
    c_len = ctx.shape[1]
    assert b == 1 and d == D_MODEL and s_len % (DFT_ROWS * BLOCK) == 0
    x, ctx = x[0], ctx[0]

    cond = jnp.zeros((8, d), F32).at[0].set(c[0]).at[1].set(c_ctx)
    mod = _ada_call(cond, w_ada, b_ada).reshape(DEPTH, 8, N_MOD, 1, d)

    ffn_f32 = (ffn_w_gate, ffn_w_up, ffn_w_down)
    ffn_w = (ffn_f32, 0, 0)
    w_ao, w_fo = w_attn_o.astype(BF16), w_fourier.astype(BF16)
    w_po, w_ou = w_pool.astype(BF16), w_out.astype(BF16)

    cos_x, sin_x = _rope_tables(s_len)
    cos_c, sin_c = jnp.ones((c_len, D_HEAD), F32), jnp.zeros((c_len, D_HEAD), F32)
    w1, g_x, cc, sc = (jnp.asarray(t).astype(BF16) for t in _dft_tables(s_len))
    _, g_c, cc_c, sc_c = (jnp.asarray(t).astype(BF16) for t in _dft_tables(c_len))

    tm_x, tm_m = ROW_TILE, MERGE_ROW_TILE
    for l in range(DEPTH):
        last = l == DEPTH - 1
        mx, mc = mod[l, 0], mod[l, 1]
        nw = norm_w[l][:, None, :]
        qg, kg = q_gain[l][None, :], k_gain[l][None, :]
        ps = pool_scale[l][None, :]

        x, ffn_w_post, w_ab = _ffn_call(x, nw[0], mx[0], mx[1], mx[2], ffn_w, tm_x, (ffn_f32, l, 1), (w_in, l))
        ctx, _, _ = _ffn_call(ctx, nw[0], mc[0], mc[1], mc[2], ffn_w, c_len)

        qx, kvx, ufx, upx, gx = _inproj_call(x, nw[1], mx[3], mx[4], qg, kg, cos_x, sin_x, w_ab, tm_x)
        qc, kvc, ufc, upc, gc = _inproj_call(ctx, nw[1], mc[3], mc[4], qg, kg, cos_c, sin_c, w_ab, c_len)

        ax = _window_attn_call(qx, kvx, kvc, sink[l])
        fx = _dft_seq_call(w1, g_x, ufx, cc, sc)
        x = _merge_call(ax, fx, upx, gx, x, mx[5], ps, w_ao, w_fo, w_po, w_ou, l, tm_m)
        if not last:
            ac = _ctx_attn_call(qc, kvc, sink[l])
            fc = _dft_small_call(g_c, ufc, cc_c, sc_c)
            ctx = _merge_call(ac, fc, upc, gc, ctx, mc[5], ps, w_ao, w_fo, w_po, w_ou, l, c_len)

        x, ffn_w_next, _ = _ffn_call(x, nw[2], mx[6], mx[7], mx[8], ffn_w_post, tm_x,
                                     None if last else (ffn_f32, l + 1, 0))
        if not last:
            ctx, _, _ = _ffn_call(ctx, nw[2], mc[6], mc[7], mc[8], ffn_w_post, c_len)
        ffn_w = ffn_w_next
    return x[None]
```

```python
import functools

import numpy as np
import jax
import jax.numpy as jnp
from jax import lax
from jax.experimental import pallas as pl
from jax.experimental.pallas import tpu as pltpu

D_MODEL = 2048
DEPTH = 4
GRID_W = 64
N_HEADS = 8
N_KV_HEADS = 2
GROUP = N_HEADS // N_KV_HEADS
D_HEAD = 128
ATTN_W = N_HEADS * D_HEAD
KV_W = N_KV_HEADS * D_HEAD
WINDOW = 128
BLOCK = 128
FOURIER_W = D_MODEL // 4
N_FOURIER_GROUPS = 4
FOURIER_GROUP_W = FOURIER_W // N_FOURIER_GROUPS
POOL_WINDOWS = (2, 4, 8, 16)
POOL_W = D_MODEL // 4
POOL_GROUP_W = POOL_W // len(POOL_WINDOWS)
POOL_OUT_GROUP_W = D_MODEL // len(POOL_WINDOWS)
N_BRANCHES = 3
GATE_W = N_BRANCHES * D_MODEL
IN_W = ATTN_W + 2 * KV_W + FOURIER_W + POOL_W + GATE_W
D_FF = 5632
N_MOD = 9
ROPE_BASE = 10000.0
EPS = 1e-6
NEG_INF = -1e30

BF16 = jnp.bfloat16
F32 = jnp.float32

VMEM_LIMIT_BYTES = 60 * 1024 * 1024
ROW_TILE = 1024
MERGE_ROW_TILE = 256
ADA_TILE_N = 2048
POOL_HALO = 16
COL_TILE = 512
FFN_TILE_F = 512
LOG2_E = 1.4426950408889634
Q_SCALE = LOG2_E * D_HEAD ** -0.5
DFT_ROWS = GRID_W
DFT_S2_GROUP = 16


def _params(*semantics):
    return pltpu.CompilerParams(dimension_semantics=semantics, vmem_limit_bytes=VMEM_LIMIT_BYTES)


def _modulate(x, gain, shift, scale):
    y = x * lax.rsqrt(jnp.mean(x * x, axis=-1, keepdims=True) + EPS)
    return (y * gain) * (1 + scale) + shift


def _ada_kernel(cond_ref, w_ref, b_ref, o_ref):
    h = jax.nn.silu(cond_ref[...]).astype(BF16)
    o_ref[0] = jnp.dot(h, w_ref[0].astype(BF16), preferred_element_type=F32) + b_ref[0]


def _ada_call(cond, w_ada, b_ada):
    depth, d, n = w_ada.shape
    tn = ADA_TILE_N
    return pl.pallas_call(
        _ada_kernel,
        grid=(depth, n // tn),
        in_specs=[
            pl.BlockSpec((8, d), lambda l, j: (0, 0)),
            pl.BlockSpec((1, d, tn), lambda l, j: (l, 0, j)),
            pl.BlockSpec((1, 1, tn), lambda l, j: (l, 0, j)),
        ],
        out_specs=pl.BlockSpec((1, 8, tn), lambda l, j: (l, 0, j)),
        out_shape=jax.ShapeDtypeStruct((depth, 8, n), F32),
        compiler_params=_params("arbitrary", "arbitrary"),
        name="adaln",
    )(cond, w_ada, b_ada.reshape(depth, 1, n))


def _ffn_kernel(*refs, convert_next, convert_win):
    refs = list(refs)
    take = lambda n: [refs.pop(0) for _ in range(n)]
    x_hbm, nw_ref, sh_ref, sc_ref, gt_ref, wg_ref, wu_ref, wd_ref = take(8)
    ng_ref, nu_ref, nd_ref = take(3) if convert_next else (None,) * 3
    wia_ref, wib_ref = take(2) if convert_win else (None,) * 2
    (o_hbm,) = take(1)
    cg_ref, cu_ref, cd_ref = take(3) if convert_next else (None,) * 3
    cia_ref, cib_ref = take(2) if convert_win else (None,) * 2
    xs_ref, acc_ref, h_ref, sem = take(4)
    i, j = pl.program_id(0), pl.program_id(1)
    n_tiles, n_ff = pl.num_programs(0), pl.num_programs(1)
    tm = xs_ref.shape[0]

    def x_copy(tile):
        return pltpu.make_async_copy(x_hbm.at[pl.ds(tile * tm, tm)], xs_ref, sem.at[0])

    def o_copy(tile):
        return pltpu.make_async_copy(acc_ref, o_hbm.at[pl.ds(tile * tm, tm)], sem.at[1])

    def gate_up():
        h = h_ref[...]
        g = jnp.dot(h, wg_ref[...].astype(BF16), preferred_element_type=F32)
        u = jnp.dot(h, wu_ref[...].astype(BF16), preferred_element_type=F32)
        return (jax.nn.silu(g) * u).astype(BF16)

    def down(a):
        return (0.5 * gt_ref[...]) * jnp.dot(a, wd_ref[...].astype(BF16), preferred_element_type=F32)

    def convert():
        if convert_next:
            cg_ref[...] = ng_ref[...].astype(BF16)
            cu_ref[...] = nu_ref[...].astype(BF16)
            cd_ref[...] = nd_ref[...].astype(BF16)
        if convert_win:
            cia_ref[...] = wia_ref[...].astype(BF16)
            cib_ref[...] = wib_ref[...].astype(BF16)

    @pl.when(j == 0)
    def _():
        @pl.when(i == 0)
        def _():
            x_copy(0).start()

        x_copy(i).wait()
        h_ref[...] = _modulate(xs_ref[...], nw_ref[...], sh_ref[...], sc_ref[...]).astype(BF16)
        a = gate_up()
        convert()

        @pl.when(i > 0)
        def _():
            o_copy(i - 1).wait()

        acc_ref[...] = xs_ref[...] + down(a)

        @pl.when(i + 1 < n_tiles)
        def _():
            x_copy(i + 1).start()

    @pl.when(j > 0)
    def _():
        acc_ref[...] += down(gate_up())
        convert()

    @pl.when(j == n_ff - 1)
    def _():
        o_copy(i).start()

        @pl.when(i == n_tiles - 1)
        def _():
            o_copy(i).wait()


def _ffn_call(x, nw, shift, scale, gate, w, tm, next_f32=None, win_f32=None):
    s, d = x.shape
    n_tiles = s // tm
    row = pl.BlockSpec((1, d), lambda i, j: (0, 0))
    if not isinstance(w[0], tuple):
        tf = FFN_TILE_F
        w_args = list(w)
        w_specs = [pl.BlockSpec((None, d, tf), lambda i, j: (j, 0, 0)),
                   pl.BlockSpec((None, d, tf), lambda i, j: (j, 0, 0)),
                   pl.BlockSpec((tf, d), lambda i, j: (j, 0))]
    else:
        tf = FFN_TILE_F
        w_args, l0, w0 = w
        w_args = list(w_args)
        w_specs = [pl.BlockSpec((None, None, d, tf), lambda i, j: (l0, w0, 0, j)),
                   pl.BlockSpec((None, None, d, tf), lambda i, j: (l0, w0, 0, j)),
                   pl.BlockSpec((None, None, tf, d), lambda i, j: (l0, w0, j, 0))]
    in_specs = [pl.BlockSpec(memory_space=pl.ANY), row, row, row, row, *w_specs]
    out_specs = [pl.BlockSpec(memory_space=pl.ANY)]
    out_shape = [jax.ShapeDtypeStruct((s, d), F32)]
    args = [x, nw, shift, scale, gate, *w_args]
    if next_f32 is not None:
        (ng, nu, nd), l2, w2 = next_f32
        n_ff = D_FF // tf
        rb = d // n_tiles
        rd = D_FF // (n_tiles * n_ff)
        assert rb * n_tiles == d and rd * n_tiles * n_ff == D_FF
        in_specs += [
            pl.BlockSpec((None, None, rb, tf), lambda i, j: (l2, w2, i, j)),
            pl.BlockSpec((None, None, rb, tf), lambda i, j: (l2, w2, i, j)),
            pl.BlockSpec((None, None, rd, d), lambda i, j: (l2, w2, i * n_ff + j, 0)),
        ]
        out_specs += [
            pl.BlockSpec((None, rb, tf), lambda i, j: (j, i, 0)),
            pl.BlockSpec((None, rb, tf), lambda i, j: (j, i, 0)),
            pl.BlockSpec((rd, d), lambda i, j: (i * n_ff + j, 0)),
        ]
        out_shape += [jax.ShapeDtypeStruct((n_ff, d, tf), BF16), jax.ShapeDtypeStruct((n_ff, d, tf), BF16),
                      jax.ShapeDtypeStruct((D_FF, d), BF16)]
        args += [ng, nu, nd]
    if win_f32 is not None:
        w_in, l1 = win_f32
        ct, rb = COL_TILE, d // n_tiles
        assert D_FF // tf >= INPROJ_STEPS
        blk_a = lambda j: jnp.minimum(j, INPROJ_STEPS - 1)
        blk_b = lambda j: jnp.minimum(j, INPROJ_STEPS - 2)
        in_specs += [
            pl.BlockSpec((None, rb, ct), lambda i, j: (l1, i, _inproj_tile_a(blk_a(j)))),
            pl.BlockSpec((None, rb, ct), lambda i, j: (l1, i, _inproj_tile_b_of_block(blk_b(j)))),
        ]
        out_specs += [pl.BlockSpec((rb, ct), lambda i, j: (i, blk_a(j))),
                      pl.BlockSpec((rb, ct), lambda i, j: (i, blk_b(j)))]
        out_shape += [jax.ShapeDtypeStruct((d, INPROJ_STEPS * ct), BF16),
                      jax.ShapeDtypeStruct((d, (INPROJ_STEPS - 1) * ct), BF16)]
        args += [w_in, w_in]
    n_conv = 3 if next_f32 is not None else 0
    outs = pl.pallas_call(
        functools.partial(_ffn_kernel, convert_next=next_f32 is not None, convert_win=win_f32 is not None),
        grid=(n_tiles, D_FF // tf),
        in_specs=in_specs,
        out_specs=out_specs,
        out_shape=out_shape,
        scratch_shapes=[pltpu.VMEM((tm, d), F32), pltpu.VMEM((tm, d), F32), pltpu.VMEM((tm, d), BF16),
                        pltpu.SemaphoreType.DMA((2,))],
        compiler_params=_params("arbitrary", "arbitrary"),
        name="ffn",
    )(*args)
    w_next = tuple(outs[1:1 + n_conv]) if next_f32 is not None else None
    win_next = tuple(outs[1 + n_conv:]) if win_f32 is not None else None
    return outs[0], w_next, win_next


Q_TILES = ATTN_W // COL_TILE
KV_TILE = Q_TILES
UF_TILE = KV_TILE + 1
UP_TILE = UF_TILE + 1
G_TILE0 = UP_TILE + 1
N_COL_TILES = IN_W // COL_TILE


def _norm_rope(z, gain, cos, sin_signed):
    y = z * lax.rsqrt(jnp.mean(z * z, axis=-1, keepdims=True) + EPS) * gain
    lane = lax.broadcasted_iota(jnp.int32, y.shape, 1)
    first_half = (lane & (D_HEAD // 4)) == 0
    partner = jnp.where(first_half, pltpu.roll(y, D_HEAD - D_HEAD // 4, 1), pltpu.roll(y, D_HEAD // 4, 1))
    return y * cos + partner * sin_signed


STEP_Q, STEP_KV_UF, STEP_UP, STEP_G0 = 0, 1, 2, 3


def _inproj_kernel(x_ref, nw_ref, sh_ref, sc_ref, qg_ref, kg_ref, cos_ref, sin_ref, w_ref, w2_ref,
                   q_ref, kv_ref, uf_ref, up_ref, g_ref, h_ref, z_ref):
    j = pl.program_id(1)

    def proj(w):
        return jnp.dot(h_ref[...], w[...].astype(BF16), preferred_element_type=F32)

    def q_epilogue(z, col0):
        cos, sin = cos_ref[...], sin_ref[...]
        for hh in range(COL_TILE // D_HEAD):
            sl = slice(hh * D_HEAD, (hh + 1) * D_HEAD)
            q = _norm_rope(z[:, sl], qg_ref[...], cos, sin) * Q_SCALE
            q_ref[:, col0 + hh * D_HEAD:col0 + (hh + 1) * D_HEAD] = q.astype(BF16)

    def kv_epilogue(z):
        cos, sin = cos_ref[...], sin_ref[...]
        for hh in range(N_KV_HEADS):
            sl = slice(hh * D_HEAD, (hh + 1) * D_HEAD)
            kv_ref[:, sl] = _norm_rope(z[:, sl], kg_ref[...], cos, sin).astype(BF16)
        kv_ref[:, KV_W:] = z[:, KV_W:].astype(BF16)

    @pl.when(j == STEP_Q)
    def _():
        h_ref[...] = _modulate(x_ref[...], nw_ref[...], sh_ref[...], sc_ref[...]).astype(BF16)
        z_ref[0] = proj(w_ref)
        z_ref[1] = proj(w2_ref)

    @pl.when(j == STEP_KV_UF)
    def _():
        z_ref[2] = proj(w_ref)
        uf_ref[...] = proj(w2_ref).astype(BF16)
        for t in range(Q_TILES):
            q_epilogue(z_ref[t], t * COL_TILE)

    @pl.when(j == STEP_UP)
    def _():
        up_ref[...] = proj(w_ref).astype(BF16)
        kv_epilogue(z_ref[2])

    @pl.when(j >= STEP_G0)
    def _():
        g_ref[:, :COL_TILE] = proj(w_ref).astype(BF16)
        g_ref[:, COL_TILE:] = proj(w2_ref).astype(BF16)


INPROJ_STEPS = STEP_G0 + (N_COL_TILES - G_TILE0) // 2


def _inproj_tile_a(s):
    return 2 * s - (s >= STEP_G0).astype(jnp.int32)


def _inproj_tile_b_of_block(k):
    return 2 * k + 1 + (k >= STEP_UP).astype(jnp.int32)


def _inproj_block_b(s):
    return s - (s > STEP_UP).astype(jnp.int32)


def _inproj_call(x, nw, shift, scale, qg, kg, cos, sin, w_ab, tm):
    s, d = x.shape
    ct = COL_TILE
    assert (Q_TILES, KV_TILE, UF_TILE, UP_TILE, G_TILE0) == (2, 2, 3, 4, 5) and (N_COL_TILES - G_TILE0) % 2 == 0
    row = pl.BlockSpec((1, d), lambda i, j: (0, 0))
    hrow = pl.BlockSpec((1, D_HEAD), lambda i, j: (0, 0))
    tab = pl.BlockSpec((tm, D_HEAD), lambda i, j: (i, 0))
    n_steps = INPROJ_STEPS
    return pl.pallas_call(
        _inproj_kernel,
        grid=(s // tm, n_steps),
        in_specs=[
            pl.BlockSpec((tm, d), lambda i, j: (i, 0)),
            row, row, row, hrow, hrow, tab, tab,
            pl.BlockSpec((d, ct), lambda i, j: (0, j)),
            pl.BlockSpec((d, ct), lambda i, j: (0, _inproj_block_b(j))),
        ],
        out_specs=[
            pl.BlockSpec((tm, Q_TILES * ct), lambda i, j: (i, 0)),
            pl.BlockSpec((tm, ct), lambda i, j: (i, 0)),
            pl.BlockSpec((tm, ct), lambda i, j: (i, 0)),
            pl.BlockSpec((tm, ct), lambda i, j: (i, 0)),
            pl.BlockSpec((tm, 2 * ct), lambda i, j: (i, jnp.maximum(j - STEP_G0, 0))),
        ],
        out_shape=[
            jax.ShapeDtypeStruct((s, ATTN_W), BF16),
            jax.ShapeDtypeStruct((s, 2 * KV_W), BF16),
            jax.ShapeDtypeStruct((s, FOURIER_W), BF16),
            jax.ShapeDtypeStruct((s, POOL_W), BF16),
            jax.ShapeDtypeStruct((s, GATE_W), BF16),
        ],
        scratch_shapes=[pltpu.VMEM((tm, d), BF16), pltpu.VMEM((3, tm, ct), F32)],
        compiler_params=_params("arbitrary", "arbitrary"),
        name="inproj",
    )(x, nw, shift, scale, qg, kg, cos, sin, *w_ab)


def _qk(q, k):
    return lax.dot_general(q, k, (((1,), (1,)), ((), ())), preferred_element_type=F32)


def _k_cols(kvh):
    return slice(kvh * D_HEAD, (kvh + 1) * D_HEAD)


def _v_cols(kvh):
    return slice(KV_W + kvh * D_HEAD, KV_W + (kvh + 1) * D_HEAD)


def _sink_column(sink_ref, kvh, rows):
    head = lax.broadcasted_iota(jnp.int32, (GROUP * rows, 1), 0) // rows
    col = jnp.full((GROUP * rows, 1), sink_ref[kvh * GROUP], F32)
    for hh in range(1, GROUP):
        col = jnp.where(head == hh, sink_ref[kvh * GROUP + hh], col)
    return col * LOG2_E


def _stack_heads(q_ref, kvh):
    heads = range(kvh * GROUP, (kvh + 1) * GROUP)
    return jnp.concatenate([q_ref[:, hh * D_HEAD:(hh + 1) * D_HEAD] for hh in heads], axis=0)


def _softmax_pv(scores, values, sink_col):
    def lane_tiles(a):
        return [a[:, t:t + D_HEAD] for t in range(0, a.shape[1], D_HEAD)]

    m_tile = functools.reduce(jnp.maximum, [t for s in scores for t in lane_tiles(s)])
    m = jnp.maximum(sink_col, jnp.max(m_tile, axis=-1, keepdims=True))
    e_tile = None
    acc = None
    for s, v in zip(scores, values):
        e = jnp.exp2(s - m)
        for t in lane_tiles(e):
            e_tile = t if e_tile is None else e_tile + t
        pv = jnp.dot(e.astype(BF16), v, preferred_element_type=F32)
        acc = pv if acc is None else acc + pv
    denom = jnp.exp2(sink_col - m) + jnp.sum(e_tile, axis=-1, keepdims=True)
    return acc / denom


def _unstack_heads(out, o_ref, kvh, rows):
    for g in range(GROUP):
        hh = kvh * GROUP + g
        o_ref[:, hh * D_HEAD:(hh + 1) * D_HEAD] = out[g * rows:(g + 1) * rows].astype(o_ref.dtype)


ATTN_Q_BLOCKS = 8


def _window_attn_kernel(sink_ref, q_ref, *refs):
    kv_refs, kvx_ref, o_ref = refs[:-2], refs[-2], refs[-1]
    i = pl.program_id(0)
    n_steps = pl.num_programs(0)
    shape = (GROUP * BLOCK, BLOCK)
    qi = lax.broadcasted_iota(jnp.int32, shape, 0) % BLOCK
    kj = lax.broadcasted_iota(jnp.int32, shape, 1)
    for b in range(ATTN_Q_BLOCKS):
        kvp_ref, kvc_ref, kvn_ref = kv_refs[b:b + 3]
        keep_prev = (kj >= qi) if b > 0 else (kj >= qi) & (i > 0)
        keep_next = (kj <= qi) if b < ATTN_Q_BLOCKS - 1 else (kj <= qi) & (i < n_steps - 1)
        rows = slice(b * BLOCK, (b + 1) * BLOCK)
        for kvh in range(N_KV_HEADS):
            q = _stack_heads(q_ref.at[rows], kvh)
            ks, vs = _k_cols(kvh), _v_cols(kvh)
            s_prev = jnp.where(keep_prev, _qk(q, kvp_ref[:, ks]), NEG_INF)
            s_cur = _qk(q, kvc_ref[:, ks])
            s_next = jnp.where(keep_next, _qk(q, kvn_ref[:, ks]), NEG_INF)
            s_ctx = _qk(q, kvx_ref[:, ks])
            out = _softmax_pv([s_prev, s_cur, s_next, s_ctx],
                              [kvp_ref[:, vs], kvc_ref[:, vs], kvn_ref[:, vs], kvx_ref[:, vs]],
                              _sink_column(sink_ref, kvh, BLOCK))
            _unstack_heads(out, o_ref.at[rows], kvh, BLOCK)


def _window_attn_call(q, kv, kv_ctx, sink):
    s = q.shape[0]
    c = kv_ctx.shape[0]
    nb = s // BLOCK
    qb = ATTN_Q_BLOCKS
    kblk = lambda off: pl.BlockSpec((BLOCK, 2 * KV_W), lambda i: (jnp.clip(i * qb + off, 0, nb - 1), 0))
    return pl.pallas_call(
        _window_attn_kernel,
        grid=(nb // qb,),
        in_specs=[
            pl.BlockSpec(memory_space=pltpu.SMEM),
            pl.BlockSpec((qb * BLOCK, ATTN_W), lambda i: (i, 0)),
            *[kblk(off) for off in range(-1, qb + 1)],
            pl.BlockSpec((c, 2 * KV_W), lambda i: (0, 0)),
        ],
        out_specs=pl.BlockSpec((qb * BLOCK, ATTN_W), lambda i: (i, 0)),
        out_shape=jax.ShapeDtypeStruct((s, ATTN_W), BF16),
        compiler_params=_params("arbitrary"),
        name="window_attn",
    )(sink, q, *([kv] * (qb + 2)), kv_ctx)


def _ctx_attn_kernel(sink_ref, q_ref, kv_ref, o_ref):
    rows = q_ref.shape[0]
    for kvh in range(N_KV_HEADS):
        q = _stack_heads(q_ref, kvh)
        out = _softmax_pv([_qk(q, kv_ref[:, _k_cols(kvh)])], [kv_ref[:, _v_cols(kvh)]],
                          _sink_column(sink_ref, kvh, rows))
        _unstack_heads(out, o_ref, kvh, rows)


def _ctx_attn_call(q, kv, sink):
    c = q.shape[0]
    return pl.pallas_call(
        _ctx_attn_kernel,
        grid=(1,),
        in_specs=[
            pl.BlockSpec(memory_space=pltpu.SMEM),
            pl.BlockSpec((c, ATTN_W), lambda i: (0, 0)),
            pl.BlockSpec((c, 2 * KV_W), lambda i: (0, 0)),
        ],
        out_specs=pl.BlockSpec((c, ATTN_W), lambda i: (0, 0)),
        out_shape=jax.ShapeDtypeStruct((c, ATTN_W), BF16),
        compiler_params=_params("arbitrary"),
        name="ctx_attn",
    )(sink, q, kv)


def _dft_tables(s_len):
    n1 = DFT_ROWS if s_len > 1024 else 1
    n2 = s_len // n1
    k1 = np.arange(n1)
    ang1 = 2.0 * np.pi * ((k1[:, None] * k1[None, :]) % n1) / n1
    w1 = np.concatenate([np.cos(ang1), -np.sin(ang1)], axis=0)
    w1 = np.kron(w1, np.eye(DFT_S2_GROUP))
    k2 = np.arange(n2)
    phase = (k2[None, None, :] * (k1[:, None, None] + n1 * k2[None, :, None])) % s_len
    ang2 = 2.0 * np.pi * phase / s_len
    gr, gi = np.cos(ang2), -np.sin(ang2)
    g = np.concatenate([np.concatenate([gr, -gi], axis=2),
                        np.concatenate([gi, gr], axis=2)], axis=1)
    if n1 == 1:
        g = g[:, :, :n2]
    c = np.arange(FOURIER_GROUP_W)
    angc = 2.0 * np.pi * ((c[:, None] * c[None, :]) % FOURIER_GROUP_W) / FOURIER_GROUP_W
    norm = 1.0 / np.sqrt(float(s_len) * FOURIER_GROUP_W)
    return (w1.astype(np.float32), g.astype(np.float32),
            (np.cos(angc) * norm).astype(np.float32), (np.sin(angc) * norm).astype(np.float32))


DFT_K1_GROUP = 8


def _dft_cols(g, x, cc_ref, sc_ref, o_ref, col0):
    y = jnp.dot(g, x, preferred_element_type=F32)
    n = g.shape[0] // 2
    yr, yi = y[:n].astype(BF16), y[n:].astype(BF16)
    for gi in range(N_FOURIER_GROUPS):
        sl = slice(gi * FOURIER_GROUP_W, (gi + 1) * FOURIER_GROUP_W)
        o = (jnp.dot(yr[:, sl], cc_ref[...], preferred_element_type=F32)
             + jnp.dot(yi[:, sl], sc_ref[...], preferred_element_type=F32))
        o_ref[:, col0 + gi * FOURIER_GROUP_W:col0 + (gi + 1) * FOURIER_GROUP_W] = o.astype(o_ref.dtype)


def _dft_seq_kernel(w_ref, u_ref, g_ref, cc_ref, sc_ref, o_ref, x1_ref):
    j = pl.program_id(0)
    n1, grp, cw = u_ref.shape
    n_a = x1_ref.shape[1] // grp
    kb = g_ref.shape[0]

    @pl.when(j < n_a)
    def _():
        y = jnp.dot(w_ref[...], u_ref[...].reshape(n1 * grp, cw), preferred_element_type=F32)
        s2 = pl.multiple_of(j * grp, grp)
        x1_ref[:, pl.ds(s2, grp), :] = y.astype(BF16).reshape(2 * n1, grp, cw)

    @pl.when(j >= n_a)
    def _():
        k0 = (j - n_a) * kb
        for b in range(kb):
            x = jnp.concatenate([x1_ref[k0 + b], x1_ref[n1 + k0 + b]], axis=0)
            _dft_cols(g_ref[b], x, cc_ref, sc_ref, o_ref, b * cw)


def _dft_seq_call(w1k, g, uf, cc, sc):
    s, cw = uf.shape
    n1 = DFT_ROWS
    n2 = s // n1
    grp, kb, gw = DFT_S2_GROUP, DFT_K1_GROUP, FOURIER_GROUP_W
    n_a, n_b = n2 // grp, n1 // kb
    stage2 = lambda j: jnp.maximum(j - n_a, 0)
    out = pl.pallas_call(
        _dft_seq_kernel,
        grid=(n_a + n_b,),
        in_specs=[pl.BlockSpec((2 * n1 * grp, n1 * grp), lambda j: (0, 0)),
                  pl.BlockSpec((n1, grp, cw), lambda j: (0, jnp.minimum(j, n_a - 1), 0)),
                  pl.BlockSpec((kb, 2 * n2, 2 * n2), lambda j: (stage2(j), 0, 0)),
                  pl.BlockSpec((gw, gw), lambda j: (0, 0)),
                  pl.BlockSpec((gw, gw), lambda j: (0, 0))],
        out_specs=pl.BlockSpec((n2, kb * cw), lambda j: (0, stage2(j))),
        out_shape=jax.ShapeDtypeStruct((n2, n1 * cw), BF16),
        scratch_shapes=[pltpu.VMEM((2 * n1, n2, cw), BF16)],
        compiler_params=_params("arbitrary"),
        name="dft_seq",
    )(w1k, uf.reshape(n1, n2, cw), g, cc, sc)
    return out.reshape(s, cw)


def _dft_ctx_kernel(g_ref, u_ref, cc_ref, sc_ref, o_ref):
    _dft_cols(g_ref[0], u_ref[...], cc_ref, sc_ref, o_ref, 0)


def _dft_small_call(g, uf, cc, sc):
    c, cw = uf.shape
    gw = FOURIER_GROUP_W
    return pl.pallas_call(
        _dft_ctx_kernel,
        grid=(1,),
        in_specs=[pl.BlockSpec((1, 2 * c, c), lambda k: (0, 0, 0)),
                  pl.BlockSpec((c, cw), lambda k: (0, 0)),
                  pl.BlockSpec((gw, gw), lambda k: (0, 0)),
                  pl.BlockSpec((gw, gw), lambda k: (0, 0))],
        out_specs=pl.BlockSpec((c, cw), lambda k: (0, 0)),
        out_shape=jax.ShapeDtypeStruct((c, cw), BF16),
        compiler_params=_params("arbitrary"),
        name="dft_ctx",
    )(g, uf, cc, sc)


def _merge_kernel(a_ref, f_ref, up_ref, upp_ref, upn_ref, ga_ref, gf_ref, gp_ref, x_ref,
                  gate_ref, ps_ref, wo_ref, wf_ref, wp_ref, wout_ref, o_ref, ext_ref, *, s_len):
    i = pl.program_id(0)
    tm = up_ref.shape[0]
    h = POOL_HALO
    ext_ref[0:h] = jnp.where(i > 0, upp_ref[...].astype(F32), 0.0)
    ext_ref[h:h + tm] = up_ref[...].astype(F32)
    ext_ref[h + tm:] = jnp.where(i < pl.num_programs(0) - 1, upn_ref[...].astype(F32), 0.0)

    t = i * tm + lax.broadcasted_iota(jnp.int32, (tm, 1), 0)
    pooled = []
    for gi, w in enumerate(POOL_WINDOWS):
        sl = slice(gi * POOL_GROUP_W, (gi + 1) * POOL_GROUP_W)
        total = ext_ref[h - w // 2:h - w // 2 + tm, sl]
        for off in range(-(w // 2) + 1, w - w // 2):
            total = total + ext_ref[h + off:h + off + tm, sl]
        count = jnp.minimum(t + (w - w // 2), s_len) - jnp.maximum(t - w // 2, 0)
        u = ext_ref[h:h + tm, sl]
        pooled.append((total / count.astype(F32) - u).astype(BF16))
    yp = jnp.concatenate(
        [jnp.dot(p, wp_ref[gi], preferred_element_type=F32) for gi, p in enumerate(pooled)], axis=-1)

    ya = jnp.dot(a_ref[...], wo_ref[...], preferred_element_type=F32)
    yf = jnp.dot(f_ref[...], wf_ref[...], preferred_element_type=F32)
    y = (jax.nn.sigmoid(ga_ref[...].astype(F32)) * ya
         + jax.nn.sigmoid(gf_ref[...].astype(F32)) * yf
         + jax.nn.sigmoid(gp_ref[...].astype(F32)) * (yp * ps_ref[...]))
    mix = jnp.dot(y.astype(BF16), wout_ref[...], preferred_element_type=F32)
    o_ref[...] = x_ref[...] + gate_ref[...] * mix


def _merge_call(a, f, up, g, x, gate, pool_scale, w_attn_o, w_fourier, w_pool, w_out, l, tm):
    s, d = x.shape
    h = POOL_HALO
    n_halo = s // h
    row = pl.BlockSpec((1, d), lambda i: (0, 0))
    once = dict(pipeline_mode=pl.Buffered(1))
    return pl.pallas_call(
        functools.partial(_merge_kernel, s_len=s),
        grid=(s // tm,),
        in_specs=[
            pl.BlockSpec((tm, ATTN_W), lambda i: (i, 0)),
            pl.BlockSpec((tm, FOURIER_W), lambda i: (i, 0)),
            pl.BlockSpec((tm, POOL_W), lambda i: (i, 0)),
            pl.BlockSpec((h, POOL_W), lambda i: (jnp.maximum(i * (tm // h) - 1, 0), 0)),
            pl.BlockSpec((h, POOL_W), lambda i: (jnp.minimum((i + 1) * (tm // h), n_halo - 1), 0)),
            pl.BlockSpec((tm, d), lambda i: (i, 0)),
            pl.BlockSpec((tm, d), lambda i: (i, 1)),
            pl.BlockSpec((tm, d), lambda i: (i, 2)),
            pl.BlockSpec((tm, d), lambda i: (i, 0)),
            row, row,
            pl.BlockSpec((None, ATTN_W, d), lambda i: (l, 0, 0), **once),
            pl.BlockSpec((None, FOURIER_W, d), lambda i: (l, 0, 0), **once),
            pl.BlockSpec((None, len(POOL_WINDOWS), POOL_GROUP_W, POOL_OUT_GROUP_W),
                         lambda i: (l, 0, 0, 0), **once),
            pl.BlockSpec((None, d, d), lambda i: (l, 0, 0), **once),
        ],
        out_specs=pl.BlockSpec((tm, d), lambda i: (i, 0)),
        out_shape=jax.ShapeDtypeStruct((s, d), F32),
        scratch_shapes=[pltpu.VMEM((tm + 2 * h, POOL_W), F32)],
        compiler_params=_params("arbitrary"),
        name="merge",
    )(a, f, up, up, up, g, g, g, x, gate, pool_scale, w_attn_o, w_fourier, w_pool, w_out)


def _rope_tables(s_len):
    rows_n = s_len // GRID_W
    n_freq = D_HEAD // 4
    freqs = ROPE_BASE ** (-jnp.arange(n_freq, dtype=F32) / n_freq)
    ar = jnp.arange(rows_n).astype(F32)[:, None] * freqs
    ac = jnp.arange(GRID_W).astype(F32)[:, None] * freqs
    by_row = lambda t: jnp.repeat(t, GRID_W, axis=0)
    by_col = lambda t: jnp.tile(t, (rows_n, 1))
    cr, sr, cc, sc = by_row(jnp.cos(ar)), by_row(jnp.sin(ar)), by_col(jnp.cos(ac)), by_col(jnp.sin(ac))
    cos = jnp.concatenate([cr, cr, cc, cc], axis=-1)
    sin = jnp.concatenate([-sr, sr, -sc, sc], axis=-1)
    return cos, sin


def kernel(x, c, ctx, c_ctx, w_ada, b_ada, norm_w, ffn_w_gate, ffn_w_up, ffn_w_down,
           w_in, q_gain, k_gain, sink, w_attn_o, w_fourier, w_pool, pool_scale, w_out):
    b, s_len, d = x.shape
    c_len = ctx.shape[1]
    assert b == 1 and d == D_MODEL and s_len % (DFT_ROWS * BLOCK) == 0
    x, ctx = x[0], ctx[0]

    cond = jnp.zeros((8, d), F32).at[0].set(c[0]).at[1].set(c_ctx)
    mod = _ada_call(cond, w_ada, b_ada).reshape(DEPTH, 8, N_MOD, 1, d)

    ffn_f32 = (ffn_w_gate, ffn_w_up, ffn_w_down)
    ffn_w = (ffn_f32, 0, 0)
    w_ao, w_fo = w_attn_o.astype(BF16), w_fourier.astype(BF16)
    w_po, w_ou = w_pool.astype(BF16), w_out.astype(BF16)

    cos_x, sin_x = _rope_tables(s_len)
    cos_c, sin_c = jnp.ones((c_len, D_HEAD), F32), jnp.zeros((c_len, D_HEAD), F32)
    w1, g_x, cc, sc = (jnp.asarray(t).astype(BF16) for t in _dft_tables(s_len))
    _, g_c, cc_c, sc_c = (jnp.asarray(t).astype(BF16) for t in _dft_tables(c_len))

    tm_x, tm_m = ROW_TILE, MERGE_ROW_TILE
    for l in range(DEPTH):
        last = l == DEPTH - 1
        mx, mc = mod[l, 0], mod[l, 1]
        nw = norm_w[l][:, None, :]
        qg, kg = q_gain[l][None, :], k_gain[l][None, :]
        ps = pool_scale[l][None, :]

        x, ffn_w_post, w_ab = _ffn_call(x, nw[0], mx[0], mx[1], mx[2], ffn_w, tm_x, (ffn_f32, l, 1), (w_in, l))
        ctx, _, _ = _ffn_call(ctx, nw[0], mc[0], mc[1], mc[2], ffn_w, c_len)

        qx, kvx, ufx, upx, gx = _inproj_call(x, nw[1], mx[3], mx[4], qg, kg, cos_x, sin_x, w_ab, tm_x)
        qc, kvc, ufc, upc, gc = _inproj_call(ctx, nw[1], mc[3], mc[4], qg, kg, cos_c, sin_c, w_ab, c_len)

        ax = _window_attn_call(qx, kvx, kvc, sink[l])
        fx = _dft_seq_call(w1, g_x, ufx, cc, sc)
        x = _merge_call(ax, fx, upx, gx, x, mx[5], ps, w_ao, w_fo, w_po, w_ou, l, tm_m)
        if not last:
            ac = _ctx_attn_call(qc, kvc, sink[l])
            fc = _dft_small_call(g_c, ufc, cc_c, sc_c)
            ctx = _merge_call(ac, fc, upc, gc, ctx, mc[5], ps, w_ao, w_fo, w_po, w_ou, l, c_len)

        x, ffn_w_next, _ = _ffn_call(x, nw[2], mx[6], mx[7], mx[8], ffn_w_post, tm_x,
                                     None if last else (ffn_f32, l + 1, 0))
        if not last:
            ctx, _, _ = _ffn_call(ctx, nw[2], mc[6], mc[7], mc[8], ffn_w_post, c_len)
        ffn_w = ffn_w_next
    return x[None]
```
